```python
import math
import jax
import jax.numpy as jnp
from jax import lax
import numpy as np

D_MODEL = 1024
BATCH = 8
SEQ = 8192
DEPTH = 2
DEC_BATCH = 16
DEC_SEQ = 64
PAST_LEN = 1024

CHUNK = 64
EPS = 1e-6
NEG = -1e30
ROPE_THETA = 10000.0
Q_BLOCK = 128
DH_A = 64
H_A = D_MODEL // (2 * DH_A)
A_WIDTH = H_A * 2 * DH_A
DH_B = 64
H_B = D_MODEL // DH_B
B_WIDTH = H_B * DH_B
PREV_CHUNKS = 8
BAND_PAST = PREV_CHUNKS * CHUNK
BAND_KEYS = BAND_PAST + CHUNK
REL_CLIP = 128
PEER_HEADS = 8
N_KEYS = 128
N_EXPERTS = N_KEYS * N_KEYS
D_KEY = 256
D_KEY_HALF = D_KEY // 2
TOPK_HALF = 16
TOPK = 16
PEER_BLOCK = 256
N_A = (DEPTH + 1) // 2
N_B = DEPTH // 2

kernel_name = "hybrid_diffattn_bandattn_peer_stream_step"


def rmsnorm(x, g):
    xf = x.astype(jnp.float32)
    y = xf * lax.rsqrt(jnp.mean(xf * xf, axis=-1, keepdims=True) + EPS)
    return (y * g.astype(jnp.float32)).astype(x.dtype)


def rope(x, pos):
    half = x.shape[-1] // 2
    inv = ROPE_THETA ** (-jnp.arange(half, dtype=jnp.float32) / half)
    ang = pos.astype(jnp.float32)[:, None] * inv[None, :]
    shp = (pos.shape[0],) + (1,) * (x.ndim - 3) + (half,)
    cos = jnp.cos(ang).reshape(shp)
    sin = jnp.sin(ang).reshape(shp)
    xf = x.astype(jnp.float32)
    x1, x2 = xf[..., :half], xf[..., half:]
    return jnp.concatenate([x1 * cos - x2 * sin, x2 * cos + x1 * sin], axis=-1).astype(x.dtype)


def ada_mod(c, w, b):
    m = jax.nn.silu(c) @ w + b
    return m.reshape(c.shape[0], 6, 1, c.shape[-1])


def modulate(x, g, shift, scale):
    return rmsnorm(x, g) * (1.0 + scale) + shift


def diff_lambda(lq1, lk1, lq2, lk2, lam_init):
    f = lambda a, b: jnp.exp(jnp.sum(a.astype(jnp.float32) * b.astype(jnp.float32)))
    return f(lq1, lk1) - f(lq2, lk2) + lam_init


def diff_qkv(h, pos, w_in, g_q, g_k):
    B, T, _ = h.shape
    q, k, v = jnp.split(h @ w_in, 3, axis=-1)
    q = rope(rmsnorm(q.reshape(B, T, H_A, 2, DH_A), g_q), pos)
    k = rope(rmsnorm(k.reshape(B, T, H_A, 2, DH_A), g_k), pos)
    v = v.reshape(B, T, H_A, 2 * DH_A)
    return q, k, v


def diff_out(o, g_sub, lam_init, w_out):
    B, T = o.shape[0], o.shape[1]
    o = rmsnorm(o, g_sub) * (1.0 - lam_init)
    return o.reshape(B, T, A_WIDTH) @ w_out


def diff_attn_prompt(h, w_in, g_q, g_k, lam, g_sub, w_out, lam_init):
    B, S, _ = h.shape
    pos = jnp.arange(S)
    q, k, v = diff_qkv(h, pos, w_in, g_q, g_k)
    nqb = S // Q_BLOCK
    qb = jnp.moveaxis(q.reshape(B, nqb, Q_BLOCK, H_A, 2, DH_A), 1, 0)
    kchunk = pos // CHUNK
    vf = v.astype(jnp.float32)
    scale = DH_A ** -0.5

    def block(args):
        qi, bi = args
        s = jnp.einsum('bqhcd,bkhcd->bhcqk', qi, k).astype(jnp.float32) * scale
        qchunk = (bi * Q_BLOCK + jnp.arange(Q_BLOCK)) // CHUNK
        mask = kchunk[None, :] <= qchunk[:, None]
        s = jnp.where(mask, s, NEG)
        p = jax.nn.softmax(s, axis=-1)
        a = p[:, :, 0] - lam * p[:, :, 1]
        return jnp.einsum('bhqk,bkhe->bqhe', a, vf)

    o = lax.map(block, (qb, jnp.arange(nqb)))
    o = jnp.moveaxis(o, 0, 1).reshape(B, S, H_A, 2 * DH_A).astype(h.dtype)
    return diff_out(o, g_sub, lam_init, w_out), k, v


def diff_attn_sample(h, cache_k, cache_v, w_in, g_q, g_k, lam, g_sub, w_out, lam_init):
    T = h.shape[1]
    pos = PAST_LEN + jnp.arange(T)
    q, k, v = diff_qkv(h, pos, w_in, g_q, g_k)
    k_all = jnp.concatenate([cache_k.astype(k.dtype), k], axis=1)
    v_all = jnp.concatenate([cache_v.astype(v.dtype), v], axis=1)
    s = jnp.einsum('bqhcd,bkhcd->bhcqk', q, k_all).astype(jnp.float32) * (DH_A ** -0.5)
    p = jax.nn.softmax(s, axis=-1)
    a = p[:, :, 0] - lam * p[:, :, 1]
    o = jnp.einsum('bhqk,bkhe->bqhe', a, v_all.astype(jnp.float32)).astype(h.dtype)
    return diff_out(o, g_sub, lam_init, w_out), k, v


def band_qkv(h, w_in, g_q, g_k):
    B, T, _ = h.shape
    q, k, v = jnp.split(h @ w_in, 3, axis=-1)
    q = rmsnorm(q.reshape(B, T, H_B, DH_B), g_q)
    k = rmsnorm(k.reshape(B, T, H_B, DH_B), g_k)
    v = v.reshape(B, T, H_B, DH_B)
    return q, k, v


def rel_bias(table, rel):
    idx = jnp.clip(rel, -REL_CLIP, REL_CLIP) + REL_CLIP
    return table[:, idx].astype(jnp.float32)


def band_attn_prompt(h, w_in, g_q, g_k, table, w_out):
    B, S, _ = h.shape
    nC = S // CHUNK
    q, k, v = band_qkv(h, w_in, g_q, g_k)
    pad = ((0, 0), (BAND_PAST, 0), (0, 0), (0, 0))
    kpad, vpad = jnp.pad(k, pad), jnp.pad(v, pad)
    band_idx = (jnp.arange(nC) * CHUNK)[:, None] + jnp.arange(BAND_KEYS)[None, :]
    valid = band_idx >= BAND_PAST
    rel = BAND_PAST + jnp.arange(CHUNK)[:, None] - jnp.arange(BAND_KEYS)[None, :]
    bias = rel_bias(table, rel)
    scale = DH_B ** -0.5

    def one_seq(args):
        qs, ks, vs = args
        qc = qs.reshape(nC, CHUNK, H_B, DH_B)
        kb, vb = ks[band_idx], vs[band_idx]
        s = jnp.einsum('cqhd,ckhd->chqk', qc, kb).astype(jnp.float32) * scale + bias[None]
        s = jnp.where(valid[:, None, None, :], s, NEG)
        p = jax.nn.softmax(s, axis=-1)
        o = jnp.einsum('chqk,ckhd->cqhd', p, vb.astype(jnp.float32))
        return o.reshape(S, B_WIDTH)

    o = lax.map(one_seq, (q, kpad, vpad)).astype(h.dtype)
    n_keep = min(BAND_PAST, S)
    return o @ w_out, k[:, S - n_keep:], v[:, S - n_keep:]


def band_attn_sample(h, cache_k, cache_v, w_in, g_q, g_k, table, w_out):
    B, T, _ = h.shape
    L = cache_k.shape[1]
    q, k, v = band_qkv(h, w_in, g_q, g_k)
    k_all = jnp.concatenate([cache_k.astype(k.dtype), k], axis=1)
    v_all = jnp.concatenate([cache_v.astype(v.dtype), v], axis=1)
    rel = L + jnp.arange(T)[:, None] - jnp.arange(L + T)[None, :]
    s = jnp.einsum('bqhd,bkhd->bhqk', q, k_all).astype(jnp.float32) * (DH_B ** -0.5) + rel_bias(table, rel)[None]
    p = jax.nn.softmax(s, axis=-1)
    o = jnp.einsum('bhqk,bkhd->bqhd', p, v_all.astype(jnp.float32)).reshape(B, T, B_WIDTH).astype(h.dtype)
    return o @ w_out, k_all[:, T:], v_all[:, T:]


def peer(h, w_q, sub_keys, u_tab, v_tab):
    B, T, D = h.shape
    flat = h.reshape(B * T, D)
    n = flat.shape[0]
    n_pad = (-n) % PEER_BLOCK
    flat = jnp.pad(flat, ((0, n_pad), (0, 0)))
    blocks = flat.reshape(-1, PEER_BLOCK, D)

    def one(xb):
        q = (xb @ w_q).reshape(PEER_BLOCK, PEER_HEADS, 2, D_KEY_HALF)
        s = jnp.einsum('thcd,hcnd->thcn', q, sub_keys).astype(jnp.float32)
        s1, i1 = lax.top_k(s[:, :, 0], TOPK_HALF)
        s2, i2 = lax.top_k(s[:, :, 1], TOPK_HALF)
        cand = (s1[..., :, None] + s2[..., None, :]).reshape(PEER_BLOCK, PEER_HEADS, TOPK_HALF * TOPK_HALF)
        cidx = (i1[..., :, None] * N_KEYS + i2[..., None, :]).reshape(PEER_BLOCK, PEER_HEADS, TOPK_HALF * TOPK_HALF)
        top_s, sel = lax.top_k(cand, TOPK)
        e = jnp.take_along_axis(cidx, sel, axis=-1)
        g = jax.nn.softmax(top_s, axis=-1)
        hid = jax.nn.gelu(jnp.einsum('td,thkd->thk', xb, u_tab[e]).astype(jnp.float32), approximate=False)
        out = jnp.einsum('thk,thkd->td', g * hid, v_tab[e].astype(jnp.float32))
        return out.astype(xb.dtype)

    out = lax.map(one, blocks).reshape(-1, D)[:n]
    return out.reshape(B, T, D)


def setup_inputs(seed: int = 0) -> dict:
    key = jax.random.key(seed)
    ks = jax.random.split(key, 32)

    def nrm(k, shape, scale):
        return scale * jax.random.normal(k, shape, jnp.float32)

    band_len = min(BAND_PAST, PAST_LEN)
    sd = D_MODEL ** -0.5
    return {
        "x_prompt": nrm(ks[0], (BATCH, SEQ, D_MODEL), 1.0),
        "x_sample": nrm(ks[1], (DEC_BATCH, DEC_SEQ, D_MODEL), 1.0),
        "c_prompt": nrm(ks[2], (BATCH, D_MODEL), 1.0),
        "c_sample": nrm(ks[3], (DEC_BATCH, D_MODEL), 1.0),
        "cache_a_k": nrm(ks[4], (N_A, DEC_BATCH, PAST_LEN, H_A, 2, DH_A), 1.0),
        "cache_a_v": nrm(ks[5], (N_A, DEC_BATCH, PAST_LEN, H_A, 2 * DH_A), 1.0),
        "cache_b_k": nrm(ks[6], (N_B, DEC_BATCH, band_len, H_B, DH_B), 1.0),
        "cache_b_v": nrm(ks[7], (N_B, DEC_BATCH, band_len, H_B, DH_B), 1.0),
        "ada_w": nrm(ks[8], (DEPTH, D_MODEL, 6 * D_MODEL), 0.5 * sd),
        "ada_b": nrm(ks[9], (DEPTH, 6 * D_MODEL), 0.02),
        "norm_g": 1.0 + nrm(ks[10], (DEPTH, 2, D_MODEL), 0.02),
        "a_w_in": nrm(ks[11], (N_A, D_MODEL, 3 * A_WIDTH), sd),
        "a_g_q": 1.0 + nrm(ks[12], (N_A, DH_A), 0.02),
        "a_g_k": 1.0 + nrm(ks[13], (N_A, DH_A), 0.02),
        "a_lq1": nrm(ks[14], (N_A, DH_A), 0.1),
        "a_lk1": nrm(ks[15], (N_A, DH_A), 0.1),
        "a_lq2": nrm(ks[16], (N_A, DH_A), 0.1),
        "a_lk2": nrm(ks[17], (N_A, DH_A), 0.1),
        "a_g_sub": 1.0 + nrm(ks[18], (N_A, 2 * DH_A), 0.02),
        "a_w_out": nrm(ks[19], (N_A, A_WIDTH, D_MODEL), A_WIDTH ** -0.5),
        "b_w_in": nrm(ks[20], (N_B, D_MODEL, 3 * B_WIDTH), sd),
        "b_g_q": 1.0 + nrm(ks[21], (N_B, DH_B), 0.02),
        "b_g_k": 1.0 + nrm(ks[22], (N_B, DH_B), 0.02),
        "b_rel_bias": nrm(ks[23], (N_B, H_B, 2 * REL_CLIP + 1), 0.5),
        "b_w_out": nrm(ks[24], (N_B, B_WIDTH, D_MODEL), B_WIDTH ** -0.5),
        "peer_w_q": nrm(ks[25], (DEPTH, D_MODEL, PEER_HEADS * D_KEY), sd),
        "peer_sub_keys": nrm(ks[26], (DEPTH, PEER_HEADS, 2, N_KEYS, D_KEY_HALF), D_KEY_HALF ** -0.5),
        "peer_u": nrm(ks[27], (DEPTH, N_EXPERTS, D_MODEL), sd),
        "peer_v": nrm(ks[28], (DEPTH, N_EXPERTS, D_MODEL), 1.0),
    }


def reference(x_prompt, x_sample, c_prompt, c_sample, cache_a_k, cache_a_v, cache_b_k, cache_b_v,
              ada_w, ada_b, norm_g, a_w_in, a_g_q, a_g_k, a_lq1, a_lk1, a_lq2, a_lk2, a_g_sub, a_w_out,
              b_w_in, b_g_q, b_g_k, b_rel_bias, b_w_out, peer_w_q, peer_sub_keys, peer_u, peer_v):
    xp, xs = x_prompt, x_sample
    ak_p, av_p, ak_s, av_s = [], [], [], []
    bk_p, bv_p, bk_s, bv_s = [], [], [], []
    for i in range(DEPTH):
        j = i // 2
        mp = ada_mod(c_prompt, ada_w[i], ada_b[i])
        ms = ada_mod(c_sample, ada_w[i], ada_b[i])
        hp = modulate(xp, norm_g[i, 0], mp[:, 0], mp[:, 1])
        hs = modulate(xs, norm_g[i, 0], ms[:, 0], ms[:, 1])
        if i % 2 == 0:
            lam_init = 0.8 - 0.6 * math.exp(-0.3 * i)
            lam = diff_lambda(a_lq1[j], a_lk1[j], a_lq2[j], a_lk2[j], lam_init)
            yp, kp, vp = diff_attn_prompt(hp, a_w_in[j], a_g_q[j], a_g_k[j], lam, a_g_sub[j], a_w_out[j], lam_init)
            ys, ks_, vs_ = diff_attn_sample(hs, cache_a_k[j], cache_a_v[j], a_w_in[j], a_g_q[j], a_g_k[j],
                                            lam, a_g_sub[j], a_w_out[j], lam_init)
            ak_p.append(kp); av_p.append(vp); ak_s.append(ks_); av_s.append(vs_)
        else:
            yp, kp, vp = band_attn_prompt(hp, b_w_in[j], b_g_q[j], b_g_k[j], b_rel_bias[j], b_w_out[j])
            ys, ks_, vs_ = band_attn_sample(hs, cache_b_k[j], cache_b_v[j], b_w_in[j], b_g_q[j], b_g_k[j],
                                            b_rel_bias[j], b_w_out[j])
            bk_p.append(kp); bv_p.append(vp); bk_s.append(ks_); bv_s.append(vs_)
        xp = xp + mp[:, 2] * yp
        xs = xs + ms[:, 2] * ys
        hp = modulate(xp, norm_g[i, 1], mp[:, 3], mp[:, 4])
        hs = modulate(xs, norm_g[i, 1], ms[:, 3], ms[:, 4])
        xp = xp + mp[:, 5] * peer(hp, peer_w_q[i], peer_sub_keys[i], peer_u[i], peer_v[i])
        xs = xs + ms[:, 5] * peer(hs, peer_w_q[i], peer_sub_keys[i], peer_u[i], peer_v[i])
    return (xp, xs,
            jnp.stack(ak_p, 0), jnp.stack(av_p, 0), jnp.stack(ak_s, 0), jnp.stack(av_s, 0),
            jnp.stack(bk_p, 0), jnp.stack(bv_p, 0), jnp.stack(bk_s, 0), jnp.stack(bv_s, 0))
```

```python
import functools
import math

import numpy as np
import jax
import jax.numpy as jnp
from jax import lax
from jax.experimental import pallas as pl
from jax.experimental.pallas import tpu as pltpu

F32 = jnp.float32
BF16 = jnp.bfloat16

D_MODEL = 1024
CHUNK = 64
EPS = 1e-6
NEG = -1e30
ROPE_THETA = 10000.0
DH = 64
LANES = 128
PREV_CHUNKS = 8
BAND_PAST = PREV_CHUNKS * CHUNK
REL_CLIP = 128
PEER_HEADS = 8
N_KEYS = 128
TOPK = 16
VMEM_LIMIT = 48 * 1024 * 1024

TM_PROJ = 256
TQ_FLASH = 512
TQ_BAND = 256
TM_DENSE = 512
DENSE_PARTS = 2
E_TILE = 2 * N_KEYS

_NT = (((1,), (1,)), ((), ()))


def _cparams(sem):
    return pltpu.CompilerParams(dimension_semantics=sem, vmem_limit_bytes=VMEM_LIMIT)


def _dot(a, b):
    return jnp.dot(a, b, preferred_element_type=F32)


def _dot_nt(a, b):
    return lax.dot_general(a, b, _NT, preferred_element_type=F32)


def _split(a):
    hi = a.astype(BF16)
    lo = (a - hi.astype(F32)).astype(BF16)
    return hi, lo


def _ada_kernel(c_ref, w_ref, b_ref, o_ref):
    c = c_ref[...]
    a = c * (1.0 / (1.0 + jnp.exp(-c)))
    ah, al = _split(a)
    wh, wl = _split(w_ref[...])
    o_ref[...] = _dot(ah, wh) + _dot(al, wh) + _dot(ah, wl) + b_ref[...]


def _ada_mod(c_all, w, b):
    n, d = c_all.shape
    nout = w.shape[1]
    tn = 512
    return pl.pallas_call(
        _ada_kernel,
        grid=(nout // tn,),
        in_specs=[pl.BlockSpec((n, d), lambda j: (0, 0)),
                  pl.BlockSpec((d, tn), lambda j: (0, j)),
                  pl.BlockSpec((1, tn), lambda j: (0, j))],
        out_specs=pl.BlockSpec((n, tn), lambda j: (0, j)),
        out_shape=jax.ShapeDtypeStruct((n, nout), F32),
        compiler_params=_cparams(("parallel",)),
        name="ada_mod",
    )(c_all, w, b.reshape(1, nout))


class _Mod:
    def __init__(self, arr, per_row, tiles_per_batch):
        self.arr, self.per_row, self.tpb = arr, per_row, tiles_per_batch

    def spec(self, k, tm):
        if self.per_row:
            return pl.BlockSpec((None, tm, D_MODEL), lambda i, *_: (k, i, 0))
        tpb = self.tpb(tm)
        return pl.BlockSpec((None, 1, D_MODEL), lambda i, *_: ((i // tpb) * 6 + k, 0, 0))


def _modulated(x, ng, scale, shift):
    ms = jnp.mean(x * x, axis=-1, keepdims=True)
    return (x * lax.rsqrt(ms + EPS) * ng) * (1.0 + scale) + shift


def _inproj_kernel(x_ref, sh_ref, sc_ref, ng_ref, w_ref, gq_ref, gk_ref, cos_ref, sin_ref,
                   q_ref, k_ref, v_ref, *, rope, qscale):
    tm = x_ref.shape[0]
    hb = _modulated(x_ref[...], ng_ref[...], sc_ref[...], sh_ref[...]).astype(BF16)
    lane = lax.broadcasted_iota(jnp.int32, (tm, LANES), 1)
    lo = lane < DH
    swap_sel = (lane & (DH // 2)) != 0

    def norm_rope(xb, g):
        x2 = xb * xb
        slo = jnp.sum(jnp.where(lo, x2, 0.0), axis=-1, keepdims=True)
        shi = jnp.sum(jnp.where(lo, 0.0, x2), axis=-1, keepdims=True)
        ms = jnp.where(lo, slo, shi) * (1.0 / DH)
        y = xb * lax.rsqrt(ms + EPS) * g
        if rope:
            sw = jnp.where(swap_sel, pltpu.roll(y, DH // 2, 1), pltpu.roll(y, LANES - DH // 2, 1))
            y = y * cos_ref[...] + sw * sin_ref[...]
        return y

    nblk = w_ref.shape[1] // (2 * LANES)
    for j in range(nblk):
        acc = _dot(hb, w_ref[:, j * 2 * LANES:(j + 1) * 2 * LANES])
        for half in range(2):
            blk = acc[:, half * LANES:(half + 1) * LANES]
            col = j * 2 * LANES + half * LANES
            if col < D_MODEL:
                q_ref[:, col:col + LANES] = (norm_rope(blk, gq_ref[...]) * qscale).astype(BF16)
            elif col < 2 * D_MODEL:
                k_ref[:, col - D_MODEL:col - D_MODEL + LANES] = norm_rope(blk, gk_ref[...])
            else:
                v_ref[:, col - 2 * D_MODEL:col - 2 * D_MODEL + LANES] = blk


def _inproj(x, mod, ng, w_bf, gq, gk, cos, sin, *, rope, tm):
    t = x.shape[0]
    nrep = cos.shape[0] // tm
    row = lambda i: (i, 0)
    full = lambda i: (0, 0)
    tab = lambda i: (i % nrep, 0)
    return pl.pallas_call(
        functools.partial(_inproj_kernel, rope=rope, qscale=DH ** -0.5),
        grid=(t // tm,),
        in_specs=[pl.BlockSpec((tm, D_MODEL), row), mod.spec(0, tm), mod.spec(1, tm),
                  pl.BlockSpec((1, D_MODEL), full),
                  pl.BlockSpec(w_bf.shape, full),
                  pl.BlockSpec((1, LANES), full), pl.BlockSpec((1, LANES), full),
                  pl.BlockSpec((tm, LANES), tab), pl.BlockSpec((tm, LANES), tab)],
        out_specs=[pl.BlockSpec((tm, D_MODEL), row)] * 3,
        out_shape=[jax.ShapeDtypeStruct((t, D_MODEL), BF16),
                   jax.ShapeDtypeStruct((t, D_MODEL), F32),
                   jax.ShapeDtypeStruct((t, D_MODEL), F32)],
        compiler_params=_cparams(("parallel",)),
        name="qkv_proj",
    )(x, mod.arr, mod.arr, ng, w_bf, gq, gk, cos, sin)


def _diff_lambda(lamv_ref, lam_init):
    lv = lamv_ref[...]
    e1 = jnp.exp(jnp.sum(lv[0:1] * lv[1:2], axis=-1, keepdims=True))
    e2 = jnp.exp(jnp.sum(lv[2:3] * lv[3:4], axis=-1, keepdims=True))
    return e1 - e2 + lam_init


def _diff_finish(o0, o1, lamv_ref, gsub_ref, lam_init):
    o = o0 - _diff_lambda(lamv_ref, lam_init) * o1
    ms = jnp.mean(o * o, axis=-1, keepdims=True)
    return ((o * lax.rsqrt(ms + EPS) * gsub_ref[...]) * (1.0 - lam_init)).astype(BF16)


def _component_queries(q):
    lane = lax.broadcasted_iota(jnp.int32, q.shape, 1)
    zero = jnp.zeros_like(q)
    return jnp.where(lane < DH, q, zero), jnp.where(lane < DH, zero, q)


def _flash_diff_kernel(qt_ref, kt_ref, q_ref, k_ref, v_ref, lamv_ref, gsub_ref, o_ref,
                       m_ref, l_ref, a_ref, *, lam_init):
    p = pl.program_id(2)
    qi = qt_ref[p]
    ki = kt_ref[p]
    tq = q_ref.shape[0]

    @pl.when(ki == 0)
    def _():
        m_ref[...] = jnp.full(m_ref.shape, NEG, F32)
        l_ref[...] = jnp.zeros(l_ref.shape, F32)
        a_ref[...] = jnp.zeros(a_ref.shape, F32)

    def update(masked):
        kb = k_ref[...].astype(BF16)
        vb = v_ref[...].astype(BF16)
        if masked:
            row = lax.broadcasted_iota(jnp.int32, (tq, tq), 0)
            col = lax.broadcasted_iota(jnp.int32, (tq, tq), 1)
            visible = (col // CHUNK) <= (row // CHUNK)
        for c, qc in enumerate(_component_queries(q_ref[...])):
            s = _dot_nt(qc, kb)
            if masked:
                s = jnp.where(visible, s, NEG)
            m_prev = m_ref[c]
            m_new = jnp.maximum(m_prev, jnp.max(s, axis=-1, keepdims=True))
            alpha = jnp.exp(m_prev - m_new)
            pm = jnp.exp(s - m_new)
            l_ref[c] = alpha * l_ref[c] + jnp.sum(pm, axis=-1, keepdims=True)
            a_ref[c] = alpha * a_ref[c] + _dot(pm.astype(BF16), vb)
            m_ref[c] = m_new

    @pl.when(ki < qi)
    def _():
        update(False)

    @pl.when(ki == qi)
    def _():
        update(True)
        o_ref[...] = _diff_finish(a_ref[0] / l_ref[0], a_ref[1] / l_ref[1], lamv_ref, gsub_ref, lam_init)


def _flash_diff(q, k, v, lamv, gsub, *, batch, seq, lam_init):
    tq = min(TQ_FLASH, seq)
    nq = seq // tq
    heads = D_MODEL // LANES
    pairs = [(a, b) for a in range(nq) for b in range(a + 1)]
    qt = jnp.asarray([a for a, _ in pairs], jnp.int32)
    kt = jnp.asarray([b for _, b in pairs], jnp.int32)
    qmap = lambda b, h, p, qt, kt: (b * nq + qt[p], h)
    kmap = lambda b, h, p, qt, kt: (b * nq + kt[p], h)
    return pl.pallas_call(
        functools.partial(_flash_diff_kernel, lam_init=lam_init),
        grid_spec=pltpu.PrefetchScalarGridSpec(
            num_scalar_prefetch=2,
            grid=(batch, heads, len(pairs)),
            in_specs=[pl.BlockSpec((tq, LANES), qmap),
                      pl.BlockSpec((tq, LANES), kmap),
                      pl.BlockSpec((tq, LANES), kmap),
                      pl.BlockSpec(lamv.shape, lambda *_: (0, 0)),
                      pl.BlockSpec((1, LANES), lambda *_: (0, 0))],
            out_specs=pl.BlockSpec((tq, LANES), qmap),
            scratch_shapes=[pltpu.VMEM((2, tq, 1), F32), pltpu.VMEM((2, tq, 1), F32),
                            pltpu.VMEM((2, tq, LANES), F32)]),
        out_shape=jax.ShapeDtypeStruct(q.shape, BF16),
        compiler_params=_cparams(("parallel", "parallel", "arbitrary")),
        name="flash_diff_attn",
    )(qt, kt, q, k, v, lamv, gsub)


def _joint_softmax_attend(qc, pieces):
    ss = []
    for kb, _, bias, visible in pieces:
        s = _dot_nt(qc, kb)
        if bias is not None:
            s = s + bias
        if visible is not None:
            s = jnp.where(visible, s, NEG)
        ss.append(s)
    m = functools.reduce(jnp.maximum, [jnp.max(s, axis=-1, keepdims=True) for s in ss])
    l = 0.0
    o = 0.0
    for s, (_, vb, _, _) in zip(ss, pieces):
        pm = jnp.exp(s - m)
        l = l + jnp.sum(pm, axis=-1, keepdims=True)
        o = o + _dot(pm.astype(BF16), vb)
    return o / l


def _diff_sample_kernel(q_ref, ck_ref, cv_ref, nk_ref, nv_ref, lamv_ref, gsub_ref, o_ref, *, lam_init):
    pieces = [(ck_ref[...].astype(BF16), cv_ref[...].astype(BF16), None, None),
              (nk_ref[...].astype(BF16), nv_ref[...].astype(BF16), None, None)]
    o0, o1 = [_joint_softmax_attend(qc, pieces) for qc in _component_queries(q_ref[...])]
    o_ref[...] = _diff_finish(o0, o1, lamv_ref, gsub_ref, lam_init)


def _diff_sample(q, ck, cv, nk, nv, lamv, gsub, *, batch, t_new, past, lam_init):
    heads = D_MODEL // LANES
    bh = lambda b, h: (b, h)
    return pl.pallas_call(
        functools.partial(_diff_sample_kernel, lam_init=lam_init),
        grid=(batch, heads),
        in_specs=[pl.BlockSpec((t_new, LANES), bh),
                  pl.BlockSpec((past, LANES), bh), pl.BlockSpec((past, LANES), bh),
                  pl.BlockSpec((t_new, LANES), bh), pl.BlockSpec((t_new, LANES), bh),
                  pl.BlockSpec(lamv.shape, lambda b, h: (0, 0)),
                  pl.BlockSpec((1, LANES), lambda b, h: (0, 0))],
        out_specs=pl.BlockSpec((t_new, LANES), bh),
        out_shape=jax.ShapeDtypeStruct(q.shape, BF16),
        compiler_params=_cparams(("parallel", "parallel")),
        name="diff_attn_sample",
    )(q, ck, cv, nk, nv, lamv, gsub)


def _band_finish(o_even, o_odd):
    lane = lax.broadcasted_iota(jnp.int32, o_even.shape, 1)
    return jnp.where(lane < DH, o_even, o_odd).astype(BF16)


def _band_prompt_kernel(q_ref, k0_ref, k1_ref, k2_ref, v0_ref, v1_ref, v2_ref, bias_ref, o_ref):
    qi = pl.program_id(2)
    tq = q_ref.shape[0]
    row_c = lax.broadcasted_iota(jnp.int32, (tq, tq), 0) // CHUNK
    col_c = lax.broadcasted_iota(jnp.int32, (tq, tq), 1) // CHUNK
    npiece = 3
    outs = []
    for head, qc in enumerate(_component_queries(q_ref[...])):
        pieces = []
        for r, (k_ref, v_ref) in enumerate(((k0_ref, v0_ref), (k1_ref, v1_ref), (k2_ref, v2_ref))):
            before_start = jnp.where(qi + (r - (npiece - 1)) < 0, 4 * PREV_CHUNKS, 0)
            kc = col_c + (r - (npiece - 1)) * (tq // CHUNK) + before_start
            visible = (kc <= row_c) & (kc >= row_c - PREV_CHUNKS)
            pieces.append((k_ref[...].astype(BF16), v_ref[...].astype(BF16),
                           bias_ref[head, :, r * tq:(r + 1) * tq], visible))
        outs.append(_joint_softmax_attend(qc, pieces))
    o_ref[...] = _band_finish(*outs)


def _band_bias_tiles(table, rel):
    idx = jnp.clip(rel, -REL_CLIP, REL_CLIP) + REL_CLIP
    b = table[:, idx]
    return b.reshape(table.shape[0] // 2, 2, *b.shape[1:])


def _band_prompt(q, k, v, table, *, batch, seq):
    tq = TQ_BAND
    assert BAND_PAST == 2 * tq and seq % tq == 0
    nq = seq // tq
    hp = D_MODEL // LANES
    rel = jnp.arange(tq)[:, None] - jnp.arange(3 * tq)[None, :] + 2 * tq
    bias = _band_bias_tiles(table, rel)
    qmap = lambda h, b, i: (b * nq + i, h)
    kmap = lambda r: (lambda h, b, i: (b * nq + jnp.maximum(i + r - 2, 0), h))
    return pl.pallas_call(
        _band_prompt_kernel,
        grid=(hp, batch, nq),
        in_specs=[pl.BlockSpec((tq, LANES), qmap)]
                 + [pl.BlockSpec((tq, LANES), kmap(r)) for r in range(3)] * 2
                 + [pl.BlockSpec((None, 2, tq, 3 * tq), lambda h, b, i: (h, 0, 0, 0))],
        out_specs=pl.BlockSpec((tq, LANES), qmap),
        out_shape=jax.ShapeDtypeStruct(q.shape, BF16),
        compiler_params=_cparams(("parallel", "parallel", "parallel")),
        name="band_attn_prompt",
    )(q, k, k, k, v, v, v, bias)


def _band_sample_kernel(q_ref, ck_ref, cv_ref, nk_ref, nv_ref, bc_ref, bn_ref, o_ref):
    outs = []
    for head, qc in enumerate(_component_queries(q_ref[...])):
        pieces = [(ck_ref[...].astype(BF16), cv_ref[...].astype(BF16), bc_ref[head], None),
                  (nk_ref[...].astype(BF16), nv_ref[...].astype(BF16), bn_ref[head], None)]
        outs.append(_joint_softmax_attend(qc, pieces))
    o_ref[...] = _band_finish(*outs)


def _band_sample(q, ck, cv, nk, nv, table, *, batch, t_new, past):
    hp = D_MODEL // LANES
    rel = past + jnp.arange(t_new)[:, None] - jnp.arange(past + t_new)[None, :]
    bias = _band_bias_tiles(table, rel)
    bias_c, bias_n = bias[..., :past], bias[..., past:]
    hb = lambda h, b: (b, h)
    return pl.pallas_call(
        _band_sample_kernel,
        grid=(hp, batch),
        in_specs=[pl.BlockSpec((t_new, LANES), hb),
                  pl.BlockSpec((past, LANES), hb), pl.BlockSpec((past, LANES), hb),
                  pl.BlockSpec((t_new, LANES), hb), pl.BlockSpec((t_new, LANES), hb),
                  pl.BlockSpec((None, 2, t_new, past), lambda h, b: (h, 0, 0, 0)),
                  pl.BlockSpec((None, 2, t_new, t_new), lambda h, b: (h, 0, 0, 0))],
        out_specs=pl.BlockSpec((t_new, LANES), hb),
        out_shape=jax.ShapeDtypeStruct(q.shape, BF16),
        compiler_params=_cparams(("parallel", "parallel")),
        name="band_attn_sample",
    )(q, ck, cv, nk, nv, bias_c, bias_n)


def _outproj_kernel(o_ref, w_ref, x_ref, gate_ref, y_ref):
    y_ref[...] = x_ref[...] + gate_ref[...] * _dot(o_ref[...], w_ref[...])


def _outproj(o, w_bf, x, mod, *, tm):
    t = x.shape[0]
    row = lambda i: (i, 0)
    return pl.pallas_call(
        _outproj_kernel,
        grid=(t // tm,),
        in_specs=[pl.BlockSpec((tm, D_MODEL), row), pl.BlockSpec(w_bf.shape, lambda i: (0, 0)),
                  pl.BlockSpec((tm, D_MODEL), row), mod.spec(2, tm)],
        out_specs=pl.BlockSpec((tm, D_MODEL), row),
        out_shape=jax.ShapeDtypeStruct(x.shape, F32),
        compiler_params=_cparams(("parallel",)),
        name="out_proj",
    )(o, w_bf, x, mod.arr)


def _top_rows(s, k):
    rows = s.shape[0]
    rid = lax.broadcasted_iota(jnp.int32, s.shape, 0).astype(F32)
    vals, ids = [], []
    for _ in range(k):
        m = jnp.max(s, axis=0, keepdims=True)
        i = jnp.min(jnp.where(s == m, rid, float(rows)), axis=0, keepdims=True)
        vals.append(m)
        ids.append(i)
        s = jnp.where(rid == i, -jnp.inf, s)
    return jnp.concatenate(vals, axis=0), jnp.concatenate(ids, axis=0)


def _pick_rows(sel, table):
    out = jnp.zeros(sel.shape, F32)
    for a in range(table.shape[0]):
        out = out + jnp.where(sel == float(a), table[a:a + 1, :], 0.0)
    return out


def _peer_select_kernel(x_ref, sh_ref, sc_ref, ng_ref, wq_ref, keys_ref, hb_ref, a_ref, b_ref, g_ref):
    hb = _modulated(x_ref[...], ng_ref[...], sc_ref[...], sh_ref[...]).astype(BF16)
    hb_ref[...] = hb

    def head(h, carry):
        qh = _dot(hb, wq_ref[h])
        top = []
        for c in range(2):
            qc = qh[:, c * N_KEYS:(c + 1) * N_KEYS].astype(BF16)
            top.append(_top_rows(_dot_nt(keys_ref[2 * h + c], qc), TOPK))
        (s1, i1), (s2, i2) = top
        cand = jnp.concatenate([s1[a:a + 1, :] + s2 for a in range(TOPK)], axis=0)
        top_s, pos = _top_rows(cand, TOPK)
        a_sel = jnp.floor(pos * (1.0 / TOPK))
        b_sel = pos - a_sel * TOPK
        e = jnp.exp(top_s - top_s[0:1, :])
        rows = pl.ds(pl.multiple_of(h * TOPK, TOPK), TOPK)
        a_ref[rows, :] = _pick_rows(a_sel, i1).astype(jnp.int32)
        b_ref[rows, :] = _pick_rows(b_sel, i2).astype(jnp.int32)
        g_ref[rows, :] = e / jnp.sum(e, axis=0, keepdims=True)
        return carry

    lax.fori_loop(0, PEER_HEADS, head, 0)


def _peer_select(x, mod, ng, wq_bf, keys_bf, *, tm):
    t = x.shape[0]
    row = lambda i: (i, 0)
    colblk = lambda i: (0, i)
    npair = PEER_HEADS * TOPK
    return pl.pallas_call(
        _peer_select_kernel,
        grid=(t // tm,),
        in_specs=[pl.BlockSpec((tm, D_MODEL), row), mod.spec(3, tm), mod.spec(4, tm),
                  pl.BlockSpec((1, D_MODEL), lambda i: (0, 0)),
                  pl.BlockSpec(wq_bf.shape, lambda i: (0, 0, 0)),
                  pl.BlockSpec(keys_bf.shape, lambda i: (0, 0, 0))],
        out_specs=[pl.BlockSpec((tm, D_MODEL), row)] + [pl.BlockSpec((npair, tm), colblk)] * 3,
        out_shape=[jax.ShapeDtypeStruct((t, D_MODEL), BF16),
                   jax.ShapeDtypeStruct((npair, t), jnp.int32),
                   jax.ShapeDtypeStruct((npair, t), jnp.int32),
                   jax.ShapeDtypeStruct((npair, t), F32)],
        compiler_params=_cparams(("parallel",)),
        name="peer_select",
    )(x, mod.arr, mod.arr, ng, wq_bf, keys_bf)


def _peer_dense_kernel(hb_ref, a_ref, b_ref, g_ref, u_ref, v_ref, x_ref, gate_ref, y_ref,
                       w3_ref, acc_ref, *, parts):
    part = pl.program_id(1)
    j = pl.program_id(2)
    tm = hb_ref.shape[0]
    rows = N_KEYS // parts

    @pl.when((part == 0) & (j == 0))
    def _():
        acc_ref[...] = jnp.zeros(acc_ref.shape, F32)

    @pl.when(j == 0)
    def _():
        i1_ids = lax.broadcasted_iota(jnp.int32, (rows, N_KEYS), 0) + part * rows
        i2_ids = lax.broadcasted_iota(jnp.int32, (N_KEYS, N_KEYS), 0)

        def token(t, carry):
            arow = a_ref[pl.ds(t, 1), :]
            brow = b_ref[pl.ds(t, 1), :]
            grow = g_ref[pl.ds(t, 1), :]
            oa = jnp.where(arow == i1_ids, 1.0, 0.0).astype(BF16)
            ghi, glo = _split(jnp.where(brow == i2_ids, grow, 0.0))
            w = _dot_nt(oa, ghi) + _dot_nt(oa, glo)
            w3_ref[pl.ds(pl.multiple_of(t * rows, rows), rows), :] = w
            return carry

        lax.fori_loop(0, tm, token, 0)

    w = jnp.concatenate([w3_ref[pl.ds(2 * j, tm, stride=rows), :],
                         w3_ref[pl.ds(2 * j + 1, tm, stride=rows), :]], axis=1)
    hid = _dot_nt(hb_ref[...], u_ref[...])
    act = 0.5 * hid * (1.0 + lax.erf(hid * math.sqrt(0.5)))
    acc_ref[...] += _dot((w * act).astype(BF16), v_ref[...])

    @pl.when((part == parts - 1) & (j == pl.num_programs(2) - 1))
    def _():
        y_ref[...] = x_ref[...] + gate_ref[...] * acc_ref[...]


def _peer_dense(hb, a, b, g, u_bf, v_bf, x, mod, *, tm, parts):
    t = x.shape[0]
    npair = a.shape[1]
    nj = u_bf.shape[0] // E_TILE // parts
    row = lambda i, p, j: (i, 0)
    etile = lambda i, p, j: (p * nj + j, 0)
    rows = N_KEYS // parts
    return pl.pallas_call(
        functools.partial(_peer_dense_kernel, parts=parts),
        grid=(t // tm, parts, nj),
        in_specs=[pl.BlockSpec((tm, D_MODEL), row)] + [pl.BlockSpec((tm, npair), row)] * 3
                 + [pl.BlockSpec((E_TILE, D_MODEL), etile)] * 2
                 + [pl.BlockSpec((tm, D_MODEL), row), mod.spec(5, tm)],
        out_specs=pl.BlockSpec((tm, D_MODEL), row),
        out_shape=jax.ShapeDtypeStruct(x.shape, F32),
        scratch_shapes=[pltpu.VMEM((tm * rows, N_KEYS), F32), pltpu.VMEM((tm, D_MODEL), F32)],
        compiler_params=_cparams(("parallel", "arbitrary", "arbitrary")),
        name="peer_dense",
    )(hb, a, b, g, u_bf, v_bf, x, mod.arr)


def _rope_tables(pos):
    half = DH // 2
    inv = ROPE_THETA ** (-jnp.arange(half, dtype=F32) / half)
    ang = pos.astype(F32)[:, None] * inv[None, :]
    cos, sin = jnp.cos(ang), jnp.sin(ang)
    return jnp.tile(cos, (1, 4)), jnp.concatenate([-sin, sin, -sin, sin], axis=1)


def _pair_tile(g):
    return jnp.tile(g.reshape(1, -1), (1, LANES // g.shape[-1]))


def kernel(x_prompt, x_sample, c_prompt, c_sample, cache_a_k, cache_a_v, cache_b_k, cache_b_v, ada_w, ada_b, norm_g, a_w_in, a_g_q, a_g_k, a_lq1, a_lk1, a_lq2, a_lk2, a_g_sub, a_w_out, b_w_in, b_g_q, b_g_k, b_rel_bias, b_w_out, peer_w_q, peer_sub_keys, peer_u, peer_v):
    batch, seq, _ = x_prompt.shape
    dbatch, t_new, _ = x_sample.shape
    past_a = cache_a_k.shape[2]
    past_b = cache_b_k.shape[2]
    depth = ada_w.shape[0]
    tp, ts = batch * seq, dbatch * t_new
    tm_s = min(TM_PROJ, ts)
    tmd_s = min(TM_DENSE, ts)

    xp = x_prompt.reshape(tp, D_MODEL)
    xs = x_sample.reshape(ts, D_MODEL)
    c_all = jnp.concatenate([c_prompt, c_sample], axis=0)

    cos_p, sin_p = _rope_tables(jnp.arange(seq))
    cos_s, sin_s = _rope_tables(past_a + jnp.arange(t_new))
    cos_s, sin_s = jnp.tile(cos_s, (tm_s // t_new, 1)), jnp.tile(sin_s, (tm_s // t_new, 1))

    outs = {n: [] for n in ("akp", "avp", "aks", "avs", "bkp", "bvp", "bks", "bvs")}
    for i in range(depth):
        j = i // 2
        mod = _ada_mod(c_all, ada_w[i], ada_b[i])
        mod_p = _Mod(mod[:batch].reshape(batch * 6, 1, D_MODEL), False, lambda tm: seq // tm)
        mod_s = _Mod(jnp.repeat(mod[batch:].reshape(dbatch, 6, D_MODEL).transpose(1, 0, 2), t_new, axis=1),
                     True, None)
        ng0, ng1 = norm_g[i, 0].reshape(1, -1), norm_g[i, 1].reshape(1, -1)

        if i % 2 == 0:
            lam_init = 0.8 - 0.6 * math.exp(-0.3 * i)
            w_in = a_w_in[j].astype(BF16)
            gq, gk = _pair_tile(a_g_q[j]), _pair_tile(a_g_k[j])
            lamv = jnp.stack([a_lq1[j], a_lk1[j], a_lq2[j], a_lk2[j]])
            gsub = a_g_sub[j].reshape(1, -1)
            qp, kp, vp = _inproj(xp, mod_p, ng0, w_in, gq, gk, cos_p, sin_p, rope=True, tm=TM_PROJ)
            qs, ks, vs = _inproj(xs, mod_s, ng0, w_in, gq, gk, cos_s, sin_s, rope=True, tm=tm_s)
            op = _flash_diff(qp, kp, vp, lamv, gsub, batch=batch, seq=seq, lam_init=lam_init)
            os_ = _diff_sample(qs, cache_a_k[j].reshape(dbatch * past_a, D_MODEL),
                               cache_a_v[j].reshape(dbatch * past_a, D_MODEL), ks, vs, lamv, gsub,
                               batch=dbatch, t_new=t_new, past=past_a, lam_init=lam_init)
            w_out = a_w_out[j].astype(BF16)
            heads = D_MODEL // LANES
            outs["akp"].append(kp.reshape(batch, seq, heads, 2, DH))
            outs["avp"].append(vp.reshape(batch, seq, heads, 2 * DH))
            outs["aks"].append(ks.reshape(dbatch, t_new, heads, 2, DH))
            outs["avs"].append(vs.reshape(dbatch, t_new, heads, 2 * DH))
        else:
            w_in = b_w_in[j].astype(BF16)
            gq, gk = _pair_tile(b_g_q[j]), _pair_tile(b_g_k[j])
            qp, kp, vp = _inproj(xp, mod_p, ng0, w_in, gq, gk, cos_p, sin_p, rope=False, tm=TM_PROJ)
            qs, ks, vs = _inproj(xs, mod_s, ng0, w_in, gq, gk, cos_s, sin_s, rope=False, tm=tm_s)
            op = _band_prompt(qp, kp, vp, b_rel_bias[j], batch=batch, seq=seq)
            os_ = _band_sample(qs, cache_b_k[j].reshape(dbatch * past_b, D_MODEL),
                               cache_b_v[j].reshape(dbatch * past_b, D_MODEL), ks, vs, b_rel_bias[j],
                               batch=dbatch, t_new=t_new, past=past_b)
            w_out = b_w_out[j].astype(BF16)
            heads = D_MODEL // DH
            keep = min(BAND_PAST, seq)
            k4 = kp.reshape(batch, seq, heads, DH)
            v4 = vp.reshape(batch, seq, heads, DH)
            outs["bkp"].append(k4[:, seq - keep:])
            outs["bvp"].append(v4[:, seq - keep:])
            outs["bks"].append(jnp.concatenate([cache_b_k[j], ks.reshape(dbatch, t_new, heads, DH)], axis=1)[:, t_new:])
            outs["bvs"].append(jnp.concatenate([cache_b_v[j], vs.reshape(dbatch, t_new, heads, DH)], axis=1)[:, t_new:])

        xp = _outproj(op, w_out, xp, mod_p, tm=TM_PROJ)
        xs = _outproj(os_, w_out, xs, mod_s, tm=tm_s)

        wq = peer_w_q[i].astype(BF16).reshape(D_MODEL, PEER_HEADS, 2 * N_KEYS).transpose(1, 0, 2)
        keys = peer_sub_keys[i].astype(BF16).reshape(PEER_HEADS * 2, N_KEYS, -1)
        u_bf, v_bf = peer_u[i].astype(BF16), peer_v[i].astype(BF16)
        new = []
        for x, m, tm, tmd in ((xp, mod_p, TM_PROJ, TM_DENSE), (xs, mod_s, tm_s, tmd_s)):
            hb, a, b, g = _peer_select(x, m, ng1, wq, keys, tm=tm)
            new.append(_peer_dense(hb, a.T, b.T, g.T, u_bf, v_bf, x, m, tm=tmd, parts=DENSE_PARTS))
        xp, xs = new

    st = lambda n: jnp.stack(outs[n], 0)
    return (xp.reshape(x_prompt.shape), xs.reshape(x_sample.shape),
            st("akp"), st("avp"), st("aks"), st("avs"), st("bkp"), st("bvp"), st("bks"), st("bvs"))
```

```python
import functools
import math

import numpy as np
import jax
import jax.numpy as jnp
from jax import lax
from jax.experimental import pallas as pl
from jax.experimental.pallas import tpu as pltpu

F32 = jnp.float32
BF16 = jnp.bfloat16

D_MODEL = 1024
CHUNK = 64
EPS = 1e-6
NEG = -1e30
ROPE_THETA = 10000.0
DH = 64
LANES = 128
PREV_CHUNKS = 8
BAND_PAST = PREV_CHUNKS * CHUNK
REL_CLIP = 128
PEER_HEADS = 8
N_KEYS = 128
TOPK = 16
VMEM_LIMIT = 48 * 1024 * 1024

TM_PROJ = 256
TQ_FLASH = 1024
FLASH_ROWS = 1024
TQ_BAND = 256
TM_DENSE = 512
DENSE_PARTS = 2
E_TILE = 4 * N_KEYS
W3_PAD = 8

_NT = (((1,), (1,)), ((), ()))


def _cparams(sem):
    return pltpu.CompilerParams(dimension_semantics=sem, vmem_limit_bytes=VMEM_LIMIT)


def _dot(a, b):
    return jnp.dot(a, b, preferred_element_type=F32)


def _dot_nt(a, b):
    return lax.dot_general(a, b, _NT, preferred_element_type=F32)


def _split(a):
    hi = a.astype(BF16)
    lo = (a - hi.astype(F32)).astype(BF16)
    return hi, lo


def _ada_kernel(c_ref, w_ref, b_ref, o_ref):
    c = c_ref[...]
    a = c * (1.0 / (1.0 + jnp.exp(-c)))
    ah, al = _split(a)
    wh, wl = _split(w_ref[...])
    o_ref[...] = _dot(ah, wh) + _dot(al, wh) + _dot(ah, wl) + b_ref[...]


def _ada_mod(c_all, w, b):
    n, d = c_all.shape
    nout = w.shape[1]
    tn = 512
    return pl.pallas_call(
        _ada_kernel,
        grid=(nout // tn,),
        in_specs=[pl.BlockSpec((n, d), lambda j: (0, 0)),
                  pl.BlockSpec((d, tn), lambda j: (0, j)),
                  pl.BlockSpec((1, tn), lambda j: (0, j))],
        out_specs=pl.BlockSpec((n, tn), lambda j: (0, j)),
        out_shape=jax.ShapeDtypeStruct((n, nout), F32),
        compiler_params=_cparams(("parallel",)),
        name="ada_mod",
    )(c_all, w, b.reshape(1, nout))


class _Mod:
    def __init__(self, arr, per_row, tiles_per_batch):
        self.arr, self.per_row, self.tpb = arr, per_row, tiles_per_batch

    def spec(self, k, tm):
        if self.per_row:
            return pl.BlockSpec((None, tm, D_MODEL), lambda i, *_: (k, i, 0))
        tpb = self.tpb(tm)
        return pl.BlockSpec((None, 1, D_MODEL), lambda i, *_: ((i // tpb) * 6 + k, 0, 0))


def _modulated(x, ng, scale, shift):
    ms = jnp.mean(x * x, axis=-1, keepdims=True)
    return (x * lax.rsqrt(ms + EPS) * ng) * (1.0 + scale) + shift


def _inproj_kernel(x_ref, sh_ref, sc_ref, ng_ref, w_ref, gq_ref, gk_ref, cos_ref, sin_ref,
                   q_ref, k_ref, v_ref, *, rope, qscale):
    tm = x_ref.shape[0]
    hb = _modulated(x_ref[...], ng_ref[...], sc_ref[...], sh_ref[...]).astype(BF16)
    lane = lax.broadcasted_iota(jnp.int32, (tm, LANES), 1)
    lo = lane < DH
    swap_sel = (lane & (DH // 2)) != 0

    def norm_rope(xb, g):
        x2 = xb * xb
        slo = jnp.sum(jnp.where(lo, x2, 0.0), axis=-1, keepdims=True)
        shi = jnp.sum(jnp.where(lo, 0.0, x2), axis=-1, keepdims=True)
        ms = jnp.where(lo, slo, shi) * (1.0 / DH)
        y = xb * lax.rsqrt(ms + EPS) * g
        if rope:
            sw = jnp.where(swap_sel, pltpu.roll(y, DH // 2, 1), pltpu.roll(y, LANES - DH // 2, 1))
            y = y * cos_ref[...] + sw * sin_ref[...]
        return y

    nblk = w_ref.shape[1] // (2 * LANES)
    for j in range(nblk):
        acc = _dot(hb, w_ref[:, j * 2 * LANES:(j + 1) * 2 * LANES])
        for half in range(2):
            blk = acc[:, half * LANES:(half + 1) * LANES]
            col = j * 2 * LANES + half * LANES
            if col < D_MODEL:
                q_ref[:, col:col + LANES] = (norm_rope(blk, gq_ref[...]) * qscale).astype(BF16)
            elif col < 2 * D_MODEL:
                k_ref[:, col - D_MODEL:col - D_MODEL + LANES] = norm_rope(blk, gk_ref[...])
            else:
                v_ref[:, col - 2 * D_MODEL:col - 2 * D_MODEL + LANES] = blk


def _inproj(x, mod, ng, w_bf, gq, gk, cos, sin, *, rope, tm):
    t = x.shape[0]
    nrep = cos.shape[0] // tm
    row = lambda i: (i, 0)
    full = lambda i: (0, 0)
    tab = lambda i: (i % nrep, 0)
    return pl.pallas_call(
        functools.partial(_inproj_kernel, rope=rope, qscale=DH ** -0.5),
        grid=(t // tm,),
        in_specs=[pl.BlockSpec((tm, D_MODEL), row), mod.spec(0, tm), mod.spec(1, tm),
                  pl.BlockSpec((1, D_MODEL), full),
                  pl.BlockSpec(w_bf.shape, full),
                  pl.BlockSpec((1, LANES), full), pl.BlockSpec((1, LANES), full),
                  pl.BlockSpec((tm, LANES), tab), pl.BlockSpec((tm, LANES), tab)],
        out_specs=[pl.BlockSpec((tm, D_MODEL), row)] * 3,
        out_shape=[jax.ShapeDtypeStruct((t, D_MODEL), BF16),
                   jax.ShapeDtypeStruct((t, D_MODEL), F32),
                   jax.ShapeDtypeStruct((t, D_MODEL), F32)],
        compiler_params=_cparams(("parallel",)),
        name="qkv_proj",
    )(x, mod.arr, mod.arr, ng, w_bf, gq, gk, cos, sin)


def _diff_lambda(lamv_ref, lam_init):
    lv = lamv_ref[...]
    e1 = jnp.exp(jnp.sum(lv[0:1] * lv[1:2], axis=-1, keepdims=True))
    e2 = jnp.exp(jnp.sum(lv[2:3] * lv[3:4], axis=-1, keepdims=True))
    return e1 - e2 + lam_init


def _diff_finish(o0, o1, lamv_ref, gsub_ref, lam_init):
    o = o0 - _diff_lambda(lamv_ref, lam_init) * o1
    ms = jnp.mean(o * o, axis=-1, keepdims=True)
    return ((o * lax.rsqrt(ms + EPS) * gsub_ref[...]) * (1.0 - lam_init)).astype(BF16)


def _component_queries(q):
    lane = lax.broadcasted_iota(jnp.int32, q.shape, 1)
    zero = jnp.zeros_like(q)
    return jnp.where(lane < DH, q, zero), jnp.where(lane < DH, zero, q)


def _flash_diff_kernel(qt_ref, kt_ref, q_ref, k_ref, v_ref, lamv_ref, gsub_ref, o_ref,
                       m_ref, l_ref, a_ref, *, lam_init):
    p = pl.program_id(2)
    qi = qt_ref[p]
    ki = kt_ref[p]
    tq = q_ref.shape[0]

    @pl.when(ki == 0)
    def _():
        m_ref[...] = jnp.full(m_ref.shape, NEG, F32)
        l_ref[...] = jnp.zeros(l_ref.shape, F32)
        a_ref[...] = jnp.zeros(a_ref.shape, F32)

    def update(masked):
        kb = k_ref[...].astype(BF16)
        vb = v_ref[...].astype(BF16)
        qq = jnp.concatenate(_component_queries(q_ref[...]), axis=0)
        for r in range(2 * tq // FLASH_ROWS):
            rows = slice(r * FLASH_ROWS, (r + 1) * FLASH_ROWS)
            q0 = (r * FLASH_ROWS) % tq
            ncol = min(q0 + FLASH_ROWS, tq) if masked else tq
            s = _dot_nt(qq[rows], kb[:ncol])
            if masked:
                row_c = ((lax.broadcasted_iota(jnp.int32, (FLASH_ROWS, ncol), 0) + q0) % tq) // CHUNK
                col_c = lax.broadcasted_iota(jnp.int32, (FLASH_ROWS, ncol), 1) // CHUNK
                s = jnp.where(col_c <= row_c, s, NEG)
            m_prev = m_ref[rows, :]
            m_new = jnp.maximum(m_prev, jnp.max(s, axis=-1, keepdims=True))
            alpha = jnp.exp(m_prev - m_new)
            pm = jnp.exp(s - jnp.tile(m_new, (1, ncol // LANES)))
            l_ref[rows, :] = alpha * l_ref[rows, :] + jnp.sum(pm, axis=-1, keepdims=True)
            a_ref[rows, :] = alpha * a_ref[rows, :] + _dot(pm.astype(BF16), vb[:ncol])
            m_ref[rows, :] = m_new

    @pl.when(ki < qi)
    def _():
        update(False)

    @pl.when(ki == qi)
    def _():
        update(True)
        o = a_ref[...] / l_ref[...]
        o_ref[...] = _diff_finish(o[:tq], o[tq:], lamv_ref, gsub_ref, lam_init)


def _flash_diff(q, k, v, lamv, gsub, *, batch, seq, lam_init):
    tq = min(TQ_FLASH, seq)
    nq = seq // tq
    heads = D_MODEL // LANES
    pairs = [(a, b) for a in range(nq) for b in range(a + 1)]
    qt = jnp.asarray([a for a, _ in pairs], jnp.int32)
    kt = jnp.asarray([b for _, b in pairs], jnp.int32)
    qmap = lambda b, h, p, qt, kt: (b * nq + qt[p], h)
    kmap = lambda b, h, p, qt, kt: (b * nq + kt[p], h)
    return pl.pallas_call(
        functools.partial(_flash_diff_kernel, lam_init=lam_init),
        grid_spec=pltpu.PrefetchScalarGridSpec(
            num_scalar_prefetch=2,
            grid=(batch, heads, len(pairs)),
            in_specs=[pl.BlockSpec((tq, LANES), qmap),
                      pl.BlockSpec((tq, LANES), kmap),
                      pl.BlockSpec((tq, LANES), kmap),
                      pl.BlockSpec(lamv.shape, lambda *_: (0, 0)),
                      pl.BlockSpec((1, LANES), lambda *_: (0, 0))],
            out_specs=pl.BlockSpec((tq, LANES), qmap),
            scratch_shapes=[pltpu.VMEM((2 * tq, LANES), F32)] * 3),
        out_shape=jax.ShapeDtypeStruct(q.shape, BF16),
        compiler_params=_cparams(("parallel", "parallel", "arbitrary")),
        name="flash_diff_attn",
    )(qt, kt, q, k, v, lamv, gsub)


def _joint_softmax_attend(qc, pieces):
    ss = []
    for kb, _, bias, visible in pieces:
        s = _dot_nt(qc, kb)
        if bias is not None:
            s = s + bias
        if visible is not None:
            s = jnp.where(visible, s, NEG)
        ss.append(s)
    m = functools.reduce(jnp.maximum, [jnp.max(s, axis=-1, keepdims=True) for s in ss])
    l = 0.0
    o = 0.0
    for s, (_, vb, _, _) in zip(ss, pieces):
        pm = jnp.exp(s - m)
        l = l + jnp.sum(pm, axis=-1, keepdims=True)
        o = o + _dot(pm.astype(BF16), vb)
    return o / l


def _diff_sample_kernel(q_ref, ck_ref, cv_ref, nk_ref, nv_ref, lamv_ref, gsub_ref, o_ref, *, lam_init):
    pieces = [(ck_ref[...].astype(BF16), cv_ref[...].astype(BF16), None, None),
              (nk_ref[...].astype(BF16), nv_ref[...].astype(BF16), None, None)]
    o0, o1 = [_joint_softmax_attend(qc, pieces) for qc in _component_queries(q_ref[...])]
    o_ref[...] = _diff_finish(o0, o1, lamv_ref, gsub_ref, lam_init)


def _diff_sample(q, ck, cv, nk, nv, lamv, gsub, *, batch, t_new, past, lam_init):
    heads = D_MODEL // LANES
    bh = lambda b, h: (b, h)
    return pl.pallas_call(
        functools.partial(_diff_sample_kernel, lam_init=lam_init),
        grid=(batch, heads),
        in_specs=[pl.BlockSpec((t_new, LANES), bh),
                  pl.BlockSpec((past, LANES), bh), pl.BlockSpec((past, LANES), bh),
                  pl.BlockSpec((t_new, LANES), bh), pl.BlockSpec((t_new, LANES), bh),
                  pl.BlockSpec(lamv.shape, lambda b, h: (0, 0)),
                  pl.BlockSpec((1, LANES), lambda b, h: (0, 0))],
        out_specs=pl.BlockSpec((t_new, LANES), bh),
        out_shape=jax.ShapeDtypeStruct(q.shape, BF16),
        compiler_params=_cparams(("parallel", "parallel")),
        name="diff_attn_sample",
    )(q, ck, cv, nk, nv, lamv, gsub)


def _band_finish(o_even, o_odd):
    lane = lax.broadcasted_iota(jnp.int32, o_even.shape, 1)
    return jnp.where(lane < DH, o_even, o_odd).astype(BF16)


def _band_prompt_kernel(q_ref, k0_ref, k1_ref, k2_ref, v0_ref, v1_ref, v2_ref, bias_ref, o_ref):
    qi = pl.program_id(2)
    tq = q_ref.shape[0]
    row_c = lax.broadcasted_iota(jnp.int32, (tq, tq), 0) // CHUNK
    col_c = lax.broadcasted_iota(jnp.int32, (tq, tq), 1) // CHUNK
    npiece = 3
    outs = []
    for head, qc in enumerate(_component_queries(q_ref[...])):
        pieces = []
        for r, (k_ref, v_ref) in enumerate(((k0_ref, v0_ref), (k1_ref, v1_ref), (k2_ref, v2_ref))):
            before_start = jnp.where(qi + (r - (npiece - 1)) < 0, 4 * PREV_CHUNKS, 0)
            kc = col_c + (r - (npiece - 1)) * (tq // CHUNK) + before_start
            visible = (kc <= row_c) & (kc >= row_c - PREV_CHUNKS)
            pieces.append((k_ref[...].astype(BF16), v_ref[...].astype(BF16),
                           bias_ref[head, :, r * tq:(r + 1) * tq], visible))
        outs.append(_joint_softmax_attend(qc, pieces))
    o_ref[...] = _band_finish(*outs)


def _band_bias_tiles(table, rel):
    idx = jnp.clip(rel, -REL_CLIP, REL_CLIP) + REL_CLIP
    b = table[:, idx]
    return b.reshape(table.shape[0] // 2, 2, *b.shape[1:])


def _band_prompt(q, k, v, table, *, batch, seq):
    tq = TQ_BAND
    assert BAND_PAST == 2 * tq and seq % tq == 0
    nq = seq // tq
    hp = D_MODEL // LANES
    rel = jnp.arange(tq)[:, None] - jnp.arange(3 * tq)[None, :] + 2 * tq
    bias = _band_bias_tiles(table, rel)
    qmap = lambda h, b, i: (b * nq + i, h)
    kmap = lambda r: (lambda h, b, i: (b * nq + jnp.maximum(i + r - 2, 0), h))
    return pl.pallas_call(
        _band_prompt_kernel,
        grid=(hp, batch, nq),
        in_specs=[pl.BlockSpec((tq, LANES), qmap)]
                 + [pl.BlockSpec((tq, LANES), kmap(r)) for r in range(3)] * 2
                 + [pl.BlockSpec((None, 2, tq, 3 * tq), lambda h, b, i: (h, 0, 0, 0))],
        out_specs=pl.BlockSpec((tq, LANES), qmap),
        out_shape=jax.ShapeDtypeStruct(q.shape, BF16),
        compiler_params=_cparams(("parallel", "parallel", "parallel")),
        name="band_attn_prompt",
    )(q, k, k, k, v, v, v, bias)


def _band_sample_kernel(q_ref, ck_ref, cv_ref, nk_ref, nv_ref, bc_ref, bn_ref, o_ref):
    outs = []
    for head, qc in enumerate(_component_queries(q_ref[...])):
        pieces = [(ck_ref[...].astype(BF16), cv_ref[...].astype(BF16), bc_ref[head], None),
                  (nk_ref[...].astype(BF16), nv_ref[...].astype(BF16), bn_ref[head], None)]
        outs.append(_joint_softmax_attend(qc, pieces))
    o_ref[...] = _band_finish(*outs)


def _band_sample(q, ck, cv, nk, nv, table, *, batch, t_new, past):
    hp = D_MODEL // LANES
    rel = past + jnp.arange(t_new)[:, None] - jnp.arange(past + t_new)[None, :]
    bias = _band_bias_tiles(table, rel)
    bias_c, bias_n = bias[..., :past], bias[..., past:]
    hb = lambda h, b: (b, h)
    return pl.pallas_call(
        _band_sample_kernel,
        grid=(hp, batch),
        in_specs=[pl.BlockSpec((t_new, LANES), hb),
                  pl.BlockSpec((past, LANES), hb), pl.BlockSpec((past, LANES), hb),
                  pl.BlockSpec((t_new, LANES), hb), pl.BlockSpec((t_new, LANES), hb),
                  pl.BlockSpec((None, 2, t_new, past), lambda h, b: (h, 0, 0, 0)),
                  pl.BlockSpec((None, 2, t_new, t_new), lambda h, b: (h, 0, 0, 0))],
        out_specs=pl.BlockSpec((t_new, LANES), hb),
        out_shape=jax.ShapeDtypeStruct(q.shape, BF16),
        compiler_params=_cparams(("parallel", "parallel")),
        name="band_attn_sample",
    )(q, ck, cv, nk, nv, bias_c, bias_n)


def _outproj_kernel(o_ref, w_ref, x_ref, gate_ref, y_ref):
    y_ref[...] = x_ref[...] + gate_ref[...] * _dot(o_ref[...], w_ref[...])


def _outproj(o, w_bf, x, mod, *, tm):
    t = x.shape[0]
    row = lambda i: (i, 0)
    return pl.pallas_call(
        _outproj_kernel,
        grid=(t // tm,),
        in_specs=[pl.BlockSpec((tm, D_MODEL), row), pl.BlockSpec(w_bf.shape, lambda i: (0, 0)),
                  pl.BlockSpec((tm, D_MODEL), row), mod.spec(2, tm)],
        out_specs=pl.BlockSpec((tm, D_MODEL), row),
        out_shape=jax.ShapeDtypeStruct(x.shape, F32),
        compiler_params=_cparams(("parallel",)),
        name="out_proj",
    )(o, w_bf, x, mod.arr)


def _top_rows(s, k):
    if s.shape[1] > LANES:
        cols = [_top_rows(s[:, c:c + LANES], k) for c in range(0, s.shape[1], LANES)]
        return tuple(jnp.concatenate(x, axis=1) for x in zip(*cols))
    rows = s.shape[0]
    rid = lax.broadcasted_iota(jnp.int32, s.shape, 0).astype(F32)
    vals, ids = [], []
    for _ in range(k):
        m = jnp.max(s, axis=0, keepdims=True)
        i = jnp.min(jnp.where(s == m, rid, float(rows)), axis=0, keepdims=True)
        vals.append(m)
        ids.append(i)
        s = jnp.where(rid == i, -jnp.inf, s)
    return jnp.concatenate(vals, axis=0), jnp.concatenate(ids, axis=0)


_SUB = 8
_STAIR_PIECES = ([(0, 1, 0, _SUB), (0, 1, _SUB, _SUB), (1, 1, 0, _SUB)]
                 + [(a, 1, 0, TOPK // (a + 1)) for a in range(2, _SUB)] + [(_SUB, _SUB, 0, 1)])


def _stair_candidates(s1, s2):
    sub = lax.broadcasted_iota(jnp.int32, (_SUB, s1.shape[1]), 0)
    pieces = []
    for a0, na, b0, nb in _STAIR_PIECES:
        if na == 1:
            piece = s1[a0:a0 + 1, :] + s2[b0:b0 + _SUB, :]
            if nb < _SUB:
                piece = jnp.where(sub < nb, piece, -jnp.inf)
        else:
            piece = s1[a0:a0 + na, :] + s2[b0:b0 + 1, :]
        pieces.append(piece)
    return jnp.concatenate(pieces, axis=0)


def _stair_ranks(pos):
    a = jnp.zeros(pos.shape, F32)
    b = pos
    for p, (a0, na, b0, nb) in enumerate(_STAIR_PIECES):
        start = float(p * _SUB)
        inside = pos >= start
        if na == 1:
            a = jnp.where(inside, float(a0), a)
            b = jnp.where(inside, pos - start + float(b0), b)
        else:
            a = jnp.where(inside, pos - start + float(a0), a)
            b = jnp.where(inside, float(b0), b)
    return a, b


def _pick_rows(sel, table):
    out = jnp.zeros(sel.shape, F32)
    for a in range(table.shape[0]):
        out = out + jnp.where(sel == float(a), table[a:a + 1, :], 0.0)
    return out


def _peer_select_kernel(x_ref, sh_ref, sc_ref, ng_ref, wq_ref, keys_ref, hb_ref, a_ref, b_ref, g_ref):
    hb = _modulated(x_ref[...], ng_ref[...], sc_ref[...], sh_ref[...]).astype(BF16)
    hb_ref[...] = hb

    def head(h, carry):
        qh = _dot(hb, wq_ref[h])
        top = []
        for c in range(2):
            qc = qh[:, c * N_KEYS:(c + 1) * N_KEYS].astype(BF16)
            top.append(_top_rows(_dot_nt(keys_ref[2 * h + c], qc), TOPK))
        (s1, i1), (s2, i2) = top
        top_s, pos = _top_rows(_stair_candidates(s1, s2), TOPK)
        a_sel, b_sel = _stair_ranks(pos)
        e = jnp.exp(top_s - top_s[0:1, :])
        rows = pl.ds(pl.multiple_of(h * TOPK, TOPK), TOPK)
        a_ref[rows, :] = _pick_rows(a_sel, i1).astype(jnp.int32)
        b_ref[rows, :] = _pick_rows(b_sel, i2).astype(jnp.int32)
        g_ref[rows, :] = e / jnp.sum(e, axis=0, keepdims=True)
        return carry

    lax.fori_loop(0, PEER_HEADS, head, 0)


def _peer_select(x, mod, ng, wq_bf, keys_bf, *, tm):
    t = x.shape[0]
    row = lambda i: (i, 0)
    colblk = lambda i: (0, i)
    npair = PEER_HEADS * TOPK
    return pl.pallas_call(
        _peer_select_kernel,
        grid=(t // tm,),
        in_specs=[pl.BlockSpec((tm, D_MODEL), row), mod.spec(3, tm), mod.spec(4, tm),
                  pl.BlockSpec((1, D_MODEL), lambda i: (0, 0)),
                  pl.BlockSpec(wq_bf.shape, lambda i: (0, 0, 0)),
                  pl.BlockSpec(keys_bf.shape, lambda i: (0, 0, 0))],
        out_specs=[pl.BlockSpec((tm, D_MODEL), row)] + [pl.BlockSpec((npair, tm), colblk)] * 3,
        out_shape=[jax.ShapeDtypeStruct((t, D_MODEL), BF16),
                   jax.ShapeDtypeStruct((npair, t), jnp.int32),
                   jax.ShapeDtypeStruct((npair, t), jnp.int32),
                   jax.ShapeDtypeStruct((npair, t), F32)],
        compiler_params=_cparams(("parallel",)),
        name="peer_select",
    )(x, mod.arr, mod.arr, ng, wq_bf, keys_bf)


def _peer_dense_kernel(hb_ref, a_ref, b_ref, g_ref, u_ref, v_ref, x_ref, gate_ref, y_ref,
                       w3_ref, acc_ref, *, parts):
    part = pl.program_id(1)
    j = pl.program_id(2)
    tm = hb_ref.shape[0]
    rows = N_KEYS // parts

    @pl.when((part == 0) & (j == 0))
    def _():
        acc_ref[...] = jnp.zeros(acc_ref.shape, F32)

    pitch = w3_ref.shape[0] // rows

    @pl.when(j == 0)
    def _():
        i1_ids = lax.broadcasted_iota(jnp.int32, (rows, N_KEYS), 0) + part * rows
        i2_ids = lax.broadcasted_iota(jnp.int32, (N_KEYS, N_KEYS), 0)

        def token(t, carry):
            arow = a_ref[pl.ds(t, 1), :]
            hit = b_ref[pl.ds(t, 1), :] == i2_ids
            grow = g_ref[pl.ds(t, 1), :]
            ghi = grow.astype(BF16).astype(F32)
            oa = jnp.where(arow == i1_ids, 1.0, 0.0).astype(BF16)
            ob = jnp.concatenate([jnp.where(hit, ghi, 0.0).astype(BF16),
                                  jnp.where(hit, grow - ghi, 0.0).astype(BF16)], axis=1)
            w3_ref[pl.ds(t, rows, stride=pitch), :] = _dot_nt(jnp.concatenate([oa, oa], axis=1), ob)
            return carry

        lax.fori_loop(0, tm, token, 0, unroll=8)

    per_step = E_TILE // N_KEYS
    w = jnp.concatenate(
        [w3_ref[pl.ds(pl.multiple_of((per_step * j + q) * pitch, 8), tm), :] for q in range(per_step)], axis=1)
    hid = _dot_nt(hb_ref[...], u_ref[...])
    act = 0.5 * hid * (1.0 + lax.erf(hid * math.sqrt(0.5)))
    acc_ref[...] += _dot((w * act).astype(BF16), v_ref[...])

    @pl.when((part == parts - 1) & (j == pl.num_programs(2) - 1))
    def _():
        y_ref[...] = x_ref[...] + gate_ref[...] * acc_ref[...]


def _peer_dense(hb, a, b, g, u_bf, v_bf, x, mod, *, tm, parts):
    t = x.shape[0]
    npair = a.shape[1]
    nj = u_bf.shape[0] // E_TILE // parts
    row = lambda i, p, j: (i, 0)
    etile = lambda i, p, j: (p * nj + j, 0)
    rows = N_KEYS // parts
    return pl.pallas_call(
        functools.partial(_peer_dense_kernel, parts=parts),
        grid=(t // tm, parts, nj),
        in_specs=[pl.BlockSpec((tm, D_MODEL), row)] + [pl.BlockSpec((tm, npair), row)] * 3
                 + [pl.BlockSpec((E_TILE, D_MODEL), etile)] * 2
                 + [pl.BlockSpec((tm, D_MODEL), row), mod.spec(5, tm)],
        out_specs=pl.BlockSpec((tm, D_MODEL), row),
        out_shape=jax.ShapeDtypeStruct(x.shape, F32),
        scratch_shapes=[pltpu.VMEM((rows * (tm + W3_PAD), N_KEYS), F32), pltpu.VMEM((tm, D_MODEL), F32)],
        compiler_params=_cparams(("parallel", "arbitrary", "arbitrary")),
        name="peer_dense",
    )(hb, a, b, g, u_bf, v_bf, x, mod.arr)


def _rope_tables(pos):
    half = DH // 2
    inv = ROPE_THETA ** (-jnp.arange(half, dtype=F32) / half)
    ang = pos.astype(F32)[:, None] * inv[None, :]
    cos, sin = jnp.cos(ang), jnp.sin(ang)
    return jnp.tile(cos, (1, 4)), jnp.concatenate([-sin, sin, -sin, sin], axis=1)


def _pair_tile(g):
    return jnp.tile(g.reshape(1, -1), (1, LANES // g.shape[-1]))


def kernel(x_prompt, x_sample, c_prompt, c_sample, cache_a_k, cache_a_v, cache_b_k, cache_b_v, ada_w, ada_b, norm_g, a_w_in, a_g_q, a_g_k, a_lq1, a_lk1, a_lq2, a_lk2, a_g_sub, a_w_out, b_w_in, b_g_q, b_g_k, b_rel_bias, b_w_out, peer_w_q, peer_sub_keys, peer_u, peer_v):
    batch, seq, _ = x_prompt.shape
    dbatch, t_new, _ = x_sample.shape
    past_a = cache_a_k.shape[2]
    past_b = cache_b_k.shape[2]
    depth = ada_w.shape[0]
    tp, ts = batch * seq, dbatch * t_new
    tm_s = min(TM_PROJ, ts)
    tmd_s = min(TM_DENSE, ts)

    xp = x_prompt.reshape(tp, D_MODEL)
    xs = x_sample.reshape(ts, D_MODEL)
    c_all = jnp.concatenate([c_prompt, c_sample], axis=0)

    cos_p, sin_p = _rope_tables(jnp.arange(seq))
    cos_s, sin_s = _rope_tables(past_a + jnp.arange(t_new))
    cos_s, sin_s = jnp.tile(cos_s, (tm_s // t_new, 1)), jnp.tile(sin_s, (tm_s // t_new, 1))

    outs = {n: [] for n in ("akp", "avp", "aks", "avs", "bkp", "bvp", "bks", "bvs")}
    for i in range(depth):
        j = i // 2
        mod = _ada_mod(c_all, ada_w[i], ada_b[i])
        mod_p = _Mod(mod[:batch].reshape(batch * 6, 1, D_MODEL), False, lambda tm: seq // tm)
        mod_s = _Mod(jnp.repeat(mod[batch:].reshape(dbatch, 6, D_MODEL).transpose(1, 0, 2), t_new, axis=1),
                     True, None)
        ng0, ng1 = norm_g[i, 0].reshape(1, -1), norm_g[i, 1].reshape(1, -1)

        if i % 2 == 0:
            lam_init = 0.8 - 0.6 * math.exp(-0.3 * i)
            w_in = a_w_in[j].astype(BF16)
            gq, gk = _pair_tile(a_g_q[j]), _pair_tile(a_g_k[j])
            lamv = jnp.stack([a_lq1[j], a_lk1[j], a_lq2[j], a_lk2[j]])
            gsub = a_g_sub[j].reshape(1, -1)
            qp, kp, vp = _inproj(xp, mod_p, ng0, w_in, gq, gk, cos_p, sin_p, rope=True, tm=TM_PROJ)
            qs, ks, vs = _inproj(xs, mod_s, ng0, w_in, gq, gk, cos_s, sin_s, rope=True, tm=tm_s)
            op = _flash_diff(qp, kp, vp, lamv, gsub, batch=batch, seq=seq, lam_init=lam_init)
            os_ = _diff_sample(qs, cache_a_k[j].reshape(dbatch * past_a, D_MODEL),
                               cache_a_v[j].reshape(dbatch * past_a, D_MODEL), ks, vs, lamv, gsub,
                               batch=dbatch, t_new=t_new, past=past_a, lam_init=lam_init)
            w_out = a_w_out[j].astype(BF16)
            heads = D_MODEL // LANES
            outs["akp"].append(kp.reshape(batch, seq, heads, 2, DH))
            outs["avp"].append(vp.reshape(batch, seq, heads, 2 * DH))
            outs["aks"].append(ks.reshape(dbatch, t_new, heads, 2, DH))
            outs["avs"].append(vs.reshape(dbatch, t_new, heads, 2 * DH))
        else:
            w_in = b_w_in[j].astype(BF16)
            gq, gk = _pair_tile(b_g_q[j]), _pair_tile(b_g_k[j])
            qp, kp, vp = _inproj(xp, mod_p, ng0, w_in, gq, gk, cos_p, sin_p, rope=False, tm=TM_PROJ)
            qs, ks, vs = _inproj(xs, mod_s, ng0, w_in, gq, gk, cos_s, sin_s, rope=False, tm=tm_s)
            op = _band_prompt(qp, kp, vp, b_rel_bias[j], batch=batch, seq=seq)
            os_ = _band_sample(qs, cache_b_k[j].reshape(dbatch * past_b, D_MODEL),
                               cache_b_v[j].reshape(dbatch * past_b, D_MODEL), ks, vs, b_rel_bias[j],
                               batch=dbatch, t_new=t_new, past=past_b)
            w_out = b_w_out[j].astype(BF16)
            heads = D_MODEL // DH
            keep = min(BAND_PAST, seq)
            k4 = kp.reshape(batch, seq, heads, DH)
            v4 = vp.reshape(batch, seq, heads, DH)
            outs["bkp"].append(k4[:, seq - keep:])
            outs["bvp"].append(v4[:, seq - keep:])
            outs["bks"].append(jnp.concatenate([cache_b_k[j], ks.reshape(dbatch, t_new, heads, DH)], axis=1)[:, t_new:])
            outs["bvs"].append(jnp.concatenate([cache_b_v[j], vs.reshape(dbatch, t_new, heads, DH)], axis=1)[:, t_new:])

        xp = _outproj(op, w_out, xp, mod_p, tm=TM_PROJ)
        xs = _outproj(os_, w_out, xs, mod_s, tm=tm_s)

        wq = peer_w_q[i].astype(BF16).reshape(D_MODEL, PEER_HEADS, 2 * N_KEYS).transpose(1, 0, 2)
        keys = peer_sub_keys[i].astype(BF16).reshape(PEER_HEADS * 2, N_KEYS, -1)
        u_bf, v_bf = peer_u[i].astype(BF16), peer_v[i].astype(BF16)
        new = []
        for x, m, tm, tmd in ((xp, mod_p, TM_PROJ, TM_DENSE), (xs, mod_s, tm_s, tmd_s)):
            hb, a, b, g = _peer_select(x, m, ng1, wq, keys, tm=tm)
            new.append(_peer_dense(hb, a.T, b.T, g.T, u_bf, v_bf, x, m, tm=tmd, parts=DENSE_PARTS))
        xp, xs = new

    st = lambda n: jnp.stack(outs[n], 0)
    return (xp.reshape(x_prompt.shape), xs.reshape(x_sample.shape),
            st("akp"), st("avp"), st("aks"), st("avs"), st("bkp"), st("bvp"), st("bks"), st("bvs"))
```

```python
import functools
import math

import numpy as np
import jax
import jax.numpy as jnp
from jax import lax
from jax.experimental import pallas as pl
from jax.experimental.pallas import tpu as pltpu

F32 = jnp.float32
BF16 = jnp.bfloat16

D_MODEL = 1024
CHUNK = 64
EPS = 1e-6
NEG = -1e30
ROPE_THETA = 10000.0
DH = 64
LANES = 128
PREV_CHUNKS = 8
BAND_PAST = PREV_CHUNKS * CHUNK
REL_CLIP = 128
PEER_HEADS = 8
N_KEYS = 128
TOPK = 16
VMEM_LIMIT = 48 * 1024 * 1024

TM_PROJ = 256
TQ_FLASH = 1024
FLASH_ROWS = 1024
TQ_BAND = 256
TM_DENSE = 512
E_TILE = 8 * N_KEYS
TOKEN_UNROLL = 32
W3_PAD = 8

_NT = (((1,), (1,)), ((), ()))


def _cparams(sem):
    return pltpu.CompilerParams(dimension_semantics=sem, vmem_limit_bytes=VMEM_LIMIT)


def _dot(a, b):
    return jnp.dot(a, b, preferred_element_type=F32)


def _dot_nt(a, b):
    return lax.dot_general(a, b, _NT, preferred_element_type=F32)


def _split(a):
    hi = a.astype(BF16)
    lo = (a - hi.astype(F32)).astype(BF16)
    return hi, lo


def _ada_kernel(c_ref, w_ref, b_ref, o_ref):
    c = c_ref[...]
    a = c * (1.0 / (1.0 + jnp.exp(-c)))
    ah, al = _split(a)
    wh, wl = _split(w_ref[...])
    o_ref[...] = _dot(ah, wh) + _dot(al, wh) + _dot(ah, wl) + b_ref[...]


def _ada_mod(c_all, w, b):
    n, d = c_all.shape
    nout = w.shape[1]
    tn = 512
    return pl.pallas_call(
        _ada_kernel,
        grid=(nout // tn,),
        in_specs=[pl.BlockSpec((n, d), lambda j: (0, 0)),
                  pl.BlockSpec((d, tn), lambda j: (0, j)),
                  pl.BlockSpec((1, tn), lambda j: (0, j))],
        out_specs=pl.BlockSpec((n, tn), lambda j: (0, j)),
        out_shape=jax.ShapeDtypeStruct((n, nout), F32),
        compiler_params=_cparams(("parallel",)),
        name="ada_mod",
    )(c_all, w, b.reshape(1, nout))


class _Mod:
    def __init__(self, arr, per_row, tiles_per_batch):
        self.arr, self.per_row, self.tpb = arr, per_row, tiles_per_batch

    def spec(self, k, tm):
        if self.per_row:
            return pl.BlockSpec((None, tm, D_MODEL), lambda i, *_: (k, i, 0))
        tpb = self.tpb(tm)
        return pl.BlockSpec((None, 1, D_MODEL), lambda i, *_: ((i // tpb) * 6 + k, 0, 0))


def _modulated(x, ng, scale, shift):
    ms = jnp.mean(x * x, axis=-1, keepdims=True)
    return (x * lax.rsqrt(ms + EPS) * ng) * (1.0 + scale) + shift


def _inproj_kernel(x_ref, sh_ref, sc_ref, ng_ref, w_ref, gq_ref, gk_ref, cos_ref, sin_ref,
                   q_ref, k_ref, v_ref, *, rope, qscale):
    tm = x_ref.shape[0]
    hb = _modulated(x_ref[...], ng_ref[...], sc_ref[...], sh_ref[...]).astype(BF16)
    lane = lax.broadcasted_iota(jnp.int32, (tm, LANES), 1)
    lo = lane < DH
    swap_sel = (lane & (DH // 2)) != 0

    def norm_rope(xb, g):
        x2 = xb * xb
        slo = jnp.sum(jnp.where(lo, x2, 0.0), axis=-1, keepdims=True)
        shi = jnp.sum(jnp.where(lo, 0.0, x2), axis=-1, keepdims=True)
        ms = jnp.where(lo, slo, shi) * (1.0 / DH)
        y = xb * lax.rsqrt(ms + EPS) * g
        if rope:
            sw = jnp.where(swap_sel, pltpu.roll(y, DH // 2, 1), pltpu.roll(y, LANES - DH // 2, 1))
            y = y * cos_ref[...] + sw * sin_ref[...]
        return y

    nblk = w_ref.shape[1] // (2 * LANES)
    for j in range(nblk):
        acc = _dot(hb, w_ref[:, j * 2 * LANES:(j + 1) * 2 * LANES])
        for half in range(2):
            blk = acc[:, half * LANES:(half + 1) * LANES]
            col = j * 2 * LANES + half * LANES
            if col < D_MODEL:
                q_ref[:, col:col + LANES] = (norm_rope(blk, gq_ref[...]) * qscale).astype(BF16)
            elif col < 2 * D_MODEL:
                k_ref[:, col - D_MODEL:col - D_MODEL + LANES] = norm_rope(blk, gk_ref[...])
            else:
                v_ref[:, col - 2 * D_MODEL:col - 2 * D_MODEL + LANES] = blk


def _inproj(x, mod, ng, w_bf, gq, gk, cos, sin, *, rope, tm):
    t = x.shape[0]
    nrep = cos.shape[0] // tm
    row = lambda i: (i, 0)
    full = lambda i: (0, 0)
    tab = lambda i: (i % nrep, 0)
    return pl.pallas_call(
        functools.partial(_inproj_kernel, rope=rope, qscale=DH ** -0.5),
        grid=(t // tm,),
        in_specs=[pl.BlockSpec((tm, D_MODEL), row), mod.spec(0, tm), mod.spec(1, tm),
                  pl.BlockSpec((1, D_MODEL), full),
                  pl.BlockSpec(w_bf.shape, full),
                  pl.BlockSpec((1, LANES), full), pl.BlockSpec((1, LANES), full),
                  pl.BlockSpec((tm, LANES), tab), pl.BlockSpec((tm, LANES), tab)],
        out_specs=[pl.BlockSpec((tm, D_MODEL), row)] * 3,
        out_shape=[jax.ShapeDtypeStruct((t, D_MODEL), BF16),
                   jax.ShapeDtypeStruct((t, D_MODEL), F32),
                   jax.ShapeDtypeStruct((t, D_MODEL), F32)],
        compiler_params=_cparams(("parallel",)),
        name="qkv_proj",
    )(x, mod.arr, mod.arr, ng, w_bf, gq, gk, cos, sin)


def _diff_lambda(lamv_ref, lam_init):
    lv = lamv_ref[...]
    e1 = jnp.exp(jnp.sum(lv[0:1] * lv[1:2], axis=-1, keepdims=True))
    e2 = jnp.exp(jnp.sum(lv[2:3] * lv[3:4], axis=-1, keepdims=True))
    return e1 - e2 + lam_init


def _diff_finish(o0, o1, lamv_ref, gsub_ref, lam_init):
    o = o0 - _diff_lambda(lamv_ref, lam_init) * o1
    ms = jnp.mean(o * o, axis=-1, keepdims=True)
    return ((o * lax.rsqrt(ms + EPS) * gsub_ref[...]) * (1.0 - lam_init)).astype(BF16)


def _component_queries(q):
    lane = lax.broadcasted_iota(jnp.int32, q.shape, 1)
    zero = jnp.zeros_like(q)
    return jnp.where(lane < DH, q, zero), jnp.where(lane < DH, zero, q)


def _flash_diff_kernel(qt_ref, kt_ref, q_ref, k_ref, v_ref, lamv_ref, gsub_ref, o_ref,
                       m_ref, l_ref, a_ref, *, lam_init):
    p = pl.program_id(2)
    qi = qt_ref[p]
    ki = kt_ref[p]
    tq = q_ref.shape[0]

    @pl.when(ki == 0)
    def _():
        m_ref[...] = jnp.full(m_ref.shape, NEG, F32)
        l_ref[...] = jnp.zeros(l_ref.shape, F32)
        a_ref[...] = jnp.zeros(a_ref.shape, F32)

    def update(masked):
        kb = k_ref[...].astype(BF16)
        vb = v_ref[...].astype(BF16)
        qq = jnp.concatenate(_component_queries(q_ref[...]), axis=0)
        for r in range(2 * tq // FLASH_ROWS):
            rows = slice(r * FLASH_ROWS, (r + 1) * FLASH_ROWS)
            q0 = (r * FLASH_ROWS) % tq
            ncol = min(q0 + FLASH_ROWS, tq) if masked else tq
            s = _dot_nt(qq[rows], kb[:ncol])
            if masked:
                row_c = ((lax.broadcasted_iota(jnp.int32, (FLASH_ROWS, ncol), 0) + q0) % tq) // CHUNK
                col_c = lax.broadcasted_iota(jnp.int32, (FLASH_ROWS, ncol), 1) // CHUNK
                s = jnp.where(col_c <= row_c, s, NEG)
            m_prev = m_ref[rows, :]
            m_new = jnp.maximum(m_prev, jnp.max(s, axis=-1, keepdims=True))
            alpha = jnp.exp(m_prev - m_new)
            pm = jnp.exp(s - jnp.tile(m_new, (1, ncol // LANES)))
            l_ref[rows, :] = alpha * l_ref[rows, :] + jnp.sum(pm, axis=-1, keepdims=True)
            a_ref[rows, :] = alpha * a_ref[rows, :] + _dot(pm.astype(BF16), vb[:ncol])
            m_ref[rows, :] = m_new

    @pl.when(ki < qi)
    def _():
        update(False)

    @pl.when(ki == qi)
    def _():
        update(True)
        o = a_ref[...] / l_ref[...]
        o_ref[...] = _diff_finish(o[:tq], o[tq:], lamv_ref, gsub_ref, lam_init)


def _flash_diff(q, k, v, lamv, gsub, *, batch, seq, lam_init):
    tq = min(TQ_FLASH, seq)
    nq = seq // tq
    heads = D_MODEL // LANES
    pairs = [(a, b) for a in range(nq) for b in range(a + 1)]
    qt = jnp.asarray([a for a, _ in pairs], jnp.int32)
    kt = jnp.asarray([b for _, b in pairs], jnp.int32)
    qmap = lambda b, h, p, qt, kt: (b * nq + qt[p], h)
    kmap = lambda b, h, p, qt, kt: (b * nq + kt[p], h)
    return pl.pallas_call(
        functools.partial(_flash_diff_kernel, lam_init=lam_init),
        grid_spec=pltpu.PrefetchScalarGridSpec(
            num_scalar_prefetch=2,
            grid=(batch, heads, len(pairs)),
            in_specs=[pl.BlockSpec((tq, LANES), qmap),
                      pl.BlockSpec((tq, LANES), kmap),
                      pl.BlockSpec((tq, LANES), kmap),
                      pl.BlockSpec(lamv.shape, lambda *_: (0, 0)),
                      pl.BlockSpec((1, LANES), lambda *_: (0, 0))],
            out_specs=pl.BlockSpec((tq, LANES), qmap),
            scratch_shapes=[pltpu.VMEM((2 * tq, LANES), F32)] * 3),
        out_shape=jax.ShapeDtypeStruct(q.shape, BF16),
        compiler_params=_cparams(("parallel", "parallel", "arbitrary")),
        name="flash_diff_attn",
    )(qt, kt, q, k, v, lamv, gsub)


def _joint_softmax_attend(qc, pieces):
    ss = []
    for kb, _, bias, visible in pieces:
        s = _dot_nt(qc, kb)
        if bias is not None:
            s = s + bias
        if visible is not None:
            s = jnp.where(visible, s, NEG)
        ss.append(s)
    m = functools.reduce(jnp.maximum, [jnp.max(s, axis=-1, keepdims=True) for s in ss])
    l = 0.0
    o = 0.0
    for s, (_, vb, _, _) in zip(ss, pieces):
        pm = jnp.exp(s - m)
        l = l + jnp.sum(pm, axis=-1, keepdims=True)
        o = o + _dot(pm.astype(BF16), vb)
    return o / l


def _diff_sample_kernel(q_ref, ck_ref, cv_ref, nk_ref, nv_ref, lamv_ref, gsub_ref, o_ref, *, lam_init):
    t_new = q_ref.shape[0]
    pieces = [(ck_ref[...].astype(BF16), cv_ref[...].astype(BF16), None, None),
              (nk_ref[...].astype(BF16), nv_ref[...].astype(BF16), None, None)]
    o = _joint_softmax_attend(jnp.concatenate(_component_queries(q_ref[...]), axis=0), pieces)
    o_ref[...] = _diff_finish(o[:t_new], o[t_new:], lamv_ref, gsub_ref, lam_init)


def _diff_sample(q, ck, cv, nk, nv, lamv, gsub, *, batch, t_new, past, lam_init):
    heads = D_MODEL // LANES
    bh = lambda b, h: (b, h)
    return pl.pallas_call(
        functools.partial(_diff_sample_kernel, lam_init=lam_init),
        grid=(batch, heads),
        in_specs=[pl.BlockSpec((t_new, LANES), bh),
                  pl.BlockSpec((past, LANES), bh), pl.BlockSpec((past, LANES), bh),
                  pl.BlockSpec((t_new, LANES), bh), pl.BlockSpec((t_new, LANES), bh),
                  pl.BlockSpec(lamv.shape, lambda b, h: (0, 0)),
                  pl.BlockSpec((1, LANES), lambda b, h: (0, 0))],
        out_specs=pl.BlockSpec((t_new, LANES), bh),
        out_shape=jax.ShapeDtypeStruct(q.shape, BF16),
        compiler_params=_cparams(("parallel", "parallel")),
        name="diff_attn_sample",
    )(q, ck, cv, nk, nv, lamv, gsub)


def _band_finish(o):
    tq = o.shape[0] // 2
    lane = lax.broadcasted_iota(jnp.int32, (tq, LANES), 1)
    return jnp.where(lane < DH, o[:tq], o[tq:]).astype(BF16)


def _band_prompt_kernel(q_ref, k0_ref, k1_ref, k2_ref, v0_ref, v1_ref, v2_ref, bias_ref, o_ref):
    qi = pl.program_id(2)
    tq = q_ref.shape[0]
    row_c = (lax.broadcasted_iota(jnp.int32, (2 * tq, tq), 0) % tq) // CHUNK
    col_c = lax.broadcasted_iota(jnp.int32, (2 * tq, tq), 1) // CHUNK
    npiece = 3
    pieces = []
    for r, (k_ref, v_ref) in enumerate(((k0_ref, v0_ref), (k1_ref, v1_ref), (k2_ref, v2_ref))):
        before_start = jnp.where(qi + (r - (npiece - 1)) < 0, 4 * PREV_CHUNKS, 0)
        kc = col_c + (r - (npiece - 1)) * (tq // CHUNK) + before_start
        visible = (kc <= row_c) & (kc >= row_c - PREV_CHUNKS)
        pieces.append((k_ref[...].astype(BF16), v_ref[...].astype(BF16),
                       bias_ref[:, r * tq:(r + 1) * tq], visible))
    qq = jnp.concatenate(_component_queries(q_ref[...]), axis=0)
    o_ref[...] = _band_finish(_joint_softmax_attend(qq, pieces))


def _band_bias_tiles(table, nq, nk, c0):
    n = nq + nk - 1
    diag = jnp.clip(c0 + nq - 1 - np.arange(n), -REL_CLIP, REL_CLIP) + REL_CLIP
    e = table[:, diag]
    a = jnp.tile(e, (1, nq + 1))[:, :nq * (n + 1)].reshape(-1, nq, n + 1)[:, ::-1, :nk]
    return a.reshape(table.shape[0] // 2, 2 * nq, nk)


def _band_prompt(q, k, v, table, *, batch, seq):
    tq = TQ_BAND
    assert BAND_PAST == 2 * tq and seq % tq == 0
    nq = seq // tq
    hp = D_MODEL // LANES
    bias = _band_bias_tiles(table, tq, 3 * tq, 2 * tq)
    qmap = lambda h, b, i: (b * nq + i, h)
    kmap = lambda r: (lambda h, b, i: (b * nq + jnp.maximum(i + r - 2, 0), h))
    return pl.pallas_call(
        _band_prompt_kernel,
        grid=(hp, batch, nq),
        in_specs=[pl.BlockSpec((tq, LANES), qmap)]
                 + [pl.BlockSpec((tq, LANES), kmap(r)) for r in range(3)] * 2
                 + [pl.BlockSpec((None, 2 * tq, 3 * tq), lambda h, b, i: (h, 0, 0))],
        out_specs=pl.BlockSpec((tq, LANES), qmap),
        out_shape=jax.ShapeDtypeStruct(q.shape, BF16),
        compiler_params=_cparams(("parallel", "parallel", "parallel")),
        name="band_attn_prompt",
    )(q, k, k, k, v, v, v, bias)


def _band_sample_kernel(q_ref, ck_ref, cv_ref, nk_ref, nv_ref, bc_ref, bn_ref, o_ref):
    pieces = [(ck_ref[...].astype(BF16), cv_ref[...].astype(BF16), bc_ref[...], None),
              (nk_ref[...].astype(BF16), nv_ref[...].astype(BF16), bn_ref[...], None)]
    qq = jnp.concatenate(_component_queries(q_ref[...]), axis=0)
    o_ref[...] = _band_finish(_joint_softmax_attend(qq, pieces))


def _band_sample(q, ck, cv, nk, nv, table, *, batch, t_new, past):
    hp = D_MODEL // LANES
    bias = _band_bias_tiles(table, t_new, past + t_new, past)
    bias_c, bias_n = bias[..., :past], bias[..., past:]
    hb = lambda h, b: (b, h)
    return pl.pallas_call(
        _band_sample_kernel,
        grid=(hp, batch),
        in_specs=[pl.BlockSpec((t_new, LANES), hb),
                  pl.BlockSpec((past, LANES), hb), pl.BlockSpec((past, LANES), hb),
                  pl.BlockSpec((t_new, LANES), hb), pl.BlockSpec((t_new, LANES), hb),
                  pl.BlockSpec((None, 2 * t_new, past), lambda h, b: (h, 0, 0)),
                  pl.BlockSpec((None, 2 * t_new, t_new), lambda h, b: (h, 0, 0))],
        out_specs=pl.BlockSpec((t_new, LANES), hb),
        out_shape=jax.ShapeDtypeStruct(q.shape, BF16),
        compiler_params=_cparams(("parallel", "parallel")),
        name="band_attn_sample",
    )(q, ck, cv, nk, nv, bias_c, bias_n)


def _outproj_kernel(o_ref, w_ref, x_ref, gate_ref, y_ref):
    y_ref[...] = x_ref[...] + gate_ref[...] * _dot(o_ref[...], w_ref[...])


def _outproj(o, w_bf, x, mod, *, tm):
    t = x.shape[0]
    row = lambda i: (i, 0)
    return pl.pallas_call(
        _outproj_kernel,
        grid=(t // tm,),
        in_specs=[pl.BlockSpec((tm, D_MODEL), row), pl.BlockSpec(w_bf.shape, lambda i: (0, 0)),
                  pl.BlockSpec((tm, D_MODEL), row), mod.spec(2, tm)],
        out_specs=pl.BlockSpec((tm, D_MODEL), row),
        out_shape=jax.ShapeDtypeStruct(x.shape, F32),
        compiler_params=_cparams(("parallel",)),
        name="out_proj",
    )(o, w_bf, x, mod.arr)


def _top_rows(s, k):
    if s.shape[1] > LANES:
        cols = [_top_rows(s[:, c:c + LANES], k) for c in range(0, s.shape[1], LANES)]
        return tuple(jnp.concatenate(x, axis=1) for x in zip(*cols))
    rows = s.shape[0]
    rid = lax.broadcasted_iota(jnp.int32, s.shape, 0).astype(F32)
    vals, ids = [], []
    for _ in range(k):
        m = jnp.max(s, axis=0, keepdims=True)
        i = jnp.min(jnp.where(s == m, rid, float(rows)), axis=0, keepdims=True)
        vals.append(m)
        ids.append(i)
        s = jnp.where(rid == i, -jnp.inf, s)
    return jnp.concatenate(vals, axis=0), jnp.concatenate(ids, axis=0)


_SUB = 8
_STAIR_PIECES = ([(0, 1, 0, _SUB), (0, 1, _SUB, _SUB), (1, 1, 0, _SUB)]
                 + [(a, 1, 0, TOPK // (a + 1)) for a in range(2, _SUB)] + [(_SUB, _SUB, 0, 1)])


def _stair_candidates(s1, s2):
    sub = lax.broadcasted_iota(jnp.int32, (_SUB, s1.shape[1]), 0)
    pieces = []
    for a0, na, b0, nb in _STAIR_PIECES:
        if na == 1:
            piece = s1[a0:a0 + 1, :] + s2[b0:b0 + _SUB, :]
            if nb < _SUB:
                piece = jnp.where(sub < nb, piece, -jnp.inf)
        else:
            piece = s1[a0:a0 + na, :] + s2[b0:b0 + 1, :]
        pieces.append(piece)
    return jnp.concatenate(pieces, axis=0)


def _stair_ranks(pos):
    a = jnp.zeros(pos.shape, F32)
    b = pos
    for p, (a0, na, b0, nb) in enumerate(_STAIR_PIECES):
        start = float(p * _SUB)
        inside = pos >= start
        if na == 1:
            a = jnp.where(inside, float(a0), a)
            b = jnp.where(inside, pos - start + float(b0), b)
        else:
            a = jnp.where(inside, pos - start + float(a0), a)
            b = jnp.where(inside, float(b0), b)
    return a, b


def _pick_rows(sel, table):
    out = jnp.zeros(sel.shape, F32)
    for a in range(table.shape[0]):
        out = out + jnp.where(sel == float(a), table[a:a + 1, :], 0.0)
    return out


def _peer_select_kernel(x_ref, sh_ref, sc_ref, ng_ref, wq_ref, keys_ref, hb_ref, a_ref, b_ref, g_ref):
    hb = _modulated(x_ref[...], ng_ref[...], sc_ref[...], sh_ref[...]).astype(BF16)
    hb_ref[...] = hb

    def head(h, carry):
        qh = _dot(hb, wq_ref[h])
        top = []
        for c in range(2):
            qc = qh[:, c * N_KEYS:(c + 1) * N_KEYS].astype(BF16)
            top.append(_top_rows(_dot_nt(keys_ref[2 * h + c], qc), TOPK))
        (s1, i1), (s2, i2) = top
        top_s, pos = _top_rows(_stair_candidates(s1, s2), TOPK)
        a_sel, b_sel = _stair_ranks(pos)
        e = jnp.exp(top_s - top_s[0:1, :])
        rows = pl.ds(pl.multiple_of(h * TOPK, TOPK), TOPK)
        a_ref[rows, :] = _pick_rows(a_sel, i1).astype(jnp.int32)
        b_ref[rows, :] = _pick_rows(b_sel, i2).astype(jnp.int32)
        g_ref[rows, :] = e / jnp.sum(e, axis=0, keepdims=True)
        return carry

    lax.fori_loop(0, PEER_HEADS, head, 0)


def _peer_select(x, mod, ng, wq_bf, keys_bf, *, tm):
    t = x.shape[0]
    row = lambda i: (i, 0)
    colblk = lambda i: (0, i)
    npair = PEER_HEADS * TOPK
    return pl.pallas_call(
        _peer_select_kernel,
        grid=(t // tm,),
        in_specs=[pl.BlockSpec((tm, D_MODEL), row), mod.spec(3, tm), mod.spec(4, tm),
                  pl.BlockSpec((1, D_MODEL), lambda i: (0, 0)),
                  pl.BlockSpec(wq_bf.shape, lambda i: (0, 0, 0)),
                  pl.BlockSpec(keys_bf.shape, lambda i: (0, 0, 0))],
        out_specs=[pl.BlockSpec((tm, D_MODEL), row)] + [pl.BlockSpec((npair, tm), colblk)] * 3,
        out_shape=[jax.ShapeDtypeStruct((t, D_MODEL), BF16),
                   jax.ShapeDtypeStruct((npair, t), jnp.int32),
                   jax.ShapeDtypeStruct((npair, t), jnp.int32),
                   jax.ShapeDtypeStruct((npair, t), F32)],
        compiler_params=_cparams(("parallel",)),
        name="peer_select",
    )(x, mod.arr, mod.arr, ng, wq_bf, keys_bf)


def _peer_dense_kernel(hb_ref, a_ref, b_ref, g_ref, u_ref, v_ref, x_ref, gate_ref, y_ref, w3_ref, acc_ref):
    j = pl.program_id(1)
    tm = hb_ref.shape[0]
    half = N_KEYS // 2
    pitch = w3_ref.shape[0] // half

    @pl.when(j == 0)
    def _():
        acc_ref[...] = jnp.zeros(acc_ref.shape, F32)
        row = lax.broadcasted_iota(jnp.int32, (N_KEYS, N_KEYS), 0)
        i1_ids = jnp.where(row < half, 2 * row, 2 * (row - half) + 1)

        def token(t, carry):
            arow = a_ref[pl.ds(t, 1), :]
            brow = b_ref[pl.ds(t, 1), :]
            grow = g_ref[pl.ds(t, 1), :]
            oa = jnp.where(arow == i1_ids, 1.0, 0.0).astype(BF16)
            ob = jnp.where(brow == row, grow, 0.0).astype(BF16)
            w = _dot_nt(oa, ob)
            w3_ref[pl.ds(t, half, stride=pitch), :] = pltpu.pack_elementwise([w[:half], w[half:]], packed_dtype=BF16)
            return carry

        lax.fori_loop(0, tm, token, 0, unroll=TOKEN_UNROLL)

    cols = []
    for q in range(E_TILE // (2 * N_KEYS)):
        word = w3_ref[pl.ds(pl.multiple_of((E_TILE // (2 * N_KEYS) * j + q) * pitch, 8), tm), :]
        cols += [pltpu.unpack_elementwise(word, index=i, packed_dtype=BF16, unpacked_dtype=F32) for i in range(2)]
    w = jnp.concatenate(cols, axis=1)
    hid = _dot_nt(hb_ref[...], u_ref[...])
    act = 0.5 * hid * (1.0 + lax.erf(hid * math.sqrt(0.5)))
    acc_ref[...] += _dot((w * act).astype(BF16), v_ref[...])

    @pl.when(j == pl.num_programs(1) - 1)
    def _():
        y_ref[...] = x_ref[...] + gate_ref[...] * acc_ref[...]


def _peer_dense(hb, a, b, g, u_bf, v_bf, x, mod, *, tm):
    t = x.shape[0]
    npair = a.shape[1]
    row = lambda i, j: (i, 0)
    return pl.pallas_call(
        _peer_dense_kernel,
        grid=(t // tm, u_bf.shape[0] // E_TILE),
        in_specs=[pl.BlockSpec((tm, D_MODEL), row)] + [pl.BlockSpec((tm, npair), row)] * 3
                 + [pl.BlockSpec((E_TILE, D_MODEL), lambda i, j: (j, 0))] * 2
                 + [pl.BlockSpec((tm, D_MODEL), row), mod.spec(5, tm)],
        out_specs=pl.BlockSpec((tm, D_MODEL), row),
        out_shape=jax.ShapeDtypeStruct(x.shape, F32),
        scratch_shapes=[pltpu.VMEM((N_KEYS // 2 * (tm + W3_PAD), N_KEYS), jnp.int32),
                        pltpu.VMEM((tm, D_MODEL), F32)],
        compiler_params=_cparams(("parallel", "arbitrary")),
        name="peer_dense",
    )(hb, a, b, g, u_bf, v_bf, x, mod.arr)


def _rope_tables(pos):
    half = DH // 2
    inv = ROPE_THETA ** (-jnp.arange(half, dtype=F32) / half)
    ang = pos.astype(F32)[:, None] * inv[None, :]
    cos, sin = jnp.cos(ang), jnp.sin(ang)
    return jnp.tile(cos, (1, 4)), jnp.concatenate([-sin, sin, -sin, sin], axis=1)


def _pair_tile(g):
    return jnp.tile(g.reshape(1, -1), (1, LANES // g.shape[-1]))


def kernel(x_prompt, x_sample, c_prompt, c_sample, cache_a_k, cache_a_v, cache_b_k, cache_b_v, ada_w, ada_b, norm_g, a_w_in, a_g_q, a_g_k, a_lq1, a_lk1, a_lq2, a_lk2, a_g_sub, a_w_out, b_w_in, b_g_q, b_g_k, b_rel_bias, b_w_out, peer_w_q, peer_sub_keys, peer_u, peer_v):
    batch, seq, _ = x_prompt.shape
    dbatch, t_new, _ = x_sample.shape
    past_a = cache_a_k.shape[2]
    past_b = cache_b_k.shape[2]
    depth = ada_w.shape[0]
    tp, ts = batch * seq, dbatch * t_new
    tm_s = min(TM_PROJ, ts)
    tmd_s = min(TM_DENSE, ts)

    xp = x_prompt.reshape(tp, D_MODEL)
    xs = x_sample.reshape(ts, D_MODEL)
    c_all = jnp.concatenate([c_prompt, c_sample], axis=0)

    cos_p, sin_p = _rope_tables(jnp.arange(seq))
    cos_s, sin_s = _rope_tables(past_a + jnp.arange(t_new))
    cos_s, sin_s = jnp.tile(cos_s, (tm_s // t_new, 1)), jnp.tile(sin_s, (tm_s // t_new, 1))

    outs = {n: [] for n in ("akp", "avp", "aks", "avs", "bkp", "bvp", "bks", "bvs")}
    for i in range(depth):
        j = i // 2
        mod = _ada_mod(c_all, ada_w[i], ada_b[i])
        mod_p = _Mod(mod[:batch].reshape(batch * 6, 1, D_MODEL), False, lambda tm: seq // tm)
        mod_s = _Mod(jnp.repeat(mod[batch:].reshape(dbatch, 6, D_MODEL).transpose(1, 0, 2), t_new, axis=1),
                     True, None)
        ng0, ng1 = norm_g[i, 0].reshape(1, -1), norm_g[i, 1].reshape(1, -1)

        if i % 2 == 0:
            lam_init = 0.8 - 0.6 * math.exp(-0.3 * i)
            w_in = a_w_in[j].astype(BF16)
            gq, gk = _pair_tile(a_g_q[j]), _pair_tile(a_g_k[j])
            lamv = jnp.stack([a_lq1[j], a_lk1[j], a_lq2[j], a_lk2[j]])
            gsub = a_g_sub[j].reshape(1, -1)
            qp, kp, vp = _inproj(xp, mod_p, ng0, w_in, gq, gk, cos_p, sin_p, rope=True, tm=TM_PROJ)
            qs, ks, vs = _inproj(xs, mod_s, ng0, w_in, gq, gk, cos_s, sin_s, rope=True, tm=tm_s)
            op = _flash_diff(qp, kp, vp, lamv, gsub, batch=batch, seq=seq, lam_init=lam_init)
            os_ = _diff_sample(qs, cache_a_k[j].reshape(dbatch * past_a, D_MODEL),
                               cache_a_v[j].reshape(dbatch * past_a, D_MODEL), ks, vs, lamv, gsub,
                               batch=dbatch, t_new=t_new, past=past_a, lam_init=lam_init)
            w_out = a_w_out[j].astype(BF16)
            heads = D_MODEL // LANES
            outs["akp"].append(kp.reshape(batch, seq, heads, 2, DH))
            outs["avp"].append(vp.reshape(batch, seq, heads, 2 * DH))
            outs["aks"].append(ks.reshape(dbatch, t_new, heads, 2, DH))
            outs["avs"].append(vs.reshape(dbatch, t_new, heads, 2 * DH))
        else:
            w_in = b_w_in[j].astype(BF16)
            gq, gk = _pair_tile(b_g_q[j]), _pair_tile(b_g_k[j])
            qp, kp, vp = _inproj(xp, mod_p, ng0, w_in, gq, gk, cos_p, sin_p, rope=False, tm=TM_PROJ)
            qs, ks, vs = _inproj(xs, mod_s, ng0, w_in, gq, gk, cos_s, sin_s, rope=False, tm=tm_s)
            op = _band_prompt(qp, kp, vp, b_rel_bias[j], batch=batch, seq=seq)
            os_ = _band_sample(qs, cache_b_k[j].reshape(dbatch * past_b, D_MODEL),
                               cache_b_v[j].reshape(dbatch * past_b, D_MODEL), ks, vs, b_rel_bias[j],
                               batch=dbatch, t_new=t_new, past=past_b)
            w_out = b_w_out[j].astype(BF16)
            heads = D_MODEL // DH
            keep = min(BAND_PAST, seq)
            k4 = kp.reshape(batch, seq, heads, DH)
            v4 = vp.reshape(batch, seq, heads, DH)
            outs["bkp"].append(k4[:, seq - keep:])
            outs["bvp"].append(v4[:, seq - keep:])
            outs["bks"].append(jnp.concatenate([cache_b_k[j], ks.reshape(dbatch, t_new, heads, DH)], axis=1)[:, t_new:])
            outs["bvs"].append(jnp.concatenate([cache_b_v[j], vs.reshape(dbatch, t_new, heads, DH)], axis=1)[:, t_new:])

        xp = _outproj(op, w_out, xp, mod_p, tm=TM_PROJ)
        xs = _outproj(os_, w_out, xs, mod_s, tm=tm_s)

        wq = peer_w_q[i].astype(BF16).reshape(D_MODEL, PEER_HEADS, 2 * N_KEYS).transpose(1, 0, 2)
        keys = peer_sub_keys[i].astype(BF16).reshape(PEER_HEADS * 2, N_KEYS, -1)
        u_bf, v_bf = peer_u[i].astype(BF16), peer_v[i].astype(BF16)
        new = []
        for x, m, tm, tmd in ((xp, mod_p, TM_PROJ, TM_DENSE), (xs, mod_s, tm_s, tmd_s)):
            hb, a, b, g = _peer_select(x, m, ng1, wq, keys, tm=tm)
            new.append(_peer_dense(hb, a.T, b.T, g.T, u_bf, v_bf, x, m, tm=tmd))
        xp, xs = new

    st = lambda n: jnp.stack(outs[n], 0)
    return (xp.reshape(x_prompt.shape), xs.reshape(x_sample.shape),
            st("akp"), st("avp"), st("aks"), st("avs"), st("bkp"), st("bvp"), st("bks"), st("bvs"))
```

```python
import functools
import math

import numpy as np
import jax
import jax.numpy as jnp
from jax import lax
from jax.experimental import pallas as pl
from jax.experimental.pallas import tpu as pltpu

F32 = jnp.float32
BF16 = jnp.bfloat16

D_MODEL = 1024
CHUNK = 64
EPS = 1e-6
NEG = -1e30
ROPE_THETA = 10000.0
DH = 64
LANES = 128
PREV_CHUNKS = 8
BAND_PAST = PREV_CHUNKS * CHUNK
REL_CLIP = 128
PEER_HEADS = 8
N_KEYS = 128
TOPK = 16
VMEM_LIMIT = 48 * 1024 * 1024

TM_PROJ = 256
TQ_FLASH = 1024
FLASH_ROWS = 1024
TQ_BAND = 256
TM_DENSE = 512
TOKEN_UNROLL = 32
W3_PAD = 8

_NT = (((1,), (1,)), ((), ()))


def _cparams(sem):
    return pltpu.CompilerParams(dimension_semantics=sem, vmem_limit_bytes=VMEM_LIMIT)


def _dot(a, b):
    return jnp.dot(a, b, preferred_element_type=F32)


def _dot_nt(a, b):
    return lax.dot_general(a, b, _NT, preferred_element_type=F32)


def _split(a):
    hi = a.astype(BF16)
    lo = (a - hi.astype(F32)).astype(BF16)
    return hi, lo


def _ada_kernel(c_ref, w_ref, b_ref, o_ref):
    c = c_ref[...]
    a = c * (1.0 / (1.0 + jnp.exp(-c)))
    ah, al = _split(a)
    wh, wl = _split(w_ref[...])
    o_ref[...] = _dot(ah, wh) + _dot(al, wh) + _dot(ah, wl) + b_ref[...]


def _ada_mod(c_all, w, b):
    n, d = c_all.shape
    nout = w.shape[1]
    tn = 512
    return pl.pallas_call(
        _ada_kernel,
        grid=(nout // tn,),
        in_specs=[pl.BlockSpec((n, d), lambda j: (0, 0)),
                  pl.BlockSpec((d, tn), lambda j: (0, j)),
                  pl.BlockSpec((1, tn), lambda j: (0, j))],
        out_specs=pl.BlockSpec((n, tn), lambda j: (0, j)),
        out_shape=jax.ShapeDtypeStruct((n, nout), F32),
        compiler_params=_cparams(("parallel",)),
        name="ada_mod",
    )(c_all, w, b.reshape(1, nout))


class _Mod:
    def __init__(self, arr, per_row, tiles_per_batch):
        self.arr, self.per_row, self.tpb = arr, per_row, tiles_per_batch

    def spec(self, k, tm, tile=lambda i: i):
        if self.per_row:
            return pl.BlockSpec((None, tm, D_MODEL), lambda i, *_: (k, tile(i), 0))
        tpb = self.tpb(tm)
        return pl.BlockSpec((None, 1, D_MODEL), lambda i, *_: ((tile(i) // tpb) * 6 + k, 0, 0))


def _modulated(x, ng, scale, shift):
    ms = jnp.mean(x * x, axis=-1, keepdims=True)
    return (x * lax.rsqrt(ms + EPS) * ng) * (1.0 + scale) + shift


def _inproj_kernel(x_ref, sh_ref, sc_ref, ng_ref, w_ref, gq_ref, gk_ref, cos_ref, sin_ref,
                   q_ref, k_ref, v_ref, *, rope, qscale):
    tm = x_ref.shape[0]
    hb = _modulated(x_ref[...], ng_ref[...], sc_ref[...], sh_ref[...]).astype(BF16)
    lane = lax.broadcasted_iota(jnp.int32, (tm, LANES), 1)
    lo = lane < DH
    swap_sel = (lane & (DH // 2)) != 0

    def norm_rope(xb, g):
        x2 = xb * xb
        slo = jnp.sum(jnp.where(lo, x2, 0.0), axis=-1, keepdims=True)
        shi = jnp.sum(jnp.where(lo, 0.0, x2), axis=-1, keepdims=True)
        ms = jnp.where(lo, slo, shi) * (1.0 / DH)
        y = xb * lax.rsqrt(ms + EPS) * g
        if rope:
            sw = jnp.where(swap_sel, pltpu.roll(y, DH // 2, 1), pltpu.roll(y, LANES - DH // 2, 1))
            y = y * cos_ref[...] + sw * sin_ref[...]
        return y

    nblk = w_ref.shape[1] // (2 * LANES)
    for j in range(nblk):
        acc = _dot(hb, w_ref[:, j * 2 * LANES:(j + 1) * 2 * LANES])
        for half in range(2):
            blk = acc[:, half * LANES:(half + 1) * LANES]
            col = j * 2 * LANES + half * LANES
            if col < D_MODEL:
                q_ref[:, col:col + LANES] = (norm_rope(blk, gq_ref[...]) * qscale).astype(BF16)
            elif col < 2 * D_MODEL:
                k_ref[:, col - D_MODEL:col - D_MODEL + LANES] = norm_rope(blk, gk_ref[...])
            else:
                v_ref[:, col - 2 * D_MODEL:col - 2 * D_MODEL + LANES] = blk


def _inproj(x, mod, ng, w_bf, gq, gk, cos, sin, *, rope, tm):
    t = x.shape[0]
    nrep = cos.shape[0] // tm
    row = lambda i: (i, 0)
    full = lambda i: (0, 0)
    tab = lambda i: (i % nrep, 0)
    return pl.pallas_call(
        functools.partial(_inproj_kernel, rope=rope, qscale=DH ** -0.5),
        grid=(t // tm,),
        in_specs=[pl.BlockSpec((tm, D_MODEL), row), mod.spec(0, tm), mod.spec(1, tm),
                  pl.BlockSpec((1, D_MODEL), full),
                  pl.BlockSpec(w_bf.shape, full),
                  pl.BlockSpec((1, LANES), full), pl.BlockSpec((1, LANES), full),
                  pl.BlockSpec((tm, LANES), tab), pl.BlockSpec((tm, LANES), tab)],
        out_specs=[pl.BlockSpec((tm, D_MODEL), row)] * 3,
        out_shape=[jax.ShapeDtypeStruct((t, D_MODEL), BF16),
                   jax.ShapeDtypeStruct((t, D_MODEL), F32),
                   jax.ShapeDtypeStruct((t, D_MODEL), F32)],
        compiler_params=_cparams(("parallel",)),
        name="qkv_proj",
    )(x, mod.arr, mod.arr, ng, w_bf, gq, gk, cos, sin)


def _diff_lambda(lamv_ref, lam_init):
    lv = lamv_ref[...]
    e1 = jnp.exp(jnp.sum(lv[0:1] * lv[1:2], axis=-1, keepdims=True))
    e2 = jnp.exp(jnp.sum(lv[2:3] * lv[3:4], axis=-1, keepdims=True))
    return e1 - e2 + lam_init


def _diff_finish(o0, o1, lamv_ref, gsub_ref, lam_init):
    o = o0 - _diff_lambda(lamv_ref, lam_init) * o1
    ms = jnp.mean(o * o, axis=-1, keepdims=True)
    return ((o * lax.rsqrt(ms + EPS) * gsub_ref[...]) * (1.0 - lam_init)).astype(BF16)


def _component_queries(q):
    lane = lax.broadcasted_iota(jnp.int32, q.shape, 1)
    zero = jnp.zeros_like(q)
    return jnp.where(lane < DH, q, zero), jnp.where(lane < DH, zero, q)


def _flash_diff_kernel(qt_ref, kt_ref, q_ref, k_ref, v_ref, lamv_ref, gsub_ref, o_ref,
                       m_ref, l_ref, a_ref, *, lam_init):
    p = pl.program_id(2)
    qi = qt_ref[p]
    ki = kt_ref[p]
    tq = q_ref.shape[0]

    @pl.when(ki == 0)
    def _():
        m_ref[...] = jnp.full(m_ref.shape, NEG, F32)
        l_ref[...] = jnp.zeros(l_ref.shape, F32)
        a_ref[...] = jnp.zeros(a_ref.shape, F32)

    def update(masked):
        kb = k_ref[...].astype(BF16)
        vb = v_ref[...].astype(BF16)
        qq = jnp.concatenate(_component_queries(q_ref[...]), axis=0)
        for r in range(2 * tq // FLASH_ROWS):
            rows = slice(r * FLASH_ROWS, (r + 1) * FLASH_ROWS)
            q0 = (r * FLASH_ROWS) % tq
            ncol = min(q0 + FLASH_ROWS, tq) if masked else tq
            s = _dot_nt(qq[rows], kb[:ncol])
            if masked:
                row_c = ((lax.broadcasted_iota(jnp.int32, (FLASH_ROWS, ncol), 0) + q0) % tq) // CHUNK
                col_c = lax.broadcasted_iota(jnp.int32, (FLASH_ROWS, ncol), 1) // CHUNK
                s = jnp.where(col_c <= row_c, s, NEG)
            m_prev = m_ref[rows, :]
            m_new = jnp.maximum(m_prev, jnp.max(s, axis=-1, keepdims=True))
            alpha = jnp.exp(m_prev - m_new)
            pm = jnp.exp(s - jnp.tile(m_new, (1, ncol // LANES)))
            l_ref[rows, :] = alpha * l_ref[rows, :] + jnp.sum(pm, axis=-1, keepdims=True)
            a_ref[rows, :] = alpha * a_ref[rows, :] + _dot(pm.astype(BF16), vb[:ncol])
            m_ref[rows, :] = m_new

    @pl.when(ki < qi)
    def _():
        update(False)

    @pl.when(ki == qi)
    def _():
        update(True)
        o = a_ref[...] / l_ref[...]
        o_ref[...] = _diff_finish(o[:tq], o[tq:], lamv_ref, gsub_ref, lam_init)


def _flash_diff(q, k, v, lamv, gsub, *, batch, seq, lam_init):
    tq = min(TQ_FLASH, seq)
    nq = seq // tq
    heads = D_MODEL // LANES
    pairs = [(a, b) for a in range(nq) for b in range(a + 1)]
    qt = jnp.asarray([a for a, _ in pairs], jnp.int32)
    kt = jnp.asarray([b for _, b in pairs], jnp.int32)
    qmap = lambda b, h, p, qt, kt: (b * nq + qt[p], h)
    kmap = lambda b, h, p, qt, kt: (b * nq + kt[p], h)
    return pl.pallas_call(
        functools.partial(_flash_diff_kernel, lam_init=lam_init),
        grid_spec=pltpu.PrefetchScalarGridSpec(
            num_scalar_prefetch=2,
            grid=(batch, heads, len(pairs)),
            in_specs=[pl.BlockSpec((tq, LANES), qmap),
                      pl.BlockSpec((tq, LANES), kmap),
                      pl.BlockSpec((tq, LANES), kmap),
                      pl.BlockSpec(lamv.shape, lambda *_: (0, 0)),
                      pl.BlockSpec((1, LANES), lambda *_: (0, 0))],
            out_specs=pl.BlockSpec((tq, LANES), qmap),
            scratch_shapes=[pltpu.VMEM((2 * tq, LANES), F32)] * 3),
        out_shape=jax.ShapeDtypeStruct(q.shape, BF16),
        compiler_params=_cparams(("parallel", "parallel", "arbitrary")),
        name="flash_diff_attn",
    )(qt, kt, q, k, v, lamv, gsub)


def _joint_softmax_attend(qc, pieces):
    ss = []
    for kb, _, bias, visible in pieces:
        s = _dot_nt(qc, kb)
        if bias is not None:
            s = s + bias
        if visible is not None:
            s = jnp.where(visible, s, NEG)
        ss.append(s)
    m = functools.reduce(jnp.maximum, [jnp.max(s, axis=-1, keepdims=True) for s in ss])
    l = 0.0
    o = 0.0
    for s, (_, vb, _, _) in zip(ss, pieces):
        pm = jnp.exp(s - m)
        l = l + jnp.sum(pm, axis=-1, keepdims=True)
        o = o + _dot(pm.astype(BF16), vb)
    return o / l


def _diff_sample_kernel(q_ref, ck_ref, cv_ref, nk_ref, nv_ref, lamv_ref, gsub_ref, o_ref, *, lam_init):
    t_new = q_ref.shape[0]
    pieces = [(ck_ref[...].astype(BF16), cv_ref[...].astype(BF16), None, None),
              (nk_ref[...].astype(BF16), nv_ref[...].astype(BF16), None, None)]
    o = _joint_softmax_attend(jnp.concatenate(_component_queries(q_ref[...]), axis=0), pieces)
    o_ref[...] = _diff_finish(o[:t_new], o[t_new:], lamv_ref, gsub_ref, lam_init)


def _diff_sample(q, ck, cv, nk, nv, lamv, gsub, *, batch, t_new, past, lam_init):
    heads = D_MODEL // LANES
    bh = lambda b, h: (b, h)
    return pl.pallas_call(
        functools.partial(_diff_sample_kernel, lam_init=lam_init),
        grid=(batch, heads),
        in_specs=[pl.BlockSpec((t_new, LANES), bh),
                  pl.BlockSpec((past, LANES), bh), pl.BlockSpec((past, LANES), bh),
                  pl.BlockSpec((t_new, LANES), bh), pl.BlockSpec((t_new, LANES), bh),
                  pl.BlockSpec(lamv.shape, lambda b, h: (0, 0)),
                  pl.BlockSpec((1, LANES), lambda b, h: (0, 0))],
        out_specs=pl.BlockSpec((t_new, LANES), bh),
        out_shape=jax.ShapeDtypeStruct(q.shape, BF16),
        compiler_params=_cparams(("parallel", "parallel")),
        name="diff_attn_sample",
    )(q, ck, cv, nk, nv, lamv, gsub)


def _band_finish(o):
    tq = o.shape[0] // 2
    lane = lax.broadcasted_iota(jnp.int32, (tq, LANES), 1)
    return jnp.where(lane < DH, o[:tq], o[tq:]).astype(BF16)


def _band_prompt_kernel(q_ref, k0_ref, k1_ref, k2_ref, v0_ref, v1_ref, v2_ref, bias_ref, o_ref):
    qi = pl.program_id(2)
    tq = q_ref.shape[0]
    row_c = (lax.broadcasted_iota(jnp.int32, (2 * tq, tq), 0) % tq) // CHUNK
    col_c = lax.broadcasted_iota(jnp.int32, (2 * tq, tq), 1) // CHUNK
    npiece = 3
    pieces = []
    for r, (k_ref, v_ref) in enumerate(((k0_ref, v0_ref), (k1_ref, v1_ref), (k2_ref, v2_ref))):
        before_start = jnp.where(qi + (r - (npiece - 1)) < 0, 4 * PREV_CHUNKS, 0)
        kc = col_c + (r - (npiece - 1)) * (tq // CHUNK) + before_start
        visible = (kc <= row_c) & (kc >= row_c - PREV_CHUNKS)
        pieces.append((k_ref[...].astype(BF16), v_ref[...].astype(BF16),
                       bias_ref[:, r * tq:(r + 1) * tq], visible))
    qq = jnp.concatenate(_component_queries(q_ref[...]), axis=0)
    o_ref[...] = _band_finish(_joint_softmax_attend(qq, pieces))


def _band_bias_tiles(table, nq, nk, c0):
    n = nq + nk - 1
    diag = jnp.clip(c0 + nq - 1 - np.arange(n), -REL_CLIP, REL_CLIP) + REL_CLIP
    e = table[:, diag]
    a = jnp.tile(e, (1, nq + 1))[:, :nq * (n + 1)].reshape(-1, nq, n + 1)[:, ::-1, :nk]
    return a.reshape(table.shape[0] // 2, 2 * nq, nk)


def _band_prompt(q, k, v, table, *, batch, seq):
    tq = TQ_BAND
    assert BAND_PAST == 2 * tq and seq % tq == 0
    nq = seq // tq
    hp = D_MODEL // LANES
    bias = _band_bias_tiles(table, tq, 3 * tq, 2 * tq)
    qmap = lambda h, b, i: (b * nq + i, h)
    kmap = lambda r: (lambda h, b, i: (b * nq + jnp.maximum(i + r - 2, 0), h))
    return pl.pallas_call(
        _band_prompt_kernel,
        grid=(hp, batch, nq),
        in_specs=[pl.BlockSpec((tq, LANES), qmap)]
                 + [pl.BlockSpec((tq, LANES), kmap(r)) for r in range(3)] * 2
                 + [pl.BlockSpec((None, 2 * tq, 3 * tq), lambda h, b, i: (h, 0, 0))],
        out_specs=pl.BlockSpec((tq, LANES), qmap),
        out_shape=jax.ShapeDtypeStruct(q.shape, BF16),
        compiler_params=_cparams(("parallel", "parallel", "parallel")),
        name="band_attn_prompt",
    )(q, k, k, k, v, v, v, bias)


def _band_sample_kernel(q_ref, ck_ref, cv_ref, nk_ref, nv_ref, bc_ref, bn_ref, o_ref):
    pieces = [(ck_ref[...].astype(BF16), cv_ref[...].astype(BF16), bc_ref[...], None),
              (nk_ref[...].astype(BF16), nv_ref[...].astype(BF16), bn_ref[...], None)]
    qq = jnp.concatenate(_component_queries(q_ref[...]), axis=0)
    o_ref[...] = _band_finish(_joint_softmax_attend(qq, pieces))


def _band_sample(q, ck, cv, nk, nv, table, *, batch, t_new, past):
    hp = D_MODEL // LANES
    bias = _band_bias_tiles(table, t_new, past + t_new, past)
    bias_c, bias_n = bias[..., :past], bias[..., past:]
    hb = lambda h, b: (b, h)
    return pl.pallas_call(
        _band_sample_kernel,
        grid=(hp, batch),
        in_specs=[pl.BlockSpec((t_new, LANES), hb),
                  pl.BlockSpec((past, LANES), hb), pl.BlockSpec((past, LANES), hb),
                  pl.BlockSpec((t_new, LANES), hb), pl.BlockSpec((t_new, LANES), hb),
                  pl.BlockSpec((None, 2 * t_new, past), lambda h, b: (h, 0, 0)),
                  pl.BlockSpec((None, 2 * t_new, t_new), lambda h, b: (h, 0, 0))],
        out_specs=pl.BlockSpec((t_new, LANES), hb),
        out_shape=jax.ShapeDtypeStruct(q.shape, BF16),
        compiler_params=_cparams(("parallel", "parallel")),
        name="band_attn_sample",
    )(q, ck, cv, nk, nv, bias_c, bias_n)


def _outproj_kernel(o_ref, w_ref, x_ref, gate_ref, y_ref):
    y_ref[...] = x_ref[...] + gate_ref[...] * _dot(o_ref[...], w_ref[...])


def _outproj(o, w_bf, x, mod, *, tm):
    t = x.shape[0]
    row = lambda i: (i, 0)
    return pl.pallas_call(
        _outproj_kernel,
        grid=(t // tm,),
        in_specs=[pl.BlockSpec((tm, D_MODEL), row), pl.BlockSpec(w_bf.shape, lambda i: (0, 0)),
                  pl.BlockSpec((tm, D_MODEL), row), mod.spec(2, tm)],
        out_specs=pl.BlockSpec((tm, D_MODEL), row),
        out_shape=jax.ShapeDtypeStruct(x.shape, F32),
        compiler_params=_cparams(("parallel",)),
        name="out_proj",
    )(o, w_bf, x, mod.arr)


def _top_rows(s, k):
    if s.shape[1] > LANES:
        cols = [_top_rows(s[:, c:c + LANES], k) for c in range(0, s.shape[1], LANES)]
        return tuple(jnp.concatenate(x, axis=1) for x in zip(*cols))
    rows = s.shape[0]
    rid = lax.broadcasted_iota(jnp.int32, s.shape, 0).astype(F32)
    vals, ids = [], []
    for _ in range(k):
        m = jnp.max(s, axis=0, keepdims=True)
        i = jnp.min(jnp.where(s == m, rid, float(rows)), axis=0, keepdims=True)
        vals.append(m)
        ids.append(i)
        s = jnp.where(rid == i, -jnp.inf, s)
    return jnp.concatenate(vals, axis=0), jnp.concatenate(ids, axis=0)


_SUB = 8
_STAIR_PIECES = ([(0, 1, 0, _SUB), (0, 1, _SUB, _SUB), (1, 1, 0, _SUB)]
                 + [(a, 1, 0, TOPK // (a + 1)) for a in range(2, _SUB)] + [(_SUB, _SUB, 0, 1)])


def _stair_candidates(s1, s2):
    sub = lax.broadcasted_iota(jnp.int32, (_SUB, s1.shape[1]), 0)
    pieces = []
    for a0, na, b0, nb in _STAIR_PIECES:
        if na == 1:
            piece = s1[a0:a0 + 1, :] + s2[b0:b0 + _SUB, :]
            if nb < _SUB:
                piece = jnp.where(sub < nb, piece, -jnp.inf)
        else:
            piece = s1[a0:a0 + na, :] + s2[b0:b0 + 1, :]
        pieces.append(piece)
    return jnp.concatenate(pieces, axis=0)


def _stair_ranks(pos):
    a = jnp.zeros(pos.shape, F32)
    b = pos
    for p, (a0, na, b0, nb) in enumerate(_STAIR_PIECES):
        start = float(p * _SUB)
        inside = pos >= start
        if na == 1:
            a = jnp.where(inside, float(a0), a)
            b = jnp.where(inside, pos - start + float(b0), b)
        else:
            a = jnp.where(inside, pos - start + float(a0), a)
            b = jnp.where(inside, float(b0), b)
    return a, b


def _pick_rows(sel, table):
    out = jnp.zeros(sel.shape, F32)
    for a in range(table.shape[0]):
        out = out + jnp.where(sel == float(a), table[a:a + 1, :], 0.0)
    return out


def _select_unit(q_scr, keys_ref, sel_t_scr, h, part):
    tokens = pl.ds(pl.multiple_of(part * LANES, LANES), LANES)
    top = [_top_rows(_dot_nt(keys_ref[2 * h + c], q_scr[2 * h + c, tokens, :]), TOPK) for c in range(2)]
    (s1, i1), (s2, i2) = top
    top_s, pos = _top_rows(_stair_candidates(s1, s2), TOPK)
    a_sel, b_sel = _stair_ranks(pos)
    e = jnp.exp(top_s - top_s[0:1, :])
    rows = pl.ds(pl.multiple_of(h * TOPK, TOPK), TOPK)
    sel_t_scr[0, part, rows, :] = _pick_rows(a_sel, i1)
    sel_t_scr[1, part, rows, :] = _pick_rows(b_sel, i2)
    sel_t_scr[2, part, rows, :] = e / jnp.sum(e, axis=0, keepdims=True)


def _peer_kernel(xs_ref, sh_ref, sc_ref, ng_ref, wq_ref, keys_ref, u_ref, v_ref, xr_ref, gate_ref, y_ref,
                 hb_scr, q_scr, sel_t_scr, sel_scr, w3_ref, acc_ref):
    i = pl.program_id(0)
    j = pl.program_id(1)
    tm = xs_ref.shape[0]
    nparts = tm // LANES
    half = N_KEYS // 2
    pitch = w3_ref.shape[0] // half
    e_tile = u_ref.shape[0]
    slot_new, slot_dense = i % 2, (i + 1) % 2

    @pl.when((i == 0) & (j == 0))
    def _():
        w3_ref[...] = jnp.zeros(w3_ref.shape, w3_ref.dtype)
        hb_scr[1] = jnp.zeros(hb_scr.shape[1:], hb_scr.dtype)

    @pl.when((i > 0) & (j == 0))
    def _():
        for k in range(3):
            for part in range(nparts):
                sel_scr[k, part * LANES:(part + 1) * LANES, :] = sel_t_scr[k, part].T
        row = lax.broadcasted_iota(jnp.int32, (N_KEYS, N_KEYS), 0)
        i1_ids = jnp.where(row < half, 2 * row, 2 * (row - half) + 1).astype(F32)
        i2_ids = row.astype(F32)

        def token(t, carry):
            arow = sel_scr[0, pl.ds(t, 1), :]
            brow = sel_scr[1, pl.ds(t, 1), :]
            grow = sel_scr[2, pl.ds(t, 1), :]
            oa = jnp.where(arow == i1_ids, 1.0, 0.0).astype(BF16)
            ob = jnp.where(brow == i2_ids, grow, 0.0).astype(BF16)
            w = _dot_nt(oa, ob)
            w3_ref[pl.ds(t, half, stride=pitch), :] = pltpu.pack_elementwise([w[:half], w[half:]], packed_dtype=BF16)
            return carry

        lax.fori_loop(0, tm, token, 0, unroll=TOKEN_UNROLL)

    @pl.when(j == 0)
    def _():
        acc_ref[...] = jnp.zeros(acc_ref.shape, F32)
        hb = _modulated(xs_ref[...], ng_ref[...], sc_ref[...], sh_ref[...]).astype(BF16)
        hb_scr[slot_new] = hb
        for h in range(PEER_HEADS):
            qh = _dot(hb, wq_ref[:, h * 2 * N_KEYS:(h + 1) * 2 * N_KEYS])
            for c in range(2):
                q_scr[2 * h + c] = qh[:, c * N_KEYS:(c + 1) * N_KEYS].astype(BF16)

    _select_unit(q_scr, keys_ref, sel_t_scr, j // nparts, j % nparts)

    cols = []
    for q in range(e_tile // (2 * N_KEYS)):
        word = w3_ref[pl.ds(pl.multiple_of((e_tile // (2 * N_KEYS) * j + q) * pitch, 8), tm), :]
        cols += [pltpu.unpack_elementwise(word, index=k, packed_dtype=BF16, unpacked_dtype=F32) for k in range(2)]
    w = jnp.concatenate(cols, axis=1)
    hid = _dot_nt(hb_scr[slot_dense], u_ref[...])
    act = 0.5 * hid * (1.0 + lax.erf(hid * math.sqrt(0.5)))
    acc_ref[...] += _dot((w * act).astype(BF16), v_ref[...])

    @pl.when((i > 0) & (j == pl.num_programs(1) - 1))
    def _():
        y_ref[...] = xr_ref[...] + gate_ref[...] * acc_ref[...]


def _peer(x, mod, ng, wq_bf, keys_bf, u_bf, v_bf, *, tm):
    t = x.shape[0]
    nt = t // tm
    units = PEER_HEADS * (tm // LANES)
    e_tile = u_bf.shape[0] // units
    assert e_tile % (2 * N_KEYS) == 0 and tm % TOKEN_UNROLL == 0
    new_tile = lambda i: jnp.minimum(i, nt - 1)
    dense_tile = lambda i: jnp.maximum(i - 1, 0)
    full = lambda i, j: (0, 0)
    return pl.pallas_call(
        _peer_kernel,
        grid=(nt + 1, units),
        in_specs=[pl.BlockSpec((tm, D_MODEL), lambda i, j: (new_tile(i), 0)),
                  mod.spec(3, tm, new_tile), mod.spec(4, tm, new_tile),
                  pl.BlockSpec((1, D_MODEL), full),
                  pl.BlockSpec(wq_bf.shape, full),
                  pl.BlockSpec(keys_bf.shape, lambda i, j: (0, 0, 0)),
                  pl.BlockSpec((e_tile, D_MODEL), lambda i, j: (j, 0)),
                  pl.BlockSpec((e_tile, D_MODEL), lambda i, j: (j, 0)),
                  pl.BlockSpec((tm, D_MODEL), lambda i, j: (dense_tile(i), 0)),
                  mod.spec(5, tm, dense_tile)],
        out_specs=pl.BlockSpec((tm, D_MODEL), lambda i, j: (dense_tile(i), 0)),
        out_shape=jax.ShapeDtypeStruct(x.shape, F32),
        scratch_shapes=[pltpu.VMEM((2, tm, D_MODEL), BF16),
                        pltpu.VMEM((2 * PEER_HEADS, tm, N_KEYS), BF16),
                        pltpu.VMEM((3, tm // LANES, PEER_HEADS * TOPK, LANES), F32),
                        pltpu.VMEM((3, tm, PEER_HEADS * TOPK), F32),
                        pltpu.VMEM((N_KEYS // 2 * (tm + W3_PAD), N_KEYS), jnp.int32),
                        pltpu.VMEM((tm, D_MODEL), F32)],
        compiler_params=_cparams(("arbitrary", "arbitrary")),
        name="peer",
    )(x, mod.arr, mod.arr, ng, wq_bf, keys_bf, u_bf, v_bf, x, mod.arr)


def _rope_tables(pos):
    half = DH // 2
    inv = ROPE_THETA ** (-jnp.arange(half, dtype=F32) / half)
    ang = pos.astype(F32)[:, None] * inv[None, :]
    cos, sin = jnp.cos(ang), jnp.sin(ang)
    return jnp.tile(cos, (1, 4)), jnp.concatenate([-sin, sin, -sin, sin], axis=1)


def _pair_tile(g):
    return jnp.tile(g.reshape(1, -1), (1, LANES // g.shape[-1]))


def kernel(x_prompt, x_sample, c_prompt, c_sample, cache_a_k, cache_a_v, cache_b_k, cache_b_v, ada_w, ada_b, norm_g, a_w_in, a_g_q, a_g_k, a_lq1, a_lk1, a_lq2, a_lk2, a_g_sub, a_w_out, b_w_in, b_g_q, b_g_k, b_rel_bias, b_w_out, peer_w_q, peer_sub_keys, peer_u, peer_v):
    batch, seq, _ = x_prompt.shape
    dbatch, t_new, _ = x_sample.shape
    past_a = cache_a_k.shape[2]
    past_b = cache_b_k.shape[2]
    depth = ada_w.shape[0]
    tp, ts = batch * seq, dbatch * t_new
    tm_s = min(TM_PROJ, ts)
    tmd_s = min(TM_DENSE // 2, ts)

    xp = x_prompt.reshape(tp, D_MODEL)
    xs = x_sample.reshape(ts, D_MODEL)
    c_all = jnp.concatenate([c_prompt, c_sample], axis=0)

    cos_p, sin_p = _rope_tables(jnp.arange(seq))
    cos_s, sin_s = _rope_tables(past_a + jnp.arange(t_new))
    cos_s, sin_s = jnp.tile(cos_s, (tm_s // t_new, 1)), jnp.tile(sin_s, (tm_s // t_new, 1))

    outs = {n: [] for n in ("akp", "avp", "aks", "avs", "bkp", "bvp", "bks", "bvs")}
    for i in range(depth):
        j = i // 2
        mod = _ada_mod(c_all, ada_w[i], ada_b[i])
        mod_p = _Mod(mod[:batch].reshape(batch * 6, 1, D_MODEL), False, lambda tm: seq // tm)
        mod_s = _Mod(jnp.repeat(mod[batch:].reshape(dbatch, 6, D_MODEL).transpose(1, 0, 2), t_new, axis=1),
                     True, None)
        ng0, ng1 = norm_g[i, 0].reshape(1, -1), norm_g[i, 1].reshape(1, -1)

        if i % 2 == 0:
            lam_init = 0.8 - 0.6 * math.exp(-0.3 * i)
            w_in = a_w_in[j].astype(BF16)
            gq, gk = _pair_tile(a_g_q[j]), _pair_tile(a_g_k[j])
            lamv = jnp.stack([a_lq1[j], a_lk1[j], a_lq2[j], a_lk2[j]])
            gsub = a_g_sub[j].reshape(1, -1)
            qp, kp, vp = _inproj(xp, mod_p, ng0, w_in, gq, gk, cos_p, sin_p, rope=True, tm=TM_PROJ)
            qs, ks, vs = _inproj(xs, mod_s, ng0, w_in, gq, gk, cos_s, sin_s, rope=True, tm=tm_s)
            op = _flash_diff(qp, kp, vp, lamv, gsub, batch=batch, seq=seq, lam_init=lam_init)
            os_ = _diff_sample(qs, cache_a_k[j].reshape(dbatch * past_a, D_MODEL),
                               cache_a_v[j].reshape(dbatch * past_a, D_MODEL), ks, vs, lamv, gsub,
                               batch=dbatch, t_new=t_new, past=past_a, lam_init=lam_init)
            w_out = a_w_out[j].astype(BF16)
            heads = D_MODEL // LANES
            outs["akp"].append(kp.reshape(batch, seq, heads, 2, DH))
            outs["avp"].append(vp.reshape(batch, seq, heads, 2 * DH))
            outs["aks"].append(ks.reshape(dbatch, t_new, heads, 2, DH))
            outs["avs"].append(vs.reshape(dbatch, t_new, heads, 2 * DH))
        else:
            w_in = b_w_in[j].astype(BF16)
            gq, gk = _pair_tile(b_g_q[j]), _pair_tile(b_g_k[j])
            qp, kp, vp = _inproj(xp, mod_p, ng0, w_in, gq, gk, cos_p, sin_p, rope=False, tm=TM_PROJ)
            qs, ks, vs = _inproj(xs, mod_s, ng0, w_in, gq, gk, cos_s, sin_s, rope=False, tm=tm_s)
            op = _band_prompt(qp, kp, vp, b_rel_bias[j], batch=batch, seq=seq)
            os_ = _band_sample(qs, cache_b_k[j].reshape(dbatch * past_b, D_MODEL),
                               cache_b_v[j].reshape(dbatch * past_b, D_MODEL), ks, vs, b_rel_bias[j],
                               batch=dbatch, t_new=t_new, past=past_b)
            w_out = b_w_out[j].astype(BF16)
            heads = D_MODEL // DH
            keep = min(BAND_PAST, seq)
            k4 = kp.reshape(batch, seq, heads, DH)
            v4 = vp.reshape(batch, seq, heads, DH)
            outs["bkp"].append(k4[:, seq - keep:])
            outs["bvp"].append(v4[:, seq - keep:])
            outs["bks"].append(jnp.concatenate([cache_b_k[j], ks.reshape(dbatch, t_new, heads, DH)], axis=1)[:, t_new:])
            outs["bvs"].append(jnp.concatenate([cache_b_v[j], vs.reshape(dbatch, t_new, heads, DH)], axis=1)[:, t_new:])

        xp = _outproj(op, w_out, xp, mod_p, tm=TM_PROJ)
        xs = _outproj(os_, w_out, xs, mod_s, tm=tm_s)

        wq = peer_w_q[i].astype(BF16)
        keys = peer_sub_keys[i].astype(BF16).reshape(PEER_HEADS * 2, N_KEYS, -1)
        u_bf, v_bf = peer_u[i].astype(BF16), peer_v[i].astype(BF16)
        xp = _peer(xp, mod_p, ng1, wq, keys, u_bf, v_bf, tm=TM_DENSE)
        xs = _peer(xs, mod_s, ng1, wq, keys, u_bf, v_bf, tm=tmd_s)

    st = lambda n: jnp.stack(outs[n], 0)
    return (xp.reshape(x_prompt.shape), xs.reshape(x_sample.shape),
            st("akp"), st("avp"), st("aks"), st("avs"), st("bkp"), st("bvp"), st("bks"), st("bvs"))
```

```python
import functools
import math

import numpy as np
import jax
import jax.numpy as jnp
from jax import lax
from jax.experimental import pallas as pl
from jax.experimental.pallas import tpu as pltpu

F32 = jnp.float32
BF16 = jnp.bfloat16

D_MODEL = 1024
CHUNK = 64
EPS = 1e-6
NEG = -1e30
ROPE_THETA = 10000.0
DH = 64
LANES = 128
PREV_CHUNKS = 8
BAND_PAST = PREV_CHUNKS * CHUNK
REL_CLIP = 128
PEER_HEADS = 8
N_KEYS = 128
TOPK = 16
VMEM_LIMIT = 48 * 1024 * 1024

TM_PROJ = 256
TQ_FLASH = 1024
FLASH_ROWS = 1024
TQ_BAND = 256
BAND_QTILES = 4
TM_DENSE = 512
TOKEN_UNROLL = 32
W3_PAD = 8

_NT = (((1,), (1,)), ((), ()))


def _cparams(sem):
    return pltpu.CompilerParams(dimension_semantics=sem, vmem_limit_bytes=VMEM_LIMIT)


def _dot(a, b):
    return jnp.dot(a, b, preferred_element_type=F32)


def _dot_nt(a, b):
    return lax.dot_general(a, b, _NT, preferred_element_type=F32)


def _split(a):
    hi = a.astype(BF16)
    lo = (a - hi.astype(F32)).astype(BF16)
    return hi, lo


def _ada_kernel(c_ref, w_ref, b_ref, o_ref):
    c = c_ref[...]
    a = c * (1.0 / (1.0 + jnp.exp(-c)))
    ah, al = _split(a)
    wh, wl = _split(w_ref[...])
    o_ref[...] = _dot(ah, wh) + _dot(al, wh) + _dot(ah, wl) + b_ref[...]


def _ada_mod(c_all, w, b):
    n, d = c_all.shape
    nout = w.shape[1]
    tn = 512
    return pl.pallas_call(
        _ada_kernel,
        grid=(nout // tn,),
        in_specs=[pl.BlockSpec((n, d), lambda j: (0, 0)),
                  pl.BlockSpec((d, tn), lambda j: (0, j)),
                  pl.BlockSpec((1, tn), lambda j: (0, j))],
        out_specs=pl.BlockSpec((n, tn), lambda j: (0, j)),
        out_shape=jax.ShapeDtypeStruct((n, nout), F32),
        compiler_params=_cparams(("parallel",)),
        name="ada_mod",
    )(c_all, w, b.reshape(1, nout))


class _Mod:
    def __init__(self, arr, per_row, tiles_per_batch):
        self.arr, self.per_row, self.tpb = arr, per_row, tiles_per_batch

    def spec(self, k, tm, tile=lambda i: i):
        if self.per_row:
            return pl.BlockSpec((None, tm, D_MODEL), lambda i, *_: (k, tile(i), 0))
        tpb = self.tpb(tm)
        return pl.BlockSpec((None, 1, D_MODEL), lambda i, *_: ((tile(i) // tpb) * 6 + k, 0, 0))


def _modulated(x, ng, scale, shift):
    ms = jnp.mean(x * x, axis=-1, keepdims=True)
    return (x * lax.rsqrt(ms + EPS) * ng) * (1.0 + scale) + shift


def _inproj_kernel(x_ref, sh_ref, sc_ref, ng_ref, w_ref, gq_ref, gk_ref, cos_ref, sin_ref,
                   q_ref, k_ref, v_ref, *, rope, qscale):
    tm = x_ref.shape[0]
    hb = _modulated(x_ref[...], ng_ref[...], sc_ref[...], sh_ref[...]).astype(BF16)
    lane = lax.broadcasted_iota(jnp.int32, (tm, LANES), 1)
    lo = lane < DH
    swap_sel = (lane & (DH // 2)) != 0

    def norm_rope(xb, g):
        x2 = xb * xb
        slo = jnp.sum(jnp.where(lo, x2, 0.0), axis=-1, keepdims=True)
        shi = jnp.sum(jnp.where(lo, 0.0, x2), axis=-1, keepdims=True)
        ms = jnp.where(lo, slo, shi) * (1.0 / DH)
        y = xb * lax.rsqrt(ms + EPS) * g
        if rope:
            sw = jnp.where(swap_sel, pltpu.roll(y, DH // 2, 1), pltpu.roll(y, LANES - DH // 2, 1))
            y = y * cos_ref[...] + sw * sin_ref[...]
        return y

    nblk = w_ref.shape[1] // (2 * LANES)
    for j in range(nblk):
        acc = _dot(hb, w_ref[:, j * 2 * LANES:(j + 1) * 2 * LANES])
        for half in range(2):
            blk = acc[:, half * LANES:(half + 1) * LANES]
            col = j * 2 * LANES + half * LANES
            if col < D_MODEL:
                q_ref[:, col:col + LANES] = (norm_rope(blk, gq_ref[...]) * qscale).astype(BF16)
            elif col < 2 * D_MODEL:
                k_ref[:, col - D_MODEL:col - D_MODEL + LANES] = norm_rope(blk, gk_ref[...])
            else:
                v_ref[:, col - 2 * D_MODEL:col - 2 * D_MODEL + LANES] = blk


def _inproj(x, mod, ng, w_bf, gq, gk, cos, sin, *, rope, tm):
    t = x.shape[0]
    nrep = cos.shape[0] // tm
    row = lambda i: (i, 0)
    full = lambda i: (0, 0)
    tab = lambda i: (i % nrep, 0)
    return pl.pallas_call(
        functools.partial(_inproj_kernel, rope=rope, qscale=DH ** -0.5),
        grid=(t // tm,),
        in_specs=[pl.BlockSpec((tm, D_MODEL), row), mod.spec(0, tm), mod.spec(1, tm),
                  pl.BlockSpec((1, D_MODEL), full),
                  pl.BlockSpec(w_bf.shape, full),
                  pl.BlockSpec((1, LANES), full), pl.BlockSpec((1, LANES), full),
                  pl.BlockSpec((tm, LANES), tab), pl.BlockSpec((tm, LANES), tab)],
        out_specs=[pl.BlockSpec((tm, D_MODEL), row)] * 3,
        out_shape=[jax.ShapeDtypeStruct((t, D_MODEL), BF16),
                   jax.ShapeDtypeStruct((t, D_MODEL), F32),
                   jax.ShapeDtypeStruct((t, D_MODEL), F32)],
        compiler_params=_cparams(("parallel",)),
        name="qkv_proj",
    )(x, mod.arr, mod.arr, ng, w_bf, gq, gk, cos, sin)


def _diff_lambda(lamv_ref, lam_init):
    lv = lamv_ref[...]
    e1 = jnp.exp(jnp.sum(lv[0:1] * lv[1:2], axis=-1, keepdims=True))
    e2 = jnp.exp(jnp.sum(lv[2:3] * lv[3:4], axis=-1, keepdims=True))
    return e1 - e2 + lam_init


def _diff_finish(o0, o1, lamv_ref, gsub_ref, lam_init):
    o = o0 - _diff_lambda(lamv_ref, lam_init) * o1
    ms = jnp.mean(o * o, axis=-1, keepdims=True)
    return ((o * lax.rsqrt(ms + EPS) * gsub_ref[...]) * (1.0 - lam_init)).astype(BF16)


def _component_queries(q):
    lane = lax.broadcasted_iota(jnp.int32, q.shape, 1)
    zero = jnp.zeros_like(q)
    return jnp.where(lane < DH, q, zero), jnp.where(lane < DH, zero, q)


def _flash_diff_kernel(qt_ref, kt_ref, q_ref, k_ref, v_ref, lamv_ref, gsub_ref, o_ref,
                       m_ref, l_ref, a_ref, *, lam_init):
    p = pl.program_id(2)
    qi = qt_ref[p]
    ki = kt_ref[p]
    tq = q_ref.shape[0]

    @pl.when(ki == 0)
    def _():
        m_ref[...] = jnp.full(m_ref.shape, NEG, F32)
        l_ref[...] = jnp.zeros(l_ref.shape, F32)
        a_ref[...] = jnp.zeros(a_ref.shape, F32)

    def update(masked):
        kb = k_ref[...].astype(BF16)
        vb = v_ref[...].astype(BF16)
        qq = jnp.concatenate(_component_queries(q_ref[...]), axis=0)
        for r in range(2 * tq // FLASH_ROWS):
            rows = slice(r * FLASH_ROWS, (r + 1) * FLASH_ROWS)
            q0 = (r * FLASH_ROWS) % tq
            ncol = min(q0 + FLASH_ROWS, tq) if masked else tq
            s = _dot_nt(qq[rows], kb[:ncol])
            if masked:
                row_c = ((lax.broadcasted_iota(jnp.int32, (FLASH_ROWS, ncol), 0) + q0) % tq) // CHUNK
                col_c = lax.broadcasted_iota(jnp.int32, (FLASH_ROWS, ncol), 1) // CHUNK
                s = jnp.where(col_c <= row_c, s, NEG)
            m_prev = m_ref[rows, :]
            m_new = jnp.maximum(m_prev, jnp.max(s, axis=-1, keepdims=True))
            alpha = jnp.exp(m_prev - m_new)
            pm = jnp.exp(s - jnp.tile(m_new, (1, ncol // LANES)))
            l_ref[rows, :] = alpha * l_ref[rows, :] + jnp.sum(pm, axis=-1, keepdims=True)
            a_ref[rows, :] = alpha * a_ref[rows, :] + _dot(pm.astype(BF16), vb[:ncol])
            m_ref[rows, :] = m_new

    @pl.when(ki < qi)
    def _():
        update(False)

    @pl.when(ki == qi)
    def _():
        update(True)
        o = a_ref[...] / l_ref[...]
        o_ref[...] = _diff_finish(o[:tq], o[tq:], lamv_ref, gsub_ref, lam_init)


def _flash_diff(q, k, v, lamv, gsub, *, batch, seq, lam_init):
    tq = min(TQ_FLASH, seq)
    nq = seq // tq
    heads = D_MODEL // LANES
    pairs = [(a, b) for a in range(nq) for b in range(a + 1)]
    qt = jnp.asarray([a for a, _ in pairs], jnp.int32)
    kt = jnp.asarray([b for _, b in pairs], jnp.int32)
    qmap = lambda b, h, p, qt, kt: (b * nq + qt[p], h)
    kmap = lambda b, h, p, qt, kt: (b * nq + kt[p], h)
    return pl.pallas_call(
        functools.partial(_flash_diff_kernel, lam_init=lam_init),
        grid_spec=pltpu.PrefetchScalarGridSpec(
            num_scalar_prefetch=2,
            grid=(batch, heads, len(pairs)),
            in_specs=[pl.BlockSpec((tq, LANES), qmap),
                      pl.BlockSpec((tq, LANES), kmap),
                      pl.BlockSpec((tq, LANES), kmap),
                      pl.BlockSpec(lamv.shape, lambda *_: (0, 0)),
                      pl.BlockSpec((1, LANES), lambda *_: (0, 0))],
            out_specs=pl.BlockSpec((tq, LANES), qmap),
            scratch_shapes=[pltpu.VMEM((2 * tq, LANES), F32)] * 3),
        out_shape=jax.ShapeDtypeStruct(q.shape, BF16),
        compiler_params=_cparams(("parallel", "parallel", "arbitrary")),
        name="flash_diff_attn",
    )(qt, kt, q, k, v, lamv, gsub)


def _joint_softmax_attend(qc, pieces):
    ss = []
    for kb, _, bias, visible in pieces:
        s = _dot_nt(qc, kb)
        if bias is not None:
            s = s + bias
        if visible is not None:
            s = jnp.where(visible, s, NEG)
        ss.append(s)
    m = functools.reduce(jnp.maximum, [jnp.max(s, axis=-1, keepdims=True) for s in ss])
    l = 0.0
    o = 0.0
    for s, (_, vb, _, _) in zip(ss, pieces):
        pm = jnp.exp(s - m)
        l = l + jnp.sum(pm, axis=-1, keepdims=True)
        o = o + _dot(pm.astype(BF16), vb)
    return o / l


def _diff_sample_kernel(q_ref, ck_ref, cv_ref, nk_ref, nv_ref, lamv_ref, gsub_ref, o_ref, *, lam_init):
    t_new = q_ref.shape[0]
    pieces = [(ck_ref[...].astype(BF16), cv_ref[...].astype(BF16), None, None),
              (nk_ref[...].astype(BF16), nv_ref[...].astype(BF16), None, None)]
    o = _joint_softmax_attend(jnp.concatenate(_component_queries(q_ref[...]), axis=0), pieces)
    o_ref[...] = _diff_finish(o[:t_new], o[t_new:], lamv_ref, gsub_ref, lam_init)


def _diff_sample(q, ck, cv, nk, nv, lamv, gsub, *, batch, t_new, past, lam_init):
    heads = D_MODEL // LANES
    bh = lambda b, h: (b, h)
    return pl.pallas_call(
        functools.partial(_diff_sample_kernel, lam_init=lam_init),
        grid=(batch, heads),
        in_specs=[pl.BlockSpec((t_new, LANES), bh),
                  pl.BlockSpec((past, LANES), bh), pl.BlockSpec((past, LANES), bh),
                  pl.BlockSpec((t_new, LANES), bh), pl.BlockSpec((t_new, LANES), bh),
                  pl.BlockSpec(lamv.shape, lambda b, h: (0, 0)),
                  pl.BlockSpec((1, LANES), lambda b, h: (0, 0))],
        out_specs=pl.BlockSpec((t_new, LANES), bh),
        out_shape=jax.ShapeDtypeStruct(q.shape, BF16),
        compiler_params=_cparams(("parallel", "parallel")),
        name="diff_attn_sample",
    )(q, ck, cv, nk, nv, lamv, gsub)


def _band_finish(o):
    tq = o.shape[0] // 2
    lane = lax.broadcasted_iota(jnp.int32, (tq, LANES), 1)
    return jnp.where(lane < DH, o[:tq], o[tq:]).astype(BF16)


def _band_prompt_kernel(q_ref, *refs):
    nkv = BAND_QTILES + 2
    k_refs, v_refs, bias_ref, o_ref = refs[:nkv], refs[nkv:2 * nkv], refs[2 * nkv], refs[2 * nkv + 1]
    first = pl.program_id(2) * BAND_QTILES
    tq = q_ref.shape[0] // BAND_QTILES
    row_c = (lax.broadcasted_iota(jnp.int32, (2 * tq, tq), 0) % tq) // CHUNK
    col_c = lax.broadcasted_iota(jnp.int32, (2 * tq, tq), 1) // CHUNK
    kvs = [(k_ref[...].astype(BF16), v_ref[...].astype(BF16)) for k_ref, v_ref in zip(k_refs, v_refs)]
    for sub in range(BAND_QTILES):
        pieces = []
        for r in range(3):
            before_start = jnp.where(first + sub + r - 2 < 0, 4 * PREV_CHUNKS, 0)
            kc = col_c + (r - 2) * (tq // CHUNK) + before_start
            visible = (kc <= row_c) & (kc >= row_c - PREV_CHUNKS)
            pieces.append((*kvs[sub + r], bias_ref[:, r * tq:(r + 1) * tq], visible))
        rows = slice(sub * tq, (sub + 1) * tq)
        qq = jnp.concatenate(_component_queries(q_ref[rows, :]), axis=0)
        o_ref[rows, :] = _band_finish(_joint_softmax_attend(qq, pieces))


def _band_bias_tiles(table, nq, nk, c0):
    n = nq + nk - 1
    diag = jnp.clip(c0 + nq - 1 - np.arange(n), -REL_CLIP, REL_CLIP) + REL_CLIP
    e = table[:, diag]
    a = jnp.tile(e, (1, nq + 1))[:, :nq * (n + 1)].reshape(-1, nq, n + 1)[:, ::-1, :nk]
    return a.reshape(table.shape[0] // 2, 2 * nq, nk)


def _band_prompt(q, k, v, table, *, batch, seq):
    tq = TQ_BAND
    nsub = BAND_QTILES
    assert BAND_PAST == 2 * tq and seq % (nsub * tq) == 0
    nq = seq // tq
    hp = D_MODEL // LANES
    bias = _band_bias_tiles(table, tq, 3 * tq, 2 * tq)
    qmap = lambda h, b, i: (b * (nq // nsub) + i, h)
    kmap = lambda r: (lambda h, b, i: (b * nq + jnp.maximum(nsub * i + r - 2, 0), h))
    kv_specs = [pl.BlockSpec((tq, LANES), kmap(r)) for r in range(nsub + 2)]
    return pl.pallas_call(
        _band_prompt_kernel,
        grid=(hp, batch, nq // nsub),
        in_specs=[pl.BlockSpec((nsub * tq, LANES), qmap)] + kv_specs * 2
                 + [pl.BlockSpec((None, 2 * tq, 3 * tq), lambda h, b, i: (h, 0, 0))],
        out_specs=pl.BlockSpec((nsub * tq, LANES), qmap),
        out_shape=jax.ShapeDtypeStruct(q.shape, BF16),
        compiler_params=_cparams(("parallel", "parallel", "parallel")),
        name="band_attn_prompt",
    )(q, *([k] * (nsub + 2)), *([v] * (nsub + 2)), bias)


def _band_sample_kernel(q_ref, ck_ref, cv_ref, nk_ref, nv_ref, bc_ref, bn_ref, o_ref):
    pieces = [(ck_ref[...].astype(BF16), cv_ref[...].astype(BF16), bc_ref[...], None),
              (nk_ref[...].astype(BF16), nv_ref[...].astype(BF16), bn_ref[...], None)]
    qq = jnp.concatenate(_component_queries(q_ref[...]), axis=0)
    o_ref[...] = _band_finish(_joint_softmax_attend(qq, pieces))


def _band_sample(q, ck, cv, nk, nv, table, *, batch, t_new, past):
    hp = D_MODEL // LANES
    bias = _band_bias_tiles(table, t_new, past + t_new, past)
    bias_c, bias_n = bias[..., :past], bias[..., past:]
    hb = lambda h, b: (b, h)
    return pl.pallas_call(
        _band_sample_kernel,
        grid=(hp, batch),
        in_specs=[pl.BlockSpec((t_new, LANES), hb),
                  pl.BlockSpec((past, LANES), hb), pl.BlockSpec((past, LANES), hb),
                  pl.BlockSpec((t_new, LANES), hb), pl.BlockSpec((t_new, LANES), hb),
                  pl.BlockSpec((None, 2 * t_new, past), lambda h, b: (h, 0, 0)),
                  pl.BlockSpec((None, 2 * t_new, t_new), lambda h, b: (h, 0, 0))],
        out_specs=pl.BlockSpec((t_new, LANES), hb),
        out_shape=jax.ShapeDtypeStruct(q.shape, BF16),
        compiler_params=_cparams(("parallel", "parallel")),
        name="band_attn_sample",
    )(q, ck, cv, nk, nv, bias_c, bias_n)


def _outproj_kernel(o_ref, w_ref, x_ref, gate_ref, y_ref):
    y_ref[...] = x_ref[...] + gate_ref[...] * _dot(o_ref[...], w_ref[...])


def _outproj(o, w_bf, x, mod, *, tm):
    t = x.shape[0]
    row = lambda i: (i, 0)
    return pl.pallas_call(
        _outproj_kernel,
        grid=(t // tm,),
        in_specs=[pl.BlockSpec((tm, D_MODEL), row), pl.BlockSpec(w_bf.shape, lambda i: (0, 0)),
                  pl.BlockSpec((tm, D_MODEL), row), mod.spec(2, tm)],
        out_specs=pl.BlockSpec((tm, D_MODEL), row),
        out_shape=jax.ShapeDtypeStruct(x.shape, F32),
        compiler_params=_cparams(("parallel",)),
        name="out_proj",
    )(o, w_bf, x, mod.arr)


def _top_rows(s, k):
    if s.shape[1] > LANES:
        cols = [_top_rows(s[:, c:c + LANES], k) for c in range(0, s.shape[1], LANES)]
        return tuple(jnp.concatenate(x, axis=1) for x in zip(*cols))
    rows = s.shape[0]
    rid = lax.broadcasted_iota(jnp.int32, s.shape, 0).astype(F32)
    vals, ids = [], []
    for _ in range(k):
        m = jnp.max(s, axis=0, keepdims=True)
        i = jnp.min(jnp.where(s == m, rid, float(rows)), axis=0, keepdims=True)
        vals.append(m)
        ids.append(i)
        s = jnp.where(rid == i, -jnp.inf, s)
    return jnp.concatenate(vals, axis=0), jnp.concatenate(ids, axis=0)


_SUB = 8
_STAIR_PIECES = ([(0, 1, 0, _SUB), (0, 1, _SUB, _SUB), (1, 1, 0, _SUB)]
                 + [(a, 1, 0, TOPK // (a + 1)) for a in range(2, _SUB)] + [(_SUB, _SUB, 0, 1)])


def _stair_candidates(s1, s2):
    sub = lax.broadcasted_iota(jnp.int32, (_SUB, s1.shape[1]), 0)
    pieces = []
    for a0, na, b0, nb in _STAIR_PIECES:
        if na == 1:
            piece = s1[a0:a0 + 1, :] + s2[b0:b0 + _SUB, :]
            if nb < _SUB:
                piece = jnp.where(sub < nb, piece, -jnp.inf)
        else:
            piece = s1[a0:a0 + na, :] + s2[b0:b0 + 1, :]
        pieces.append(piece)
    return jnp.concatenate(pieces, axis=0)


def _stair_ranks(pos):
    a = jnp.zeros(pos.shape, F32)
    b = pos
    for p, (a0, na, b0, nb) in enumerate(_STAIR_PIECES):
        start = float(p * _SUB)
        inside = pos >= start
        if na == 1:
            a = jnp.where(inside, float(a0), a)
            b = jnp.where(inside, pos - start + float(b0), b)
        else:
            a = jnp.where(inside, pos - start + float(a0), a)
            b = jnp.where(inside, float(b0), b)
    return a, b


def _pick_rows(sel, table):
    out = jnp.zeros(sel.shape, F32)
    for a in range(table.shape[0]):
        out = out + jnp.where(sel == float(a), table[a:a + 1, :], 0.0)
    return out


def _select_unit(q_scr, keys_ref, sel_t_scr, h, part):
    tokens = pl.ds(pl.multiple_of(part * LANES, LANES), LANES)
    top = [_top_rows(_dot_nt(keys_ref[2 * h + c], q_scr[2 * h + c, tokens, :]), TOPK) for c in range(2)]
    (s1, i1), (s2, i2) = top
    top_s, pos = _top_rows(_stair_candidates(s1, s2), TOPK)
    a_sel, b_sel = _stair_ranks(pos)
    e = jnp.exp(top_s - top_s[0:1, :])
    rows = pl.ds(pl.multiple_of(h * TOPK, TOPK), TOPK)
    sel_t_scr[0, part, rows, :] = _pick_rows(a_sel, i1)
    sel_t_scr[1, part, rows, :] = _pick_rows(b_sel, i2)
    sel_t_scr[2, part, rows, :] = e / jnp.sum(e, axis=0, keepdims=True)


def _peer_kernel(xs_ref, sh_ref, sc_ref, ng_ref, wq_ref, keys_ref, u_ref, v_ref, xr_ref, gate_ref, y_ref,
                 hb_scr, q_scr, sel_t_scr, sel_scr, w3_ref, acc_ref):
    i = pl.program_id(0)
    j = pl.program_id(1)
    tm = xs_ref.shape[0]
    nparts = tm // LANES
    half = N_KEYS // 2
    pitch = w3_ref.shape[0] // half
    e_tile = u_ref.shape[0]
    slot_new, slot_dense = i % 2, (i + 1) % 2

    @pl.when((i == 0) & (j == 0))
    def _():
        w3_ref[...] = jnp.zeros(w3_ref.shape, w3_ref.dtype)
        hb_scr[1] = jnp.zeros(hb_scr.shape[1:], hb_scr.dtype)

    @pl.when((i > 0) & (j == 0))
    def _():
        for k in range(3):
            for part in range(nparts):
                sel_scr[k, part * LANES:(part + 1) * LANES, :] = sel_t_scr[k, part].T
        row = lax.broadcasted_iota(jnp.int32, (N_KEYS, N_KEYS), 0)
        i1_ids = jnp.where(row < half, 2 * row, 2 * (row - half) + 1).astype(F32)
        i2_ids = row.astype(F32)

        def token(t, carry):
            arow = sel_scr[0, pl.ds(t, 1), :]
            brow = sel_scr[1, pl.ds(t, 1), :]
            grow = sel_scr[2, pl.ds(t, 1), :]
            oa = jnp.where(arow == i1_ids, 1.0, 0.0).astype(BF16)
            ob = jnp.where(brow == i2_ids, grow, 0.0).astype(BF16)
            w = _dot_nt(oa, ob)
            w3_ref[pl.ds(t, half, stride=pitch), :] = pltpu.pack_elementwise([w[:half], w[half:]], packed_dtype=BF16)
            return carry

        lax.fori_loop(0, tm, token, 0, unroll=TOKEN_UNROLL)

    @pl.when(j == 0)
    def _():
        acc_ref[...] = jnp.zeros(acc_ref.shape, F32)
        hb = _modulated(xs_ref[...], ng_ref[...], sc_ref[...], sh_ref[...]).astype(BF16)
        hb_scr[slot_new] = hb
        q_all = _dot(hb, wq_ref[...])
        for hc in range(2 * PEER_HEADS):
            q_scr[hc] = q_all[:, hc * N_KEYS:(hc + 1) * N_KEYS].astype(BF16)

    _select_unit(q_scr, keys_ref, sel_t_scr, j // nparts, j % nparts)

    cols = []
    for q in range(e_tile // (2 * N_KEYS)):
        word = w3_ref[pl.ds(pl.multiple_of((e_tile // (2 * N_KEYS) * j + q) * pitch, 8), tm), :]
        cols += [pltpu.unpack_elementwise(word, index=k, packed_dtype=BF16, unpacked_dtype=F32) for k in range(2)]
    w = jnp.concatenate(cols, axis=1)
    hid = _dot_nt(hb_scr[slot_dense], u_ref[...])
    act = 0.5 * hid * (1.0 + lax.erf(hid * math.sqrt(0.5)))
    acc_ref[...] += _dot((w * act).astype(BF16), v_ref[...])

    @pl.when((i > 0) & (j == pl.num_programs(1) - 1))
    def _():
        y_ref[...] = xr_ref[...] + gate_ref[...] * acc_ref[...]


def _peer(x, mod, ng, wq_bf, keys_bf, u_bf, v_bf, *, tm):
    t = x.shape[0]
    nt = t // tm
    units = PEER_HEADS * (tm // LANES)
    e_tile = u_bf.shape[0] // units
    assert e_tile % (2 * N_KEYS) == 0 and tm % TOKEN_UNROLL == 0
    new_tile = lambda i: jnp.minimum(i, nt - 1)
    dense_tile = lambda i: jnp.maximum(i - 1, 0)
    full = lambda i, j: (0, 0)
    return pl.pallas_call(
        _peer_kernel,
        grid=(nt + 1, units),
        in_specs=[pl.BlockSpec((tm, D_MODEL), lambda i, j: (new_tile(i), 0)),
                  mod.spec(3, tm, new_tile), mod.spec(4, tm, new_tile),
                  pl.BlockSpec((1, D_MODEL), full),
                  pl.BlockSpec(wq_bf.shape, full),
                  pl.BlockSpec(keys_bf.shape, lambda i, j: (0, 0, 0)),
                  pl.BlockSpec((e_tile, D_MODEL), lambda i, j: (j, 0)),
                  pl.BlockSpec((e_tile, D_MODEL), lambda i, j: (j, 0)),
                  pl.BlockSpec((tm, D_MODEL), lambda i, j: (dense_tile(i), 0)),
                  mod.spec(5, tm, dense_tile)],
        out_specs=pl.BlockSpec((tm, D_MODEL), lambda i, j: (dense_tile(i), 0)),
        out_shape=jax.ShapeDtypeStruct(x.shape, F32),
        scratch_shapes=[pltpu.VMEM((2, tm, D_MODEL), BF16),
                        pltpu.VMEM((2 * PEER_HEADS, tm, N_KEYS), BF16),
                        pltpu.VMEM((3, tm // LANES, PEER_HEADS * TOPK, LANES), F32),
                        pltpu.VMEM((3, tm, PEER_HEADS * TOPK), F32),
                        pltpu.VMEM((N_KEYS // 2 * (tm + W3_PAD), N_KEYS), jnp.int32),
                        pltpu.VMEM((tm, D_MODEL), F32)],
        compiler_params=_cparams(("arbitrary", "arbitrary")),
        name="peer",
    )(x, mod.arr, mod.arr, ng, wq_bf, keys_bf, u_bf, v_bf, x, mod.arr)


def _rope_tables(pos):
    half = DH // 2
    inv = ROPE_THETA ** (-jnp.arange(half, dtype=F32) / half)
    ang = pos.astype(F32)[:, None] * inv[None, :]
    cos, sin = jnp.cos(ang), jnp.sin(ang)
    return jnp.tile(cos, (1, 4)), jnp.concatenate([-sin, sin, -sin, sin], axis=1)


def _pair_tile(g):
    return jnp.tile(g.reshape(1, -1), (1, LANES // g.shape[-1]))


def kernel(x_prompt, x_sample, c_prompt, c_sample, cache_a_k, cache_a_v, cache_b_k, cache_b_v, ada_w, ada_b, norm_g, a_w_in, a_g_q, a_g_k, a_lq1, a_lk1, a_lq2, a_lk2, a_g_sub, a_w_out, b_w_in, b_g_q, b_g_k, b_rel_bias, b_w_out, peer_w_q, peer_sub_keys, peer_u, peer_v):
    batch, seq, _ = x_prompt.shape
    dbatch, t_new, _ = x_sample.shape
    past_a = cache_a_k.shape[2]
    past_b = cache_b_k.shape[2]
    depth = ada_w.shape[0]
    tp, ts = batch * seq, dbatch * t_new
    tm_s = min(TM_PROJ, ts)
    tmd_s = min(TM_DENSE // 2, ts)

    xp = x_prompt.reshape(tp, D_MODEL)
    xs = x_sample.reshape(ts, D_MODEL)
    c_all = jnp.concatenate([c_prompt, c_sample], axis=0)

    cos_p, sin_p = _rope_tables(jnp.arange(seq))
    cos_s, sin_s = _rope_tables(past_a + jnp.arange(t_new))
    cos_s, sin_s = jnp.tile(cos_s, (tm_s // t_new, 1)), jnp.tile(sin_s, (tm_s // t_new, 1))

    outs = {n: [] for n in ("akp", "avp", "aks", "avs", "bkp", "bvp", "bks", "bvs")}
    for i in range(depth):
        j = i // 2
        mod = _ada_mod(c_all, ada_w[i], ada_b[i])
        mod_p = _Mod(mod[:batch].reshape(batch * 6, 1, D_MODEL), False, lambda tm: seq // tm)
        mod_s = _Mod(jnp.repeat(mod[batch:].reshape(dbatch, 6, D_MODEL).transpose(1, 0, 2), t_new, axis=1),
                     True, None)
        ng0, ng1 = norm_g[i, 0].reshape(1, -1), norm_g[i, 1].reshape(1, -1)

        if i % 2 == 0:
            lam_init = 0.8 - 0.6 * math.exp(-0.3 * i)
            w_in = a_w_in[j].astype(BF16)
            gq, gk = _pair_tile(a_g_q[j]), _pair_tile(a_g_k[j])
            lamv = jnp.stack([a_lq1[j], a_lk1[j], a_lq2[j], a_lk2[j]])
            gsub = a_g_sub[j].reshape(1, -1)
            qp, kp, vp = _inproj(xp, mod_p, ng0, w_in, gq, gk, cos_p, sin_p, rope=True, tm=TM_PROJ)
            qs, ks, vs = _inproj(xs, mod_s, ng0, w_in, gq, gk, cos_s, sin_s, rope=True, tm=tm_s)
            op = _flash_diff(qp, kp, vp, lamv, gsub, batch=batch, seq=seq, lam_init=lam_init)
            os_ = _diff_sample(qs, cache_a_k[j].reshape(dbatch * past_a, D_MODEL),
                               cache_a_v[j].reshape(dbatch * past_a, D_MODEL), ks, vs, lamv, gsub,
                               batch=dbatch, t_new=t_new, past=past_a, lam_init=lam_init)
            w_out = a_w_out[j].astype(BF16)
            heads = D_MODEL // LANES
            outs["akp"].append(kp.reshape(batch, seq, heads, 2, DH))
            outs["avp"].append(vp.reshape(batch, seq, heads, 2 * DH))
            outs["aks"].append(ks.reshape(dbatch, t_new, heads, 2, DH))
            outs["avs"].append(vs.reshape(dbatch, t_new, heads, 2 * DH))
        else:
            w_in = b_w_in[j].astype(BF16)
            gq, gk = _pair_tile(b_g_q[j]), _pair_tile(b_g_k[j])
            qp, kp, vp = _inproj(xp, mod_p, ng0, w_in, gq, gk, cos_p, sin_p, rope=False, tm=TM_PROJ)
            qs, ks, vs = _inproj(xs, mod_s, ng0, w_in, gq, gk, cos_s, sin_s, rope=False, tm=tm_s)
            op = _band_prompt(qp, kp, vp, b_rel_bias[j], batch=batch, seq=seq)
            os_ = _band_sample(qs, cache_b_k[j].reshape(dbatch * past_b, D_MODEL),
                               cache_b_v[j].reshape(dbatch * past_b, D_MODEL), ks, vs, b_rel_bias[j],
                               batch=dbatch, t_new=t_new, past=past_b)
            w_out = b_w_out[j].astype(BF16)
            heads = D_MODEL // DH
            keep = min(BAND_PAST, seq)
            k4 = kp.reshape(batch, seq, heads, DH)
            v4 = vp.reshape(batch, seq, heads, DH)
            outs["bkp"].append(k4[:, seq - keep:])
            outs["bvp"].append(v4[:, seq - keep:])
            outs["bks"].append(jnp.concatenate([cache_b_k[j], ks.reshape(dbatch, t_new, heads, DH)], axis=1)[:, t_new:])
            outs["bvs"].append(jnp.concatenate([cache_b_v[j], vs.reshape(dbatch, t_new, heads, DH)], axis=1)[:, t_new:])

        xp = _outproj(op, w_out, xp, mod_p, tm=TM_PROJ)
        xs = _outproj(os_, w_out, xs, mod_s, tm=tm_s)

        wq = peer_w_q[i].astype(BF16)
        keys = peer_sub_keys[i].astype(BF16).reshape(PEER_HEADS * 2, N_KEYS, -1)
        u_bf, v_bf = peer_u[i].astype(BF16), peer_v[i].astype(BF16)
        xp = _peer(xp, mod_p, ng1, wq, keys, u_bf, v_bf, tm=TM_DENSE)
        xs = _peer(xs, mod_s, ng1, wq, keys, u_bf, v_bf, tm=tmd_s)

    st = lambda n: jnp.stack(outs[n], 0)
    return (xp.reshape(x_prompt.shape), xs.reshape(x_sample.shape),
            st("akp"), st("avp"), st("aks"), st("avs"), st("bkp"), st("bvp"), st("bks"), st("bvs"))
```

```python
import functools
import math

import numpy as np
import jax
import jax.numpy as jnp
from jax import lax
from jax.experimental import pallas as pl
from jax.experimental.pallas import tpu as pltpu

F32 = jnp.float32
BF16 = jnp.bfloat16

D_MODEL = 1024
CHUNK = 64
EPS = 1e-6
NEG = -1e30
ROPE_THETA = 10000.0
DH = 64
LANES = 128
PREV_CHUNKS = 8
BAND_PAST = PREV_CHUNKS * CHUNK
REL_CLIP = 128
PEER_HEADS = 8
N_KEYS = 128
TOPK = 16
VMEM_LIMIT = 48 * 1024 * 1024

TM_PROJ = 256
TQ_FLASH = 1024
FLASH_ROWS = 1024
TQ_BAND = 256
BAND_QTILES = 4
TM_DENSE = 512
TOKEN_UNROLL = 64
W3_PAD = 8

_NT = (((1,), (1,)), ((), ()))


def _cparams(sem):
    return pltpu.CompilerParams(dimension_semantics=sem, vmem_limit_bytes=VMEM_LIMIT)


def _dot(a, b):
    return jnp.dot(a, b, preferred_element_type=F32)


def _dot_nt(a, b):
    return lax.dot_general(a, b, _NT, preferred_element_type=F32)


def _split(a):
    hi = a.astype(BF16)
    lo = (a - hi.astype(F32)).astype(BF16)
    return hi, lo


def _ada_kernel(c_ref, w_ref, b_ref, o_ref):
    c = c_ref[...]
    a = c * (1.0 / (1.0 + jnp.exp(-c)))
    ah, al = _split(a)
    wh, wl = _split(w_ref[...])
    o_ref[...] = _dot(ah, wh) + _dot(al, wh) + _dot(ah, wl) + b_ref[...]


def _ada_mod(c_all, w, b):
    n, d = c_all.shape
    nout = w.shape[1]
    tn = 512
    return pl.pallas_call(
        _ada_kernel,
        grid=(nout // tn,),
        in_specs=[pl.BlockSpec((n, d), lambda j: (0, 0)),
                  pl.BlockSpec((d, tn), lambda j: (0, j)),
                  pl.BlockSpec((1, tn), lambda j: (0, j))],
        out_specs=pl.BlockSpec((n, tn), lambda j: (0, j)),
        out_shape=jax.ShapeDtypeStruct((n, nout), F32),
        compiler_params=_cparams(("parallel",)),
        name="ada_mod",
    )(c_all, w, b.reshape(1, nout))


class _Mod:
    def __init__(self, arr, per_row, tiles_per_batch):
        self.arr, self.per_row, self.tpb = arr, per_row, tiles_per_batch

    def spec(self, k, tm, tile=lambda i: i):
        if self.per_row:
            return pl.BlockSpec((None, tm, D_MODEL), lambda i, *_: (k, tile(i), 0))
        tpb = self.tpb(tm)
        return pl.BlockSpec((None, 1, D_MODEL), lambda i, *_: ((tile(i) // tpb) * 6 + k, 0, 0))


def _modulated(x, ng, scale, shift):
    ms = jnp.mean(x * x, axis=-1, keepdims=True)
    return (x * lax.rsqrt(ms + EPS) * ng) * (1.0 + scale) + shift


def _inproj_kernel(x_ref, sh_ref, sc_ref, ng_ref, w_ref, gq_ref, gk_ref, cos_ref, sin_ref,
                   q_ref, k_ref, v_ref, *, rope, qscale):
    tm = x_ref.shape[0]
    hb = _modulated(x_ref[...], ng_ref[...], sc_ref[...], sh_ref[...]).astype(BF16)
    lane = lax.broadcasted_iota(jnp.int32, (tm, LANES), 1)
    lo = lane < DH
    swap_sel = (lane & (DH // 2)) != 0

    def norm_rope(xb, g):
        x2 = xb * xb
        slo = jnp.sum(jnp.where(lo, x2, 0.0), axis=-1, keepdims=True)
        shi = jnp.sum(jnp.where(lo, 0.0, x2), axis=-1, keepdims=True)
        ms = jnp.where(lo, slo, shi) * (1.0 / DH)
        y = xb * lax.rsqrt(ms + EPS) * g
        if rope:
            sw = jnp.where(swap_sel, pltpu.roll(y, DH // 2, 1), pltpu.roll(y, LANES - DH // 2, 1))
            y = y * cos_ref[...] + sw * sin_ref[...]
        return y

    nblk = w_ref.shape[1] // (2 * LANES)
    for j in range(nblk):
        acc = _dot(hb, w_ref[:, j * 2 * LANES:(j + 1) * 2 * LANES])
        for half in range(2):
            blk = acc[:, half * LANES:(half + 1) * LANES]
            col = j * 2 * LANES + half * LANES
            if col < D_MODEL:
                q_ref[:, col:col + LANES] = (norm_rope(blk, gq_ref[...]) * qscale).astype(BF16)
            elif col < 2 * D_MODEL:
                k_ref[:, col - D_MODEL:col - D_MODEL + LANES] = norm_rope(blk, gk_ref[...])
            else:
                v_ref[:, col - 2 * D_MODEL:col - 2 * D_MODEL + LANES] = blk


def _inproj(x, mod, ng, w_bf, gq, gk, cos, sin, *, rope, tm):
    t = x.shape[0]
    nrep = cos.shape[0] // tm
    row = lambda i: (i, 0)
    full = lambda i: (0, 0)
    tab = lambda i: (i % nrep, 0)
    return pl.pallas_call(
        functools.partial(_inproj_kernel, rope=rope, qscale=DH ** -0.5),
        grid=(t // tm,),
        in_specs=[pl.BlockSpec((tm, D_MODEL), row), mod.spec(0, tm), mod.spec(1, tm),
                  pl.BlockSpec((1, D_MODEL), full),
                  pl.BlockSpec(w_bf.shape, full),
                  pl.BlockSpec((1, LANES), full), pl.BlockSpec((1, LANES), full),
                  pl.BlockSpec((tm, LANES), tab), pl.BlockSpec((tm, LANES), tab)],
        out_specs=[pl.BlockSpec((tm, D_MODEL), row)] * 3,
        out_shape=[jax.ShapeDtypeStruct((t, D_MODEL), BF16),
                   jax.ShapeDtypeStruct((t, D_MODEL), F32),
                   jax.ShapeDtypeStruct((t, D_MODEL), F32)],
        compiler_params=_cparams(("parallel",)),
        name="qkv_proj",
    )(x, mod.arr, mod.arr, ng, w_bf, gq, gk, cos, sin)


def _cache_layout_kernel(k_ref, v_ref, k4_ref, v4_ref):
    heads = v4_ref.shape[1]
    for h in range(heads):
        kb = k_ref[:, h * LANES:(h + 1) * LANES]
        v4_ref[:, h, :] = v_ref[:, h * LANES:(h + 1) * LANES]
        for c in range(2):
            k4_ref[:, h, c, :] = kb[:, c * DH:(c + 1) * DH]


def _cache_layout(k, v, *, tm):
    t = k.shape[0]
    heads = D_MODEL // LANES
    row = lambda i: (i, 0)
    return pl.pallas_call(
        _cache_layout_kernel,
        grid=(t // tm,),
        in_specs=[pl.BlockSpec((tm, D_MODEL), row)] * 2,
        out_specs=[pl.BlockSpec((tm, heads, 2, DH), lambda i: (i, 0, 0, 0)),
                   pl.BlockSpec((tm, heads, LANES), lambda i: (i, 0, 0))],
        out_shape=[jax.ShapeDtypeStruct((t, heads, 2, DH), F32), jax.ShapeDtypeStruct((t, heads, LANES), F32)],
        compiler_params=_cparams(("parallel",)),
        name="cache_layout",
    )(k, v)


def _diff_lambda(lamv_ref, lam_init):
    lv = lamv_ref[...]
    e1 = jnp.exp(jnp.sum(lv[0:1] * lv[1:2], axis=-1, keepdims=True))
    e2 = jnp.exp(jnp.sum(lv[2:3] * lv[3:4], axis=-1, keepdims=True))
    return e1 - e2 + lam_init


def _diff_finish(o0, o1, lamv_ref, gsub_ref, lam_init):
    o = o0 - _diff_lambda(lamv_ref, lam_init) * o1
    ms = jnp.mean(o * o, axis=-1, keepdims=True)
    return ((o * lax.rsqrt(ms + EPS) * gsub_ref[...]) * (1.0 - lam_init)).astype(BF16)


def _component_queries(q):
    lane = lax.broadcasted_iota(jnp.int32, q.shape, 1)
    zero = jnp.zeros_like(q)
    return jnp.where(lane < DH, q, zero), jnp.where(lane < DH, zero, q)


def _flash_diff_kernel(qt_ref, kt_ref, q_ref, k_ref, v_ref, lamv_ref, gsub_ref, o_ref,
                       m_ref, l_ref, a_ref, *, lam_init):
    p = pl.program_id(2)
    qi = qt_ref[p]
    ki = kt_ref[p]
    tq = q_ref.shape[0]

    @pl.when(ki == 0)
    def _():
        m_ref[...] = jnp.full(m_ref.shape, NEG, F32)
        l_ref[...] = jnp.zeros(l_ref.shape, F32)
        a_ref[...] = jnp.zeros(a_ref.shape, F32)

    def update(masked):
        kb = k_ref[...].astype(BF16)
        vb = v_ref[...].astype(BF16)
        qq = jnp.concatenate(_component_queries(q_ref[...]), axis=0)
        for r in range(2 * tq // FLASH_ROWS):
            rows = slice(r * FLASH_ROWS, (r + 1) * FLASH_ROWS)
            q0 = (r * FLASH_ROWS) % tq
            ncol = min(q0 + FLASH_ROWS, tq) if masked else tq
            s = _dot_nt(qq[rows], kb[:ncol])
            if masked:
                row_c = ((lax.broadcasted_iota(jnp.int32, (FLASH_ROWS, ncol), 0) + q0) % tq) // CHUNK
                col_c = lax.broadcasted_iota(jnp.int32, (FLASH_ROWS, ncol), 1) // CHUNK
                s = jnp.where(col_c <= row_c, s, NEG)
            m_prev = m_ref[rows, :]
            m_new = jnp.maximum(m_prev, jnp.max(s, axis=-1, keepdims=True))
            alpha = jnp.exp(m_prev - m_new)
            pm = jnp.exp(s - jnp.tile(m_new, (1, ncol // LANES)))
            l_ref[rows, :] = alpha * l_ref[rows, :] + jnp.sum(pm, axis=-1, keepdims=True)
            a_ref[rows, :] = alpha * a_ref[rows, :] + _dot(pm.astype(BF16), vb[:ncol])
            m_ref[rows, :] = m_new

    @pl.when(ki < qi)
    def _():
        update(False)

    @pl.when(ki == qi)
    def _():
        update(True)
        o = a_ref[...] / l_ref[...]
        o_ref[...] = _diff_finish(o[:tq], o[tq:], lamv_ref, gsub_ref, lam_init)


def _flash_diff(q, k, v, lamv, gsub, *, batch, seq, lam_init):
    tq = min(TQ_FLASH, seq)
    nq = seq // tq
    heads = D_MODEL // LANES
    pairs = [(a, b) for a in range(nq) for b in range(a + 1)]
    qt = jnp.asarray([a for a, _ in pairs], jnp.int32)
    kt = jnp.asarray([b for _, b in pairs], jnp.int32)
    qmap = lambda b, h, p, qt, kt: (b * nq + qt[p], h)
    kmap = lambda b, h, p, qt, kt: (b * nq + kt[p], h)
    return pl.pallas_call(
        functools.partial(_flash_diff_kernel, lam_init=lam_init),
        grid_spec=pltpu.PrefetchScalarGridSpec(
            num_scalar_prefetch=2,
            grid=(batch, heads, len(pairs)),
            in_specs=[pl.BlockSpec((tq, LANES), qmap),
                      pl.BlockSpec((tq, LANES), kmap),
                      pl.BlockSpec((tq, LANES), kmap),
                      pl.BlockSpec(lamv.shape, lambda *_: (0, 0)),
                      pl.BlockSpec((1, LANES), lambda *_: (0, 0))],
            out_specs=pl.BlockSpec((tq, LANES), qmap),
            scratch_shapes=[pltpu.VMEM((2 * tq, LANES), F32)] * 3),
        out_shape=jax.ShapeDtypeStruct(q.shape, BF16),
        compiler_params=_cparams(("parallel", "parallel", "arbitrary")),
        name="flash_diff_attn",
    )(qt, kt, q, k, v, lamv, gsub)


def _joint_softmax_attend(qc, pieces):
    ss = []
    for kb, _, bias, visible in pieces:
        s = _dot_nt(qc, kb)
        if bias is not None:
            s = s + bias
        if visible is not None:
            s = jnp.where(visible, s, NEG)
        ss.append(s)
    m = functools.reduce(jnp.maximum, [jnp.max(s, axis=-1, keepdims=True) for s in ss])
    l = 0.0
    o = 0.0
    for s, (_, vb, _, _) in zip(ss, pieces):
        pm = jnp.exp(s - m)
        l = l + jnp.sum(pm, axis=-1, keepdims=True)
        o = o + _dot(pm.astype(BF16), vb)
    return o / l


def _diff_sample_kernel(q_ref, ck_ref, cv_ref, nk_ref, nv_ref, lamv_ref, gsub_ref, o_ref, *, lam_init):
    t_new = q_ref.shape[0]
    pieces = [(ck_ref[...].astype(BF16), cv_ref[...].astype(BF16), None, None),
              (nk_ref[...].astype(BF16), nv_ref[...].astype(BF16), None, None)]
    o = _joint_softmax_attend(jnp.concatenate(_component_queries(q_ref[...]), axis=0), pieces)
    o_ref[...] = _diff_finish(o[:t_new], o[t_new:], lamv_ref, gsub_ref, lam_init)


def _diff_sample(q, ck, cv, nk, nv, lamv, gsub, *, batch, t_new, past, lam_init):
    heads = D_MODEL // LANES
    bh = lambda b, h: (b, h)
    return pl.pallas_call(
        functools.partial(_diff_sample_kernel, lam_init=lam_init),
        grid=(batch, heads),
        in_specs=[pl.BlockSpec((t_new, LANES), bh),
                  pl.BlockSpec((past, LANES), bh), pl.BlockSpec((past, LANES), bh),
                  pl.BlockSpec((t_new, LANES), bh), pl.BlockSpec((t_new, LANES), bh),
                  pl.BlockSpec(lamv.shape, lambda b, h: (0, 0)),
                  pl.BlockSpec((1, LANES), lambda b, h: (0, 0))],
        out_specs=pl.BlockSpec((t_new, LANES), bh),
        out_shape=jax.ShapeDtypeStruct(q.shape, BF16),
        compiler_params=_cparams(("parallel", "parallel")),
        name="diff_attn_sample",
    )(q, ck, cv, nk, nv, lamv, gsub)


def _band_finish(o):
    tq = o.shape[0] // 2
    lane = lax.broadcasted_iota(jnp.int32, (tq, LANES), 1)
    return jnp.where(lane < DH, o[:tq], o[tq:]).astype(BF16)


def _band_prompt_kernel(q_ref, *refs):
    nkv = BAND_QTILES + 2
    k_refs, v_refs, bias_ref, o_ref = refs[:nkv], refs[nkv:2 * nkv], refs[2 * nkv], refs[2 * nkv + 1]
    first = pl.program_id(2) * BAND_QTILES
    tq = q_ref.shape[0] // BAND_QTILES
    row_c = (lax.broadcasted_iota(jnp.int32, (2 * tq, tq), 0) % tq) // CHUNK
    col_c = lax.broadcasted_iota(jnp.int32, (2 * tq, tq), 1) // CHUNK
    kvs = [(k_ref[...].astype(BF16), v_ref[...].astype(BF16)) for k_ref, v_ref in zip(k_refs, v_refs)]
    for sub in range(BAND_QTILES):
        pieces = []
        for r in range(3):
            before_start = jnp.where(first + sub + r - 2 < 0, 4 * PREV_CHUNKS, 0)
            kc = col_c + (r - 2) * (tq // CHUNK) + before_start
            visible = (kc <= row_c) & (kc >= row_c - PREV_CHUNKS)
            pieces.append((*kvs[sub + r], bias_ref[:, r * tq:(r + 1) * tq], visible))
        rows = slice(sub * tq, (sub + 1) * tq)
        qq = jnp.concatenate(_component_queries(q_ref[rows, :]), axis=0)
        o_ref[rows, :] = _band_finish(_joint_softmax_attend(qq, pieces))


def _band_bias_tiles(table, nq, nk, c0):
    n = nq + nk - 1
    diag = jnp.clip(c0 + nq - 1 - np.arange(n), -REL_CLIP, REL_CLIP) + REL_CLIP
    e = table[:, diag]
    a = jnp.tile(e, (1, nq + 1))[:, :nq * (n + 1)].reshape(-1, nq, n + 1)[:, ::-1, :nk]
    return a.reshape(table.shape[0] // 2, 2 * nq, nk)


def _band_prompt(q, k, v, table, *, batch, seq):
    tq = TQ_BAND
    nsub = BAND_QTILES
    assert BAND_PAST == 2 * tq and seq % (nsub * tq) == 0
    nq = seq // tq
    hp = D_MODEL // LANES
    bias = _band_bias_tiles(table, tq, 3 * tq, 2 * tq)
    qmap = lambda h, b, i: (b * (nq // nsub) + i, h)
    kmap = lambda r: (lambda h, b, i: (b * nq + jnp.maximum(nsub * i + r - 2, 0), h))
    kv_specs = [pl.BlockSpec((tq, LANES), kmap(r)) for r in range(nsub + 2)]
    return pl.pallas_call(
        _band_prompt_kernel,
        grid=(hp, batch, nq // nsub),
        in_specs=[pl.BlockSpec((nsub * tq, LANES), qmap)] + kv_specs * 2
                 + [pl.BlockSpec((None, 2 * tq, 3 * tq), lambda h, b, i: (h, 0, 0))],
        out_specs=pl.BlockSpec((nsub * tq, LANES), qmap),
        out_shape=jax.ShapeDtypeStruct(q.shape, BF16),
        compiler_params=_cparams(("parallel", "parallel", "parallel")),
        name="band_attn_prompt",
    )(q, *([k] * (nsub + 2)), *([v] * (nsub + 2)), bias)


def _band_sample_kernel(q_ref, ck_ref, cv_ref, nk_ref, nv_ref, bc_ref, bn_ref, o_ref):
    pieces = [(ck_ref[...].astype(BF16), cv_ref[...].astype(BF16), bc_ref[...], None),
              (nk_ref[...].astype(BF16), nv_ref[...].astype(BF16), bn_ref[...], None)]
    qq = jnp.concatenate(_component_queries(q_ref[...]), axis=0)
    o_ref[...] = _band_finish(_joint_softmax_attend(qq, pieces))


def _band_sample(q, ck, cv, nk, nv, table, *, batch, t_new, past):
    hp = D_MODEL // LANES
    bias = _band_bias_tiles(table, t_new, past + t_new, past)
    bias_c, bias_n = bias[..., :past], bias[..., past:]
    hb = lambda h, b: (b, h)
    return pl.pallas_call(
        _band_sample_kernel,
        grid=(hp, batch),
        in_specs=[pl.BlockSpec((t_new, LANES), hb),
                  pl.BlockSpec((past, LANES), hb), pl.BlockSpec((past, LANES), hb),
                  pl.BlockSpec((t_new, LANES), hb), pl.BlockSpec((t_new, LANES), hb),
                  pl.BlockSpec((None, 2 * t_new, past), lambda h, b: (h, 0, 0)),
                  pl.BlockSpec((None, 2 * t_new, t_new), lambda h, b: (h, 0, 0))],
        out_specs=pl.BlockSpec((t_new, LANES), hb),
        out_shape=jax.ShapeDtypeStruct(q.shape, BF16),
        compiler_params=_cparams(("parallel", "parallel")),
        name="band_attn_sample",
    )(q, ck, cv, nk, nv, bias_c, bias_n)


def _outproj_kernel(o_ref, w_ref, x_ref, gate_ref, y_ref):
    y_ref[...] = x_ref[...] + gate_ref[...] * _dot(o_ref[...], w_ref[...])


def _outproj(o, w_bf, x, mod, *, tm):
    t = x.shape[0]
    row = lambda i: (i, 0)
    return pl.pallas_call(
        _outproj_kernel,
        grid=(t // tm,),
        in_specs=[pl.BlockSpec((tm, D_MODEL), row), pl.BlockSpec(w_bf.shape, lambda i: (0, 0)),
                  pl.BlockSpec((tm, D_MODEL), row), mod.spec(2, tm)],
        out_specs=pl.BlockSpec((tm, D_MODEL), row),
        out_shape=jax.ShapeDtypeStruct(x.shape, F32),
        compiler_params=_cparams(("parallel",)),
        name="out_proj",
    )(o, w_bf, x, mod.arr)


def _top_rows(s, k):
    if s.shape[1] > LANES:
        cols = [_top_rows(s[:, c:c + LANES], k) for c in range(0, s.shape[1], LANES)]
        return tuple(jnp.concatenate(x, axis=1) for x in zip(*cols))
    rows = s.shape[0]
    rid = lax.broadcasted_iota(jnp.int32, s.shape, 0).astype(F32)
    vals, ids = [], []
    for _ in range(k):
        m = jnp.max(s, axis=0, keepdims=True)
        i = jnp.min(jnp.where(s == m, rid, float(rows)), axis=0, keepdims=True)
        vals.append(m)
        ids.append(i)
        s = jnp.where(rid == i, -jnp.inf, s)
    return jnp.concatenate(vals, axis=0), jnp.concatenate(ids, axis=0)


_SUB = 8
_STAIR_PIECES = ([(0, 1, 0, _SUB), (0, 1, _SUB, _SUB), (1, 1, 0, _SUB)]
                 + [(a, 1, 0, TOPK // (a + 1)) for a in range(2, _SUB)] + [(_SUB, _SUB, 0, 1)])


def _stair_candidates(s1, s2):
    sub = lax.broadcasted_iota(jnp.int32, (_SUB, s1.shape[1]), 0)
    pieces = []
    for a0, na, b0, nb in _STAIR_PIECES:
        if na == 1:
            piece = s1[a0:a0 + 1, :] + s2[b0:b0 + _SUB, :]
            if nb < _SUB:
                piece = jnp.where(sub < nb, piece, -jnp.inf)
        else:
            piece = s1[a0:a0 + na, :] + s2[b0:b0 + 1, :]
        pieces.append(piece)
    return jnp.concatenate(pieces, axis=0)


def _stair_ranks(pos):
    a = jnp.zeros(pos.shape, F32)
    b = pos
    for p, (a0, na, b0, nb) in enumerate(_STAIR_PIECES):
        start = float(p * _SUB)
        inside = pos >= start
        if na == 1:
            a = jnp.where(inside, float(a0), a)
            b = jnp.where(inside, pos - start + float(b0), b)
        else:
            a = jnp.where(inside, pos - start + float(a0), a)
            b = jnp.where(inside, float(b0), b)
    return a, b


def _pick_rows(sel, table):
    out = jnp.zeros(sel.shape, F32)
    for a in range(table.shape[0]):
        out = out + jnp.where(sel == float(a), table[a:a + 1, :], 0.0)
    return out


def _select_unit(q_scr, keys_ref, sel_t_scr, h, part):
    tokens = pl.ds(pl.multiple_of(part * LANES, LANES), LANES)
    top = [_top_rows(_dot_nt(keys_ref[2 * h + c], q_scr[2 * h + c, tokens, :]), TOPK) for c in range(2)]
    (s1, i1), (s2, i2) = top
    top_s, pos = _top_rows(_stair_candidates(s1, s2), TOPK)
    a_sel, b_sel = _stair_ranks(pos)
    e = jnp.exp(top_s - top_s[0:1, :])
    rows = pl.ds(pl.multiple_of(h * TOPK, TOPK), TOPK)
    sel_t_scr[0, part, rows, :] = _pick_rows(a_sel, i1)
    sel_t_scr[1, part, rows, :] = _pick_rows(b_sel, i2)
    sel_t_scr[2, part, rows, :] = e / jnp.sum(e, axis=0, keepdims=True)


def _peer_kernel(xs_ref, sh_ref, sc_ref, ng_ref, wq_ref, keys_ref, u_ref, v_ref, xr_ref, gate_ref, y_ref,
                 hb_scr, q_scr, sel_t_scr, sel_scr, w3_ref, acc_ref):
    i = pl.program_id(0)
    j = pl.program_id(1)
    tm = xs_ref.shape[0]
    nparts = tm // LANES
    half = N_KEYS // 2
    pitch = w3_ref.shape[0] // half
    e_tile = u_ref.shape[0]
    slot_new, slot_dense = i % 2, (i + 1) % 2

    @pl.when((i == 0) & (j == 0))
    def _():
        w3_ref[...] = jnp.zeros(w3_ref.shape, w3_ref.dtype)
        hb_scr[1] = jnp.zeros(hb_scr.shape[1:], hb_scr.dtype)

    @pl.when((i > 0) & (j == 0))
    def _():
        for k in range(3):
            for part in range(nparts):
                sel_scr[k, part * LANES:(part + 1) * LANES, :] = sel_t_scr[k, part].T
        row = lax.broadcasted_iota(jnp.int32, (N_KEYS, N_KEYS), 0)
        i1_ids = jnp.where(row < half, 2 * row, 2 * (row - half) + 1).astype(F32)
        i2_ids = row.astype(F32)

        def token(t, carry):
            arow = sel_scr[0, pl.ds(t, 1), :]
            brow = sel_scr[1, pl.ds(t, 1), :]
            grow = sel_scr[2, pl.ds(t, 1), :]
            oa = jnp.where(arow == i1_ids, 1.0, 0.0).astype(BF16)
            ob = jnp.where(brow == i2_ids, grow, 0.0).astype(BF16)
            w = _dot_nt(oa, ob)
            w3_ref[pl.ds(t, half, stride=pitch), :] = pltpu.pack_elementwise([w[:half], w[half:]], packed_dtype=BF16)
            return carry

        lax.fori_loop(0, tm, token, 0, unroll=TOKEN_UNROLL)

    @pl.when(j == 0)
    def _():
        acc_ref[...] = jnp.zeros(acc_ref.shape, F32)
        hb = _modulated(xs_ref[...], ng_ref[...], sc_ref[...], sh_ref[...]).astype(BF16)
        hb_scr[slot_new] = hb
        q_all = _dot(hb, wq_ref[...])
        for hc in range(2 * PEER_HEADS):
            q_scr[hc] = q_all[:, hc * N_KEYS:(hc + 1) * N_KEYS].astype(BF16)

    _select_unit(q_scr, keys_ref, sel_t_scr, j // nparts, j % nparts)

    cols = []
    for q in range(e_tile // (2 * N_KEYS)):
        word = w3_ref[pl.ds(pl.multiple_of((e_tile // (2 * N_KEYS) * j + q) * pitch, 8), tm), :]
        cols += [pltpu.unpack_elementwise(word, index=k, packed_dtype=BF16, unpacked_dtype=F32) for k in range(2)]
    w = jnp.concatenate(cols, axis=1)
    hid = _dot_nt(hb_scr[slot_dense], u_ref[...])
    act = 0.5 * hid * (1.0 + lax.erf(hid * math.sqrt(0.5)))
    acc_ref[...] += _dot((w * act).astype(BF16), v_ref[...])

    @pl.when((i > 0) & (j == pl.num_programs(1) - 1))
    def _():
        y_ref[...] = xr_ref[...] + gate_ref[...] * acc_ref[...]


def _peer(x, mod, ng, wq_bf, keys_bf, u_bf, v_bf, *, tm):
    t = x.shape[0]
    nt = t // tm
    units = PEER_HEADS * (tm // LANES)
    e_tile = u_bf.shape[0] // units
    assert e_tile % (2 * N_KEYS) == 0 and tm % TOKEN_UNROLL == 0
    new_tile = lambda i: jnp.minimum(i, nt - 1)
    dense_tile = lambda i: jnp.maximum(i - 1, 0)
    full = lambda i, j: (0, 0)
    return pl.pallas_call(
        _peer_kernel,
        grid=(nt + 1, units),
        in_specs=[pl.BlockSpec((tm, D_MODEL), lambda i, j: (new_tile(i), 0)),
                  mod.spec(3, tm, new_tile), mod.spec(4, tm, new_tile),
                  pl.BlockSpec((1, D_MODEL), full),
                  pl.BlockSpec(wq_bf.shape, full),
                  pl.BlockSpec(keys_bf.shape, lambda i, j: (0, 0, 0)),
                  pl.BlockSpec((e_tile, D_MODEL), lambda i, j: (j, 0)),
                  pl.BlockSpec((e_tile, D_MODEL), lambda i, j: (j, 0)),
                  pl.BlockSpec((tm, D_MODEL), lambda i, j: (dense_tile(i), 0)),
                  mod.spec(5, tm, dense_tile)],
        out_specs=pl.BlockSpec((tm, D_MODEL), lambda i, j: (dense_tile(i), 0)),
        out_shape=jax.ShapeDtypeStruct(x.shape, F32),
        scratch_shapes=[pltpu.VMEM((2, tm, D_MODEL), BF16),
                        pltpu.VMEM((2 * PEER_HEADS, tm, N_KEYS), BF16),
                        pltpu.VMEM((3, tm // LANES, PEER_HEADS * TOPK, LANES), F32),
                        pltpu.VMEM((3, tm, PEER_HEADS * TOPK), F32),
                        pltpu.VMEM((N_KEYS // 2 * (tm + W3_PAD), N_KEYS), jnp.int32),
                        pltpu.VMEM((tm, D_MODEL), F32)],
        compiler_params=_cparams(("arbitrary", "arbitrary")),
        name="peer",
    )(x, mod.arr, mod.arr, ng, wq_bf, keys_bf, u_bf, v_bf, x, mod.arr)


def _rope_tables(pos):
    half = DH // 2
    inv = ROPE_THETA ** (-jnp.arange(half, dtype=F32) / half)
    ang = pos.astype(F32)[:, None] * inv[None, :]
    cos, sin = jnp.cos(ang), jnp.sin(ang)
    return jnp.tile(cos, (1, 4)), jnp.concatenate([-sin, sin, -sin, sin], axis=1)


def _pair_tile(g):
    return jnp.tile(g.reshape(1, -1), (1, LANES // g.shape[-1]))


def kernel(x_prompt, x_sample, c_prompt, c_sample, cache_a_k, cache_a_v, cache_b_k, cache_b_v, ada_w, ada_b, norm_g, a_w_in, a_g_q, a_g_k, a_lq1, a_lk1, a_lq2, a_lk2, a_g_sub, a_w_out, b_w_in, b_g_q, b_g_k, b_rel_bias, b_w_out, peer_w_q, peer_sub_keys, peer_u, peer_v):
    batch, seq, _ = x_prompt.shape
    dbatch, t_new, _ = x_sample.shape
    past_a = cache_a_k.shape[2]
    past_b = cache_b_k.shape[2]
    depth = ada_w.shape[0]
    tp, ts = batch * seq, dbatch * t_new
    tm_s = min(TM_PROJ, ts)
    tmd_s = min(TM_DENSE // 2, ts)

    xp = x_prompt.reshape(tp, D_MODEL)
    xs = x_sample.reshape(ts, D_MODEL)
    c_all = jnp.concatenate([c_prompt, c_sample], axis=0)

    cos_p, sin_p = _rope_tables(jnp.arange(seq))
    cos_s, sin_s = _rope_tables(past_a + jnp.arange(t_new))
    cos_s, sin_s = jnp.tile(cos_s, (tm_s // t_new, 1)), jnp.tile(sin_s, (tm_s // t_new, 1))

    outs = {n: [] for n in ("akp", "avp", "aks", "avs", "bkp", "bvp", "bks", "bvs")}
    for i in range(depth):
        j = i // 2
        mod = _ada_mod(c_all, ada_w[i], ada_b[i])
        mod_p = _Mod(mod[:batch].reshape(batch * 6, 1, D_MODEL), False, lambda tm: seq // tm)
        mod_s = _Mod(jnp.repeat(mod[batch:].reshape(dbatch, 6, D_MODEL).transpose(1, 0, 2), t_new, axis=1),
                     True, None)
        ng0, ng1 = norm_g[i, 0].reshape(1, -1), norm_g[i, 1].reshape(1, -1)

        if i % 2 == 0:
            lam_init = 0.8 - 0.6 * math.exp(-0.3 * i)
            w_in = a_w_in[j].astype(BF16)
            gq, gk = _pair_tile(a_g_q[j]), _pair_tile(a_g_k[j])
            lamv = jnp.stack([a_lq1[j], a_lk1[j], a_lq2[j], a_lk2[j]])
            gsub = a_g_sub[j].reshape(1, -1)
            qp, kp, vp = _inproj(xp, mod_p, ng0, w_in, gq, gk, cos_p, sin_p, rope=True, tm=TM_PROJ)
            qs, ks, vs = _inproj(xs, mod_s, ng0, w_in, gq, gk, cos_s, sin_s, rope=True, tm=tm_s)
            op = _flash_diff(qp, kp, vp, lamv, gsub, batch=batch, seq=seq, lam_init=lam_init)
            os_ = _diff_sample(qs, cache_a_k[j].reshape(dbatch * past_a, D_MODEL),
                               cache_a_v[j].reshape(dbatch * past_a, D_MODEL), ks, vs, lamv, gsub,
                               batch=dbatch, t_new=t_new, past=past_a, lam_init=lam_init)
            w_out = a_w_out[j].astype(BF16)
            heads = D_MODEL // LANES
            k4p, v4p = _cache_layout(kp, vp, tm=TM_PROJ)
            k4s, v4s = _cache_layout(ks, vs, tm=tm_s)
            outs["akp"].append(k4p.reshape(batch, seq, heads, 2, DH))
            outs["avp"].append(v4p.reshape(batch, seq, heads, 2 * DH))
            outs["aks"].append(k4s.reshape(dbatch, t_new, heads, 2, DH))
            outs["avs"].append(v4s.reshape(dbatch, t_new, heads, 2 * DH))
        else:
            w_in = b_w_in[j].astype(BF16)
            gq, gk = _pair_tile(b_g_q[j]), _pair_tile(b_g_k[j])
            qp, kp, vp = _inproj(xp, mod_p, ng0, w_in, gq, gk, cos_p, sin_p, rope=False, tm=TM_PROJ)
            qs, ks, vs = _inproj(xs, mod_s, ng0, w_in, gq, gk, cos_s, sin_s, rope=False, tm=tm_s)
            op = _band_prompt(qp, kp, vp, b_rel_bias[j], batch=batch, seq=seq)
            os_ = _band_sample(qs, cache_b_k[j].reshape(dbatch * past_b, D_MODEL),
                               cache_b_v[j].reshape(dbatch * past_b, D_MODEL), ks, vs, b_rel_bias[j],
                               batch=dbatch, t_new=t_new, past=past_b)
            w_out = b_w_out[j].astype(BF16)
            heads = D_MODEL // DH
            keep = min(BAND_PAST, seq)
            k4 = kp.reshape(batch, seq, heads, DH)
            v4 = vp.reshape(batch, seq, heads, DH)
            outs["bkp"].append(k4[:, seq - keep:])
            outs["bvp"].append(v4[:, seq - keep:])
            outs["bks"].append(jnp.concatenate([cache_b_k[j], ks.reshape(dbatch, t_new, heads, DH)], axis=1)[:, t_new:])
            outs["bvs"].append(jnp.concatenate([cache_b_v[j], vs.reshape(dbatch, t_new, heads, DH)], axis=1)[:, t_new:])

        xp = _outproj(op, w_out, xp, mod_p, tm=TM_PROJ)
        xs = _outproj(os_, w_out, xs, mod_s, tm=tm_s)

        wq = peer_w_q[i].astype(BF16)
        keys = peer_sub_keys[i].astype(BF16).reshape(PEER_HEADS * 2, N_KEYS, -1)
        u_bf, v_bf = peer_u[i].astype(BF16), peer_v[i].astype(BF16)
        xp = _peer(xp, mod_p, ng1, wq, keys, u_bf, v_bf, tm=TM_DENSE)
        xs = _peer(xs, mod_s, ng1, wq, keys, u_bf, v_bf, tm=tmd_s)

    st = lambda n: jnp.stack(outs[n], 0)
    return (xp.reshape(x_prompt.shape), xs.reshape(x_sample.shape),
            st("akp"), st("avp"), st("aks"), st("avs"), st("bkp"), st("bvp"), st("bks"), st("bvs"))
```

```python
import functools
import math

import numpy as np
import jax
import jax.numpy as jnp
from jax import lax
from jax.experimental import pallas as pl
from jax.experimental.pallas import tpu as pltpu

F32 = jnp.float32
BF16 = jnp.bfloat16

D_MODEL = 1024
CHUNK = 64
EPS = 1e-6
NEG = -1e30
ROPE_THETA = 10000.0
DH = 64
LANES = 128
PREV_CHUNKS = 8
BAND_PAST = PREV_CHUNKS * CHUNK
REL_CLIP = 128
PEER_HEADS = 8
N_KEYS = 128
TOPK = 16
VMEM_LIMIT = 48 * 1024 * 1024

TM_PROJ = 256
TQ_FLASH = 1024
FLASH_ROWS = 1024
TQ_BAND = 256
BAND_QTILES = 4
TM_DENSE = 512
TOKEN_UNROLL = 64
W3_PAD = 8

_NT = (((1,), (1,)), ((), ()))


def _cparams(sem):
    return pltpu.CompilerParams(dimension_semantics=sem, vmem_limit_bytes=VMEM_LIMIT)


def _dot(a, b):
    return jnp.dot(a, b, preferred_element_type=F32)


def _dot_nt(a, b):
    return lax.dot_general(a, b, _NT, preferred_element_type=F32)


def _split(a):
    hi = a.astype(BF16)
    lo = (a - hi.astype(F32)).astype(BF16)
    return hi, lo


def _ada_kernel(c_ref, w_ref, b_ref, o_ref):
    c = c_ref[...]
    a = c * (1.0 / (1.0 + jnp.exp(-c)))
    ah, al = _split(a)
    wh, wl = _split(w_ref[...])
    o_ref[...] = _dot(ah, wh) + _dot(al, wh) + _dot(ah, wl) + b_ref[...]


def _ada_mod(c_all, w, b):
    n, d = c_all.shape
    nout = w.shape[1]
    tn = 512
    return pl.pallas_call(
        _ada_kernel,
        grid=(nout // tn,),
        in_specs=[pl.BlockSpec((n, d), lambda j: (0, 0)),
                  pl.BlockSpec((d, tn), lambda j: (0, j)),
                  pl.BlockSpec((1, tn), lambda j: (0, j))],
        out_specs=pl.BlockSpec((n, tn), lambda j: (0, j)),
        out_shape=jax.ShapeDtypeStruct((n, nout), F32),
        compiler_params=_cparams(("parallel",)),
        name="ada_mod",
    )(c_all, w, b.reshape(1, nout))


class _Mod:
    def __init__(self, arr, per_row, tiles_per_batch):
        self.arr, self.per_row, self.tpb = arr, per_row, tiles_per_batch

    def spec(self, k, tm, tile=lambda i: i):
        if self.per_row:
            return pl.BlockSpec((None, tm, D_MODEL), lambda i, *_: (k, tile(i), 0))
        tpb = self.tpb(tm)
        return pl.BlockSpec((None, 1, D_MODEL), lambda i, *_: ((tile(i) // tpb) * 6 + k, 0, 0))


def _modulated(x, ng, scale, shift):
    ms = jnp.mean(x * x, axis=-1, keepdims=True)
    return (x * lax.rsqrt(ms + EPS) * ng) * (1.0 + scale) + shift


def _inproj_kernel(x_ref, sh_ref, sc_ref, ng_ref, w_ref, gq_ref, gk_ref, cos_ref, sin_ref,
                   q_ref, k_ref, v_ref, *, rope, qscale):
    tm = x_ref.shape[0]
    hb = _modulated(x_ref[...], ng_ref[...], sc_ref[...], sh_ref[...]).astype(BF16)
    lane = lax.broadcasted_iota(jnp.int32, (tm, LANES), 1)
    lo = lane < DH
    swap_sel = (lane & (DH // 2)) != 0

    def norm_rope(xb, g):
        x2 = xb * xb
        slo = jnp.sum(jnp.where(lo, x2, 0.0), axis=-1, keepdims=True)
        shi = jnp.sum(jnp.where(lo, 0.0, x2), axis=-1, keepdims=True)
        ms = jnp.where(lo, slo, shi) * (1.0 / DH)
        y = xb * lax.rsqrt(ms + EPS) * g
        if rope:
            sw = jnp.where(swap_sel, pltpu.roll(y, DH // 2, 1), pltpu.roll(y, LANES - DH // 2, 1))
            y = y * cos_ref[...] + sw * sin_ref[...]
        return y

    nblk = w_ref.shape[1] // (2 * LANES)
    for j in range(nblk):
        acc = _dot(hb, w_ref[:, j * 2 * LANES:(j + 1) * 2 * LANES])
        for half in range(2):
            blk = acc[:, half * LANES:(half + 1) * LANES]
            col = j * 2 * LANES + half * LANES
            if col < D_MODEL:
                q_ref[:, col:col + LANES] = (norm_rope(blk, gq_ref[...]) * qscale).astype(BF16)
            elif col < 2 * D_MODEL:
                k_ref[:, col - D_MODEL:col - D_MODEL + LANES] = norm_rope(blk, gk_ref[...])
            else:
                v_ref[:, col - 2 * D_MODEL:col - 2 * D_MODEL + LANES] = blk


def _inproj(x, mod, ng, w_bf, gq, gk, cos, sin, *, rope, tm):
    t = x.shape[0]
    nrep = cos.shape[0] // tm
    row = lambda i: (i, 0)
    full = lambda i: (0, 0)
    tab = lambda i: (i % nrep, 0)
    return pl.pallas_call(
        functools.partial(_inproj_kernel, rope=rope, qscale=DH ** -0.5),
        grid=(t // tm,),
        in_specs=[pl.BlockSpec((tm, D_MODEL), row), mod.spec(0, tm), mod.spec(1, tm),
                  pl.BlockSpec((1, D_MODEL), full),
                  pl.BlockSpec(w_bf.shape, full),
                  pl.BlockSpec((1, LANES), full), pl.BlockSpec((1, LANES), full),
                  pl.BlockSpec((tm, LANES), tab), pl.BlockSpec((tm, LANES), tab)],
        out_specs=[pl.BlockSpec((tm, D_MODEL), row)] * 3,
        out_shape=[jax.ShapeDtypeStruct((t, D_MODEL), BF16),
                   jax.ShapeDtypeStruct((t, D_MODEL), F32),
                   jax.ShapeDtypeStruct((t, D_MODEL), F32)],
        compiler_params=_cparams(("parallel",)),
        name="qkv_proj",
    )(x, mod.arr, mod.arr, ng, w_bf, gq, gk, cos, sin)


def _cache_layout_kernel(k_ref, v_ref, k4_ref, v4_ref):
    heads = v4_ref.shape[1]
    for h in range(heads):
        kb = k_ref[:, h * LANES:(h + 1) * LANES]
        v4_ref[:, h, :] = v_ref[:, h * LANES:(h + 1) * LANES]
        for c in range(2):
            k4_ref[:, h, c, :] = kb[:, c * DH:(c + 1) * DH]


def _cache_layout(k, v, *, tm):
    t = k.shape[0]
    heads = D_MODEL // LANES
    row = lambda i: (i, 0)
    return pl.pallas_call(
        _cache_layout_kernel,
        grid=(t // tm,),
        in_specs=[pl.BlockSpec((tm, D_MODEL), row)] * 2,
        out_specs=[pl.BlockSpec((tm, heads, 2, DH), lambda i: (i, 0, 0, 0)),
                   pl.BlockSpec((tm, heads, LANES), lambda i: (i, 0, 0))],
        out_shape=[jax.ShapeDtypeStruct((t, heads, 2, DH), F32), jax.ShapeDtypeStruct((t, heads, LANES), F32)],
        compiler_params=_cparams(("parallel",)),
        name="cache_layout",
    )(k, v)


def _diff_lambda(lamv_ref, lam_init):
    lv = lamv_ref[...]
    e1 = jnp.exp(jnp.sum(lv[0:1] * lv[1:2], axis=-1, keepdims=True))
    e2 = jnp.exp(jnp.sum(lv[2:3] * lv[3:4], axis=-1, keepdims=True))
    return e1 - e2 + lam_init


def _diff_finish(o0, o1, lamv_ref, gsub_ref, lam_init):
    o = o0 - _diff_lambda(lamv_ref, lam_init) * o1
    ms = jnp.mean(o * o, axis=-1, keepdims=True)
    return ((o * lax.rsqrt(ms + EPS) * gsub_ref[...]) * (1.0 - lam_init)).astype(BF16)


def _component_queries(q):
    lane = lax.broadcasted_iota(jnp.int32, q.shape, 1)
    zero = jnp.zeros_like(q)
    return jnp.where(lane < DH, q, zero), jnp.where(lane < DH, zero, q)


def _flash_diff_kernel(qt_ref, kt_ref, q_ref, k_ref, v_ref, lamv_ref, gsub_ref, o_ref,
                       m_ref, l_ref, a_ref, *, lam_init):
    p = pl.program_id(2)
    qi = qt_ref[p]
    ki = kt_ref[p]
    tq = q_ref.shape[0]

    @pl.when(ki == 0)
    def _():
        m_ref[...] = jnp.full(m_ref.shape, NEG, F32)
        l_ref[...] = jnp.zeros(l_ref.shape, F32)
        a_ref[...] = jnp.zeros(a_ref.shape, F32)

    def update(masked):
        kb = k_ref[...].astype(BF16)
        vb = v_ref[...].astype(BF16)
        qq = jnp.concatenate(_component_queries(q_ref[...]), axis=0)
        for r in range(2 * tq // FLASH_ROWS):
            rows = slice(r * FLASH_ROWS, (r + 1) * FLASH_ROWS)
            q0 = (r * FLASH_ROWS) % tq
            ncol = min(q0 + FLASH_ROWS, tq) if masked else tq
            s = _dot_nt(qq[rows], kb[:ncol])
            if masked:
                row_c = ((lax.broadcasted_iota(jnp.int32, (FLASH_ROWS, ncol), 0) + q0) % tq) // CHUNK
                col_c = lax.broadcasted_iota(jnp.int32, (FLASH_ROWS, ncol), 1) // CHUNK
                s = jnp.where(col_c <= row_c, s, NEG)
            m_prev = m_ref[rows, :]
            m_new = jnp.maximum(m_prev, jnp.max(s, axis=-1, keepdims=True))
            alpha = jnp.exp(m_prev - m_new)
            pm = jnp.exp(s - jnp.tile(m_new, (1, ncol // LANES)))
            l_ref[rows, :] = alpha * l_ref[rows, :] + jnp.sum(pm, axis=-1, keepdims=True)
            a_ref[rows, :] = alpha * a_ref[rows, :] + _dot(pm.astype(BF16), vb[:ncol])
            m_ref[rows, :] = m_new

    @pl.when(ki < qi)
    def _():
        update(False)

    @pl.when(ki == qi)
    def _():
        update(True)
        o = a_ref[...] / l_ref[...]
        o_ref[...] = _diff_finish(o[:tq], o[tq:], lamv_ref, gsub_ref, lam_init)


def _flash_diff(q, k, v, lamv, gsub, *, batch, seq, lam_init):
    tq = min(TQ_FLASH, seq)
    nq = seq // tq
    heads = D_MODEL // LANES
    pairs = [(a, b) for a in range(nq) for b in range(a + 1)]
    qt = jnp.asarray([a for a, _ in pairs], jnp.int32)
    kt = jnp.asarray([b for _, b in pairs], jnp.int32)
    qmap = lambda b, h, p, qt, kt: (b * nq + qt[p], h)
    kmap = lambda b, h, p, qt, kt: (b * nq + kt[p], h)
    return pl.pallas_call(
        functools.partial(_flash_diff_kernel, lam_init=lam_init),
        grid_spec=pltpu.PrefetchScalarGridSpec(
            num_scalar_prefetch=2,
            grid=(batch, heads, len(pairs)),
            in_specs=[pl.BlockSpec((tq, LANES), qmap),
                      pl.BlockSpec((tq, LANES), kmap),
                      pl.BlockSpec((tq, LANES), kmap),
                      pl.BlockSpec(lamv.shape, lambda *_: (0, 0)),
                      pl.BlockSpec((1, LANES), lambda *_: (0, 0))],
            out_specs=pl.BlockSpec((tq, LANES), qmap),
            scratch_shapes=[pltpu.VMEM((2 * tq, LANES), F32)] * 3),
        out_shape=jax.ShapeDtypeStruct(q.shape, BF16),
        compiler_params=_cparams(("parallel", "parallel", "arbitrary")),
        name="flash_diff_attn",
    )(qt, kt, q, k, v, lamv, gsub)


def _joint_softmax_attend(qc, pieces):
    ss = []
    for kb, _, bias, visible in pieces:
        s = _dot_nt(qc, kb)
        if bias is not None:
            s = s + bias
        if visible is not None:
            s = jnp.where(visible, s, NEG)
        ss.append(s)
    m = functools.reduce(jnp.maximum, [jnp.max(s, axis=-1, keepdims=True) for s in ss])
    l = 0.0
    o = 0.0
    for s, (_, vb, _, _) in zip(ss, pieces):
        pm = jnp.exp(s - m)
        l = l + jnp.sum(pm, axis=-1, keepdims=True)
        o = o + _dot(pm.astype(BF16), vb)
    return o / l


def _diff_sample_kernel(q_ref, ck_ref, cv_ref, nk_ref, nv_ref, lamv_ref, gsub_ref, o_ref, *, lam_init):
    t_new = q_ref.shape[0]
    pieces = [(ck_ref[...].astype(BF16), cv_ref[...].astype(BF16), None, None),
              (nk_ref[...].astype(BF16), nv_ref[...].astype(BF16), None, None)]
    o = _joint_softmax_attend(jnp.concatenate(_component_queries(q_ref[...]), axis=0), pieces)
    o_ref[...] = _diff_finish(o[:t_new], o[t_new:], lamv_ref, gsub_ref, lam_init)


def _diff_sample(q, ck, cv, nk, nv, lamv, gsub, *, batch, t_new, past, lam_init):
    heads = D_MODEL // LANES
    bh = lambda b, h: (b, h)
    return pl.pallas_call(
        functools.partial(_diff_sample_kernel, lam_init=lam_init),
        grid=(batch, heads),
        in_specs=[pl.BlockSpec((t_new, LANES), bh),
                  pl.BlockSpec((past, LANES), bh), pl.BlockSpec((past, LANES), bh),
                  pl.BlockSpec((t_new, LANES), bh), pl.BlockSpec((t_new, LANES), bh),
                  pl.BlockSpec(lamv.shape, lambda b, h: (0, 0)),
                  pl.BlockSpec((1, LANES), lambda b, h: (0, 0))],
        out_specs=pl.BlockSpec((t_new, LANES), bh),
        out_shape=jax.ShapeDtypeStruct(q.shape, BF16),
        compiler_params=_cparams(("parallel", "parallel")),
        name="diff_attn_sample",
    )(q, ck, cv, nk, nv, lamv, gsub)


def _band_finish(o):
    tq = o.shape[0] // 2
    lane = lax.broadcasted_iota(jnp.int32, (tq, LANES), 1)
    return jnp.where(lane < DH, o[:tq], o[tq:]).astype(BF16)


def _band_prompt_kernel(q_ref, *refs):
    nkv = BAND_QTILES + 2
    k_refs, v_refs, bias_ref, o_ref = refs[:nkv], refs[nkv:2 * nkv], refs[2 * nkv], refs[2 * nkv + 1]
    first = pl.program_id(2) * BAND_QTILES
    tq = q_ref.shape[0] // BAND_QTILES
    row_c = (lax.broadcasted_iota(jnp.int32, (2 * tq, tq), 0) % tq) // CHUNK
    col_c = lax.broadcasted_iota(jnp.int32, (2 * tq, tq), 1) // CHUNK
    kvs = [(k_ref[...].astype(BF16), v_ref[...].astype(BF16)) for k_ref, v_ref in zip(k_refs, v_refs)]
    for sub in range(BAND_QTILES):
        pieces = []
        for r in range(3):
            before_start = jnp.where(first + sub + r - 2 < 0, 4 * PREV_CHUNKS, 0)
            kc = col_c + (r - 2) * (tq // CHUNK) + before_start
            visible = (kc <= row_c) & (kc >= row_c - PREV_CHUNKS)
            pieces.append((*kvs[sub + r], bias_ref[:, r * tq:(r + 1) * tq], visible))
        rows = slice(sub * tq, (sub + 1) * tq)
        qq = jnp.concatenate(_component_queries(q_ref[rows, :]), axis=0)
        o_ref[rows, :] = _band_finish(_joint_softmax_attend(qq, pieces))


def _band_bias_tiles(table, nq, nk, c0):
    n = nq + nk - 1
    diag = jnp.clip(c0 + nq - 1 - np.arange(n), -REL_CLIP, REL_CLIP) + REL_CLIP
    e = table[:, diag]
    a = jnp.tile(e, (1, nq + 1))[:, :nq * (n + 1)].reshape(-1, nq, n + 1)[:, ::-1, :nk]
    return a.reshape(table.shape[0] // 2, 2 * nq, nk)


def _band_prompt(q, k, v, table, *, batch, seq):
    tq = TQ_BAND
    nsub = BAND_QTILES
    assert BAND_PAST == 2 * tq and seq % (nsub * tq) == 0
    nq = seq // tq
    hp = D_MODEL // LANES
    bias = _band_bias_tiles(table, tq, 3 * tq, 2 * tq)
    qmap = lambda h, b, i: (b * (nq // nsub) + i, h)
    kmap = lambda r: (lambda h, b, i: (b * nq + jnp.maximum(nsub * i + r - 2, 0), h))
    kv_specs = [pl.BlockSpec((tq, LANES), kmap(r)) for r in range(nsub + 2)]
    return pl.pallas_call(
        _band_prompt_kernel,
        grid=(hp, batch, nq // nsub),
        in_specs=[pl.BlockSpec((nsub * tq, LANES), qmap)] + kv_specs * 2
                 + [pl.BlockSpec((None, 2 * tq, 3 * tq), lambda h, b, i: (h, 0, 0))],
        out_specs=pl.BlockSpec((nsub * tq, LANES), qmap),
        out_shape=jax.ShapeDtypeStruct(q.shape, BF16),
        compiler_params=_cparams(("parallel", "parallel", "parallel")),
        name="band_attn_prompt",
    )(q, *([k] * (nsub + 2)), *([v] * (nsub + 2)), bias)


def _band_sample_kernel(q_ref, ck_ref, cv_ref, nk_ref, nv_ref, bc_ref, bn_ref, o_ref):
    pieces = [(ck_ref[...].astype(BF16), cv_ref[...].astype(BF16), bc_ref[...], None),
              (nk_ref[...].astype(BF16), nv_ref[...].astype(BF16), bn_ref[...], None)]
    qq = jnp.concatenate(_component_queries(q_ref[...]), axis=0)
    o_ref[...] = _band_finish(_joint_softmax_attend(qq, pieces))


def _band_sample(q, ck, cv, nk, nv, table, *, batch, t_new, past):
    hp = D_MODEL // LANES
    bias = _band_bias_tiles(table, t_new, past + t_new, past)
    bias_c, bias_n = bias[..., :past], bias[..., past:]
    hb = lambda h, b: (b, h)
    return pl.pallas_call(
        _band_sample_kernel,
        grid=(hp, batch),
        in_specs=[pl.BlockSpec((t_new, LANES), hb),
                  pl.BlockSpec((past, LANES), hb), pl.BlockSpec((past, LANES), hb),
                  pl.BlockSpec((t_new, LANES), hb), pl.BlockSpec((t_new, LANES), hb),
                  pl.BlockSpec((None, 2 * t_new, past), lambda h, b: (h, 0, 0)),
                  pl.BlockSpec((None, 2 * t_new, t_new), lambda h, b: (h, 0, 0))],
        out_specs=pl.BlockSpec((t_new, LANES), hb),
        out_shape=jax.ShapeDtypeStruct(q.shape, BF16),
        compiler_params=_cparams(("parallel", "parallel")),
        name="band_attn_sample",
    )(q, ck, cv, nk, nv, bias_c, bias_n)


def _outproj_kernel(o_ref, w_ref, x_ref, gate_ref, y_ref):
    y_ref[...] = x_ref[...] + gate_ref[...] * _dot(o_ref[...], w_ref[...])


def _outproj(o, w_bf, x, mod, *, tm):
    t = x.shape[0]
    row = lambda i: (i, 0)
    return pl.pallas_call(
        _outproj_kernel,
        grid=(t // tm,),
        in_specs=[pl.BlockSpec((tm, D_MODEL), row), pl.BlockSpec(w_bf.shape, lambda i: (0, 0)),
                  pl.BlockSpec((tm, D_MODEL), row), mod.spec(2, tm)],
        out_specs=pl.BlockSpec((tm, D_MODEL), row),
        out_shape=jax.ShapeDtypeStruct(x.shape, F32),
        compiler_params=_cparams(("parallel",)),
        name="out_proj",
    )(o, w_bf, x, mod.arr)


def _top_rows(s, k):
    if s.shape[1] > LANES:
        cols = [_top_rows(s[:, c:c + LANES], k) for c in range(0, s.shape[1], LANES)]
        return tuple(jnp.concatenate(x, axis=1) for x in zip(*cols))
    rows = s.shape[0]
    rid = lax.broadcasted_iota(jnp.int32, s.shape, 0).astype(F32)
    vals, ids = [], []
    for r in range(k):
        cand = [(s[v:v + _SUB], rid[v:v + _SUB]) for v in range(0, rows, _SUB)]
        while len(cand) > 1:
            nxt = []
            for a in range(0, len(cand) - 1, 2):
                (va, ia), (vb, ib) = cand[a], cand[a + 1]
                keep = va >= vb
                nxt.append((jnp.where(keep, va, vb), jnp.where(keep, ia, ib)))
            cand = nxt + cand[len(cand) - len(cand) % 2:]
        v8, i8 = cand[0]
        m = jnp.max(v8, axis=0, keepdims=True)
        i = jnp.min(jnp.where(v8 == m, i8, float(rows)), axis=0, keepdims=True)
        vals.append(m)
        ids.append(i)
        if r + 1 < k:
            s = jnp.where(rid == i, -jnp.inf, s)
    return jnp.concatenate(vals, axis=0), jnp.concatenate(ids, axis=0)


_SUB = 8
_STAIR_PIECES = ([(0, 1, 0, _SUB), (0, 1, _SUB, _SUB), (1, 1, 0, _SUB)]
                 + [(a, 1, 0, TOPK // (a + 1)) for a in range(2, _SUB)] + [(_SUB, _SUB, 0, 1)])


def _stair_candidates(s1, s2):
    sub = lax.broadcasted_iota(jnp.int32, (_SUB, s1.shape[1]), 0)
    pieces = []
    for a0, na, b0, nb in _STAIR_PIECES:
        if na == 1:
            piece = s1[a0:a0 + 1, :] + s2[b0:b0 + _SUB, :]
            if nb < _SUB:
                piece = jnp.where(sub < nb, piece, -jnp.inf)
        else:
            piece = s1[a0:a0 + na, :] + s2[b0:b0 + 1, :]
        pieces.append(piece)
    return jnp.concatenate(pieces, axis=0)


def _stair_ranks(pos):
    a = jnp.zeros(pos.shape, F32)
    b = pos
    for p, (a0, na, b0, nb) in enumerate(_STAIR_PIECES):
        start = float(p * _SUB)
        inside = pos >= start
        if na == 1:
            a = jnp.where(inside, float(a0), a)
            b = jnp.where(inside, pos - start + float(b0), b)
        else:
            a = jnp.where(inside, pos - start + float(a0), a)
            b = jnp.where(inside, float(b0), b)
    return a, b


def _pick_rows(sel, table):
    out = jnp.zeros(sel.shape, F32)
    for a in range(table.shape[0]):
        out = out + jnp.where(sel == float(a), table[a:a + 1, :], 0.0)
    return out


def _select_unit(q_scr, keys_ref, sel_t_scr, h, part):
    tokens = pl.ds(pl.multiple_of(part * LANES, LANES), LANES)
    top = [_top_rows(_dot_nt(keys_ref[2 * h + c], q_scr[2 * h + c, tokens, :]), TOPK) for c in range(2)]
    (s1, i1), (s2, i2) = top
    top_s, pos = _top_rows(_stair_candidates(s1, s2), TOPK)
    a_sel, b_sel = _stair_ranks(pos)
    e = jnp.exp(top_s - top_s[0:1, :])
    rows = pl.ds(pl.multiple_of(h * TOPK, TOPK), TOPK)
    sel_t_scr[0, part, rows, :] = _pick_rows(a_sel, i1)
    sel_t_scr[1, part, rows, :] = _pick_rows(b_sel, i2)
    sel_t_scr[2, part, rows, :] = e / jnp.sum(e, axis=0, keepdims=True)


def _peer_kernel(xs_ref, sh_ref, sc_ref, ng_ref, wq_ref, keys_ref, u_ref, v_ref, xr_ref, gate_ref, y_ref,
                 hb_scr, q_scr, sel_t_scr, sel_scr, w3_ref, acc_ref):
    i = pl.program_id(0)
    j = pl.program_id(1)
    tm = xs_ref.shape[0]
    nparts = tm // LANES
    half = N_KEYS // 2
    pitch = w3_ref.shape[0] // half
    e_tile = u_ref.shape[0]
    slot_new, slot_dense = i % 2, (i + 1) % 2

    @pl.when((i == 0) & (j == 0))
    def _():
        w3_ref[...] = jnp.zeros(w3_ref.shape, w3_ref.dtype)
        hb_scr[1] = jnp.zeros(hb_scr.shape[1:], hb_scr.dtype)

    @pl.when((i > 0) & (j == 0))
    def _():
        for k in range(3):
            for part in range(nparts):
                sel_scr[k, part * LANES:(part + 1) * LANES, :] = sel_t_scr[k, part].T
        row = lax.broadcasted_iota(jnp.int32, (N_KEYS, N_KEYS), 0)
        i1_ids = jnp.where(row < half, 2 * row, 2 * (row - half) + 1).astype(F32)
        i2_ids = row.astype(F32)

        def token(t, carry):
            arow = sel_scr[0, pl.ds(t, 1), :]
            brow = sel_scr[1, pl.ds(t, 1), :]
            grow = sel_scr[2, pl.ds(t, 1), :]
            oa = jnp.where(arow == i1_ids, 1.0, 0.0).astype(BF16)
            ob = jnp.where(brow == i2_ids, 0.5 * grow, 0.0).astype(BF16)
            w = _dot_nt(oa, ob)
            w3_ref[pl.ds(t, half, stride=pitch), :] = pltpu.pack_elementwise([w[:half], w[half:]], packed_dtype=BF16)
            return carry

        lax.fori_loop(0, tm, token, 0, unroll=TOKEN_UNROLL)

    @pl.when(j == 0)
    def _():
        acc_ref[...] = jnp.zeros(acc_ref.shape, F32)
        hb = _modulated(xs_ref[...], ng_ref[...], sc_ref[...], sh_ref[...]).astype(BF16)
        hb_scr[slot_new] = hb
        q_all = _dot(hb, wq_ref[...])
        for hc in range(2 * PEER_HEADS):
            q_scr[hc] = q_all[:, hc * N_KEYS:(hc + 1) * N_KEYS].astype(BF16)

    _select_unit(q_scr, keys_ref, sel_t_scr, j // nparts, j % nparts)

    cols = []
    for q in range(e_tile // (2 * N_KEYS)):
        word = w3_ref[pl.ds(pl.multiple_of((e_tile // (2 * N_KEYS) * j + q) * pitch, 8), tm), :]
        cols += [pltpu.unpack_elementwise(word, index=k, packed_dtype=BF16, unpacked_dtype=F32) for k in range(2)]
    w = jnp.concatenate(cols, axis=1)
    hid = _dot_nt(hb_scr[slot_dense], u_ref[...])
    act = hid * (1.0 + lax.erf(hid * math.sqrt(0.5)))
    acc_ref[...] += _dot((w * act).astype(BF16), v_ref[...])

    @pl.when((i > 0) & (j == pl.num_programs(1) - 1))
    def _():
        y_ref[...] = xr_ref[...] + gate_ref[...] * acc_ref[...]


def _peer(x, mod, ng, wq_bf, keys_bf, u_bf, v_bf, *, tm):
    t = x.shape[0]
    nt = t // tm
    units = PEER_HEADS * (tm // LANES)
    e_tile = u_bf.shape[0] // units
    assert e_tile % (2 * N_KEYS) == 0 and tm % TOKEN_UNROLL == 0
    new_tile = lambda i: jnp.minimum(i, nt - 1)
    dense_tile = lambda i: jnp.maximum(i - 1, 0)
    full = lambda i, j: (0, 0)
    return pl.pallas_call(
        _peer_kernel,
        grid=(nt + 1, units),
        in_specs=[pl.BlockSpec((tm, D_MODEL), lambda i, j: (new_tile(i), 0)),
                  mod.spec(3, tm, new_tile), mod.spec(4, tm, new_tile),
                  pl.BlockSpec((1, D_MODEL), full),
                  pl.BlockSpec(wq_bf.shape, full),
                  pl.BlockSpec(keys_bf.shape, lambda i, j: (0, 0, 0)),
                  pl.BlockSpec((e_tile, D_MODEL), lambda i, j: (j, 0)),
                  pl.BlockSpec((e_tile, D_MODEL), lambda i, j: (j, 0)),
                  pl.BlockSpec((tm, D_MODEL), lambda i, j: (dense_tile(i), 0)),
                  mod.spec(5, tm, dense_tile)],
        out_specs=pl.BlockSpec((tm, D_MODEL), lambda i, j: (dense_tile(i), 0)),
        out_shape=jax.ShapeDtypeStruct(x.shape, F32),
        scratch_shapes=[pltpu.VMEM((2, tm, D_MODEL), BF16),
                        pltpu.VMEM((2 * PEER_HEADS, tm, N_KEYS), BF16),
                        pltpu.VMEM((3, tm // LANES, PEER_HEADS * TOPK, LANES), F32),
                        pltpu.VMEM((3, tm, PEER_HEADS * TOPK), F32),
                        pltpu.VMEM((N_KEYS // 2 * (tm + W3_PAD), N_KEYS), jnp.int32),
                        pltpu.VMEM((tm, D_MODEL), F32)],
        compiler_params=_cparams(("arbitrary", "arbitrary")),
        name="peer",
    )(x, mod.arr, mod.arr, ng, wq_bf, keys_bf, u_bf, v_bf, x, mod.arr)


def _rope_tables(pos):
    half = DH // 2
    inv = ROPE_THETA ** (-jnp.arange(half, dtype=F32) / half)
    ang = pos.astype(F32)[:, None] * inv[None, :]
    cos, sin = jnp.cos(ang), jnp.sin(ang)
    return jnp.tile(cos, (1, 4)), jnp.concatenate([-sin, sin, -sin, sin], axis=1)


def _pair_tile(g):
    return jnp.tile(g.reshape(1, -1), (1, LANES // g.shape[-1]))


def kernel(x_prompt, x_sample, c_prompt, c_sample, cache_a_k, cache_a_v, cache_b_k, cache_b_v, ada_w, ada_b, norm_g, a_w_in, a_g_q, a_g_k, a_lq1, a_lk1, a_lq2, a_lk2, a_g_sub, a_w_out, b_w_in, b_g_q, b_g_k, b_rel_bias, b_w_out, peer_w_q, peer_sub_keys, peer_u, peer_v):
    batch, seq, _ = x_prompt.shape
    dbatch, t_new, _ = x_sample.shape
    past_a = cache_a_k.shape[2]
    past_b = cache_b_k.shape[2]
    depth = ada_w.shape[0]
    tp, ts = batch * seq, dbatch * t_new
    tm_s = min(TM_PROJ, ts)
    tmd_s = min(TM_DENSE // 2, ts)

    xp = x_prompt.reshape(tp, D_MODEL)
    xs = x_sample.reshape(ts, D_MODEL)
    c_all = jnp.concatenate([c_prompt, c_sample], axis=0)

    cos_p, sin_p = _rope_tables(jnp.arange(seq))
    cos_s, sin_s = _rope_tables(past_a + jnp.arange(t_new))
    cos_s, sin_s = jnp.tile(cos_s, (tm_s // t_new, 1)), jnp.tile(sin_s, (tm_s // t_new, 1))

    outs = {n: [] for n in ("akp", "avp", "aks", "avs", "bkp", "bvp", "bks", "bvs")}
    for i in range(depth):
        j = i // 2
        mod = _ada_mod(c_all, ada_w[i], ada_b[i])
        mod_p = _Mod(mod[:batch].reshape(batch * 6, 1, D_MODEL), False, lambda tm: seq // tm)
        mod_s = _Mod(jnp.repeat(mod[batch:].reshape(dbatch, 6, D_MODEL).transpose(1, 0, 2), t_new, axis=1),
                     True, None)
        ng0, ng1 = norm_g[i, 0].reshape(1, -1), norm_g[i, 1].reshape(1, -1)

        if i % 2 == 0:
            lam_init = 0.8 - 0.6 * math.exp(-0.3 * i)
            w_in = a_w_in[j].astype(BF16)
            gq, gk = _pair_tile(a_g_q[j]), _pair_tile(a_g_k[j])
            lamv = jnp.stack([a_lq1[j], a_lk1[j], a_lq2[j], a_lk2[j]])
            gsub = a_g_sub[j].reshape(1, -1)
            qp, kp, vp = _inproj(xp, mod_p, ng0, w_in, gq, gk, cos_p, sin_p, rope=True, tm=TM_PROJ)
            qs, ks, vs = _inproj(xs, mod_s, ng0, w_in, gq, gk, cos_s, sin_s, rope=True, tm=tm_s)
            op = _flash_diff(qp, kp, vp, lamv, gsub, batch=batch, seq=seq, lam_init=lam_init)
            os_ = _diff_sample(qs, cache_a_k[j].reshape(dbatch * past_a, D_MODEL),
                               cache_a_v[j].reshape(dbatch * past_a, D_MODEL), ks, vs, lamv, gsub,
                               batch=dbatch, t_new=t_new, past=past_a, lam_init=lam_init)
            w_out = a_w_out[j].astype(BF16)
            heads = D_MODEL // LANES
            k4p, v4p = _cache_layout(kp, vp, tm=TM_PROJ)
            k4s, v4s = _cache_layout(ks, vs, tm=tm_s)
            outs["akp"].append(k4p.reshape(batch, seq, heads, 2, DH))
            outs["avp"].append(v4p.reshape(batch, seq, heads, 2 * DH))
            outs["aks"].append(k4s.reshape(dbatch, t_new, heads, 2, DH))
            outs["avs"].append(v4s.reshape(dbatch, t_new, heads, 2 * DH))
        else:
            w_in = b_w_in[j].astype(BF16)
            gq, gk = _pair_tile(b_g_q[j]), _pair_tile(b_g_k[j])
            qp, kp, vp = _inproj(xp, mod_p, ng0, w_in, gq, gk, cos_p, sin_p, rope=False, tm=TM_PROJ)
            qs, ks, vs = _inproj(xs, mod_s, ng0, w_in, gq, gk, cos_s, sin_s, rope=False, tm=tm_s)
            op = _band_prompt(qp, kp, vp, b_rel_bias[j], batch=batch, seq=seq)
            os_ = _band_sample(qs, cache_b_k[j].reshape(dbatch * past_b, D_MODEL),
                               cache_b_v[j].reshape(dbatch * past_b, D_MODEL), ks, vs, b_rel_bias[j],
                               batch=dbatch, t_new=t_new, past=past_b)
            w_out = b_w_out[j].astype(BF16)
            heads = D_MODEL // DH
            keep = min(BAND_PAST, seq)
            k4 = kp.reshape(batch, seq, heads, DH)
            v4 = vp.reshape(batch, seq, heads, DH)
            outs["bkp"].append(k4[:, seq - keep:])
            outs["bvp"].append(v4[:, seq - keep:])
            outs["bks"].append(jnp.concatenate([cache_b_k[j], ks.reshape(dbatch, t_new, heads, DH)], axis=1)[:, t_new:])
            outs["bvs"].append(jnp.concatenate([cache_b_v[j], vs.reshape(dbatch, t_new, heads, DH)], axis=1)[:, t_new:])

        xp = _outproj(op, w_out, xp, mod_p, tm=TM_PROJ)
        xs = _outproj(os_, w_out, xs, mod_s, tm=tm_s)

        wq = peer_w_q[i].astype(BF16)
        keys = peer_sub_keys[i].astype(BF16).reshape(PEER_HEADS * 2, N_KEYS, -1)
        u_bf, v_bf = peer_u[i].astype(BF16), peer_v[i].astype(BF16)
        xp = _peer(xp, mod_p, ng1, wq, keys, u_bf, v_bf, tm=TM_DENSE)
        xs = _peer(xs, mod_s, ng1, wq, keys, u_bf, v_bf, tm=tmd_s)

    st = lambda n: jnp.stack(outs[n], 0)
    return (xp.reshape(x_prompt.shape), xs.reshape(x_sample.shape),
            st("akp"), st("avp"), st("aks"), st("avs"), st("bkp"), st("bvp"), st("bks"), st("bvs"))
```

```python
import functools
import math

import numpy as np
import jax
import jax.numpy as jnp
from jax import lax
from jax.experimental import pallas as pl
from jax.experimental.pallas import tpu as pltpu

F32 = jnp.float32
BF16 = jnp.bfloat16

D_MODEL = 1024
CHUNK = 64
EPS = 1e-6
NEG = -1e30
ROPE_THETA = 10000.0
DH = 64
LANES = 128
PREV_CHUNKS = 8
BAND_PAST = PREV_CHUNKS * CHUNK
REL_CLIP = 128
PEER_HEADS = 8
N_KEYS = 128
TOPK = 16
VMEM_LIMIT = 48 * 1024 * 1024

TM_PROJ = 256
TQ_FLASH = 1024
FLASH_ROWS = 1024
TQ_BAND = 256
BAND_QTILES = 4
TM_DENSE = 512
TOKEN_UNROLL = 64
W3_PAD = 8

_NT = (((1,), (1,)), ((), ()))


def _cparams(sem):
    return pltpu.CompilerParams(dimension_semantics=sem, vmem_limit_bytes=VMEM_LIMIT)


def _dot(a, b):
    return jnp.dot(a, b, preferred_element_type=F32)


def _dot_nt(a, b):
    return lax.dot_general(a, b, _NT, preferred_element_type=F32)


def _split(a):
    hi = a.astype(BF16)
    lo = (a - hi.astype(F32)).astype(BF16)
    return hi, lo


def _ada_kernel(c_ref, w_ref, b_ref, o_ref):
    c = c_ref[...]
    a = c * (1.0 / (1.0 + jnp.exp(-c)))
    ah, al = _split(a)
    wh, wl = _split(w_ref[...])
    o_ref[...] = _dot(ah, wh) + _dot(al, wh) + _dot(ah, wl) + b_ref[...]


def _ada_mod(c_all, w, b):
    n, d = c_all.shape
    nout = w.shape[1]
    tn = 512
    return pl.pallas_call(
        _ada_kernel,
        grid=(nout // tn,),
        in_specs=[pl.BlockSpec((n, d), lambda j: (0, 0)),
                  pl.BlockSpec((d, tn), lambda j: (0, j)),
                  pl.BlockSpec((1, tn), lambda j: (0, j))],
        out_specs=pl.BlockSpec((n, tn), lambda j: (0, j)),
        out_shape=jax.ShapeDtypeStruct((n, nout), F32),
        compiler_params=_cparams(("parallel",)),
        name="ada_mod",
    )(c_all, w, b.reshape(1, nout))


class _Mod:
    def __init__(self, arr, per_row, tiles_per_batch):
        self.arr, self.per_row, self.tpb = arr, per_row, tiles_per_batch

    def spec(self, k, tm, tile=lambda i: i):
        if self.per_row:
            return pl.BlockSpec((None, tm, D_MODEL), lambda i, *_: (k, tile(i), 0))
        tpb = self.tpb(tm)
        return pl.BlockSpec((None, 1, D_MODEL), lambda i, *_: ((tile(i) // tpb) * 6 + k, 0, 0))


def _modulated(x, ng, scale, shift):
    ms = jnp.mean(x * x, axis=-1, keepdims=True)
    return (x * lax.rsqrt(ms + EPS) * ng) * (1.0 + scale) + shift


def _inproj_kernel(x_ref, sh_ref, sc_ref, ng_ref, w_ref, gq_ref, gk_ref, cos_ref, sin_ref,
                   q_ref, k_ref, v_ref, *, rope, qscale):
    tm = x_ref.shape[0]
    hb = _modulated(x_ref[...], ng_ref[...], sc_ref[...], sh_ref[...]).astype(BF16)
    lane = lax.broadcasted_iota(jnp.int32, (tm, LANES), 1)
    lo = lane < DH
    swap_sel = (lane & (DH // 2)) != 0

    def norm_rope(xb, g):
        x2 = xb * xb
        slo = jnp.sum(jnp.where(lo, x2, 0.0), axis=-1, keepdims=True)
        shi = jnp.sum(jnp.where(lo, 0.0, x2), axis=-1, keepdims=True)
        ms = jnp.where(lo, slo, shi) * (1.0 / DH)
        y = xb * lax.rsqrt(ms + EPS) * g
        if rope:
            sw = jnp.where(swap_sel, pltpu.roll(y, DH // 2, 1), pltpu.roll(y, LANES - DH // 2, 1))
            y = y * cos_ref[...] + sw * sin_ref[...]
        return y

    nblk = w_ref.shape[1] // (2 * LANES)
    for j in range(nblk):
        acc = _dot(hb, w_ref[:, j * 2 * LANES:(j + 1) * 2 * LANES])
        for half in range(2):
            blk = acc[:, half * LANES:(half + 1) * LANES]
            col = j * 2 * LANES + half * LANES
            if col < D_MODEL:
                q_ref[:, col:col + LANES] = (norm_rope(blk, gq_ref[...]) * qscale).astype(BF16)
            elif col < 2 * D_MODEL:
                k_ref[:, col - D_MODEL:col - D_MODEL + LANES] = norm_rope(blk, gk_ref[...])
            else:
                v_ref[:, col - 2 * D_MODEL:col - 2 * D_MODEL + LANES] = blk


def _inproj(x, mod, ng, w_bf, gq, gk, cos, sin, *, rope, tm):
    t = x.shape[0]
    nrep = cos.shape[0] // tm
    row = lambda i: (i, 0)
    full = lambda i: (0, 0)
    tab = lambda i: (i % nrep, 0)
    return pl.pallas_call(
        functools.partial(_inproj_kernel, rope=rope, qscale=DH ** -0.5),
        grid=(t // tm,),
        in_specs=[pl.BlockSpec((tm, D_MODEL), row), mod.spec(0, tm), mod.spec(1, tm),
                  pl.BlockSpec((1, D_MODEL), full),
                  pl.BlockSpec(w_bf.shape, full),
                  pl.BlockSpec((1, LANES), full), pl.BlockSpec((1, LANES), full),
                  pl.BlockSpec((tm, LANES), tab), pl.BlockSpec((tm, LANES), tab)],
        out_specs=[pl.BlockSpec((tm, D_MODEL), row)] * 3,
        out_shape=[jax.ShapeDtypeStruct((t, D_MODEL), BF16),
                   jax.ShapeDtypeStruct((t, D_MODEL), F32),
                   jax.ShapeDtypeStruct((t, D_MODEL), F32)],
        compiler_params=_cparams(("parallel",)),
        name="qkv_proj",
    )(x, mod.arr, mod.arr, ng, w_bf, gq, gk, cos, sin)


def _cache_layout_kernel(k_ref, v_ref, k4_ref, v4_ref):
    heads = v4_ref.shape[1]
    for h in range(heads):
        kb = k_ref[:, h * LANES:(h + 1) * LANES]
        v4_ref[:, h, :] = v_ref[:, h * LANES:(h + 1) * LANES]
        for c in range(2):
            k4_ref[:, h, c, :] = kb[:, c * DH:(c + 1) * DH]


def _cache_layout(k, v, *, tm):
    t = k.shape[0]
    heads = D_MODEL // LANES
    row = lambda i: (i, 0)
    return pl.pallas_call(
        _cache_layout_kernel,
        grid=(t // tm,),
        in_specs=[pl.BlockSpec((tm, D_MODEL), row)] * 2,
        out_specs=[pl.BlockSpec((tm, heads, 2, DH), lambda i: (i, 0, 0, 0)),
                   pl.BlockSpec((tm, heads, LANES), lambda i: (i, 0, 0))],
        out_shape=[jax.ShapeDtypeStruct((t, heads, 2, DH), F32), jax.ShapeDtypeStruct((t, heads, LANES), F32)],
        compiler_params=_cparams(("parallel",)),
        name="cache_layout",
    )(k, v)


def _diff_lambda(lamv_ref, lam_init):
    lv = lamv_ref[...]
    e1 = jnp.exp(jnp.sum(lv[0:1] * lv[1:2], axis=-1, keepdims=True))
    e2 = jnp.exp(jnp.sum(lv[2:3] * lv[3:4], axis=-1, keepdims=True))
    return e1 - e2 + lam_init


def _diff_finish(o0, o1, lamv_ref, gsub_ref, lam_init):
    o = o0 - _diff_lambda(lamv_ref, lam_init) * o1
    ms = jnp.mean(o * o, axis=-1, keepdims=True)
    return ((o * lax.rsqrt(ms + EPS) * gsub_ref[...]) * (1.0 - lam_init)).astype(BF16)


def _component_queries(q):
    lane = lax.broadcasted_iota(jnp.int32, q.shape, 1)
    zero = jnp.zeros_like(q)
    return jnp.where(lane < DH, q, zero), jnp.where(lane < DH, zero, q)


def _flash_diff_kernel(qt_ref, kt_ref, q_ref, k_ref, v_ref, lamv_ref, gsub_ref, o_ref,
                       m_ref, l_ref, a_ref, *, lam_init):
    p = pl.program_id(2)
    qi = qt_ref[p]
    ki = kt_ref[p]
    tq = q_ref.shape[0]

    @pl.when(ki == 0)
    def _():
        m_ref[...] = jnp.full(m_ref.shape, NEG, F32)
        l_ref[...] = jnp.zeros(l_ref.shape, F32)
        a_ref[...] = jnp.zeros(a_ref.shape, F32)

    def update(masked):
        kb = k_ref[...].astype(BF16)
        vb = v_ref[...].astype(BF16)
        qq = jnp.concatenate(_component_queries(q_ref[...]), axis=0)
        for r in range(2 * tq // FLASH_ROWS):
            rows = slice(r * FLASH_ROWS, (r + 1) * FLASH_ROWS)
            q0 = (r * FLASH_ROWS) % tq
            ncol = min(q0 + FLASH_ROWS, tq) if masked else tq
            s = _dot_nt(qq[rows], kb[:ncol])
            if masked:
                row_c = ((lax.broadcasted_iota(jnp.int32, (FLASH_ROWS, ncol), 0) + q0) % tq) // CHUNK
                col_c = lax.broadcasted_iota(jnp.int32, (FLASH_ROWS, ncol), 1) // CHUNK
                s = jnp.where(col_c <= row_c, s, NEG)
            m_prev = m_ref[rows, :]
            m_new = jnp.maximum(m_prev, jnp.max(s, axis=-1, keepdims=True))
            alpha = jnp.exp(m_prev - m_new)
            pm = jnp.exp(s - jnp.tile(m_new, (1, ncol // LANES)))
            l_ref[rows, :] = alpha * l_ref[rows, :] + jnp.sum(pm, axis=-1, keepdims=True)
            a_ref[rows, :] = alpha * a_ref[rows, :] + _dot(pm.astype(BF16), vb[:ncol])
            m_ref[rows, :] = m_new

    @pl.when(ki < qi)
    def _():
        update(False)

    @pl.when(ki == qi)
    def _():
        update(True)
        o = a_ref[...] / l_ref[...]
        o_ref[...] = _diff_finish(o[:tq], o[tq:], lamv_ref, gsub_ref, lam_init)


def _flash_diff(q, k, v, lamv, gsub, *, batch, seq, lam_init):
    tq = min(TQ_FLASH, seq)
    nq = seq // tq
    heads = D_MODEL // LANES
    pairs = [(a, b) for a in range(nq) for b in range(a + 1)]
    qt = jnp.asarray([a for a, _ in pairs], jnp.int32)
    kt = jnp.asarray([b for _, b in pairs], jnp.int32)
    qmap = lambda b, h, p, qt, kt: (b * nq + qt[p], h)
    kmap = lambda b, h, p, qt, kt: (b * nq + kt[p], h)
    return pl.pallas_call(
        functools.partial(_flash_diff_kernel, lam_init=lam_init),
        grid_spec=pltpu.PrefetchScalarGridSpec(
            num_scalar_prefetch=2,
            grid=(batch, heads, len(pairs)),
            in_specs=[pl.BlockSpec((tq, LANES), qmap),
                      pl.BlockSpec((tq, LANES), kmap),
                      pl.BlockSpec((tq, LANES), kmap),
                      pl.BlockSpec(lamv.shape, lambda *_: (0, 0)),
                      pl.BlockSpec((1, LANES), lambda *_: (0, 0))],
            out_specs=pl.BlockSpec((tq, LANES), qmap),
            scratch_shapes=[pltpu.VMEM((2 * tq, LANES), F32)] * 3),
        out_shape=jax.ShapeDtypeStruct(q.shape, BF16),
        compiler_params=_cparams(("parallel", "parallel", "arbitrary")),
        name="flash_diff_attn",
    )(qt, kt, q, k, v, lamv, gsub)


def _joint_softmax_attend(qc, pieces):
    ss = []
    for kb, _, bias, visible in pieces:
        s = _dot_nt(qc, kb)
        if bias is not None:
            s = s + bias
        if visible is not None:
            s = jnp.where(visible, s, NEG)
        ss.append(s)
    m = functools.reduce(jnp.maximum, [jnp.max(s, axis=-1, keepdims=True) for s in ss])
    l = 0.0
    o = 0.0
    for s, (_, vb, _, _) in zip(ss, pieces):
        pm = jnp.exp(s - m)
        l = l + jnp.sum(pm, axis=-1, keepdims=True)
        o = o + _dot(pm.astype(BF16), vb)
    return o / l


def _diff_sample_kernel(q_ref, ck_ref, cv_ref, nk_ref, nv_ref, lamv_ref, gsub_ref, o_ref, *, lam_init):
    t_new = q_ref.shape[0]
    pieces = [(ck_ref[...].astype(BF16), cv_ref[...].astype(BF16), None, None),
              (nk_ref[...].astype(BF16), nv_ref[...].astype(BF16), None, None)]
    o = _joint_softmax_attend(jnp.concatenate(_component_queries(q_ref[...]), axis=0), pieces)
    o_ref[...] = _diff_finish(o[:t_new], o[t_new:], lamv_ref, gsub_ref, lam_init)


def _diff_sample(q, ck, cv, nk, nv, lamv, gsub, *, batch, t_new, past, lam_init):
    heads = D_MODEL // LANES
    bh = lambda b, h: (b, h)
    return pl.pallas_call(
        functools.partial(_diff_sample_kernel, lam_init=lam_init),
        grid=(batch, heads),
        in_specs=[pl.BlockSpec((t_new, LANES), bh),
                  pl.BlockSpec((past, LANES), bh), pl.BlockSpec((past, LANES), bh),
                  pl.BlockSpec((t_new, LANES), bh), pl.BlockSpec((t_new, LANES), bh),
                  pl.BlockSpec(lamv.shape, lambda b, h: (0, 0)),
                  pl.BlockSpec((1, LANES), lambda b, h: (0, 0))],
        out_specs=pl.BlockSpec((t_new, LANES), bh),
        out_shape=jax.ShapeDtypeStruct(q.shape, BF16),
        compiler_params=_cparams(("parallel", "parallel")),
        name="diff_attn_sample",
    )(q, ck, cv, nk, nv, lamv, gsub)


def _band_finish(o):
    tq = o.shape[0] // 2
    lane = lax.broadcasted_iota(jnp.int32, (tq, LANES), 1)
    return jnp.where(lane < DH, o[:tq], o[tq:]).astype(BF16)


def _band_prompt_kernel(q_ref, *refs):
    nkv = BAND_QTILES + 2
    k_refs, v_refs, bias_ref, o_ref = refs[:nkv], refs[nkv:2 * nkv], refs[2 * nkv], refs[2 * nkv + 1]
    first = pl.program_id(2) * BAND_QTILES
    tq = q_ref.shape[0] // BAND_QTILES
    row_c = (lax.broadcasted_iota(jnp.int32, (2 * tq, tq), 0) % tq) // CHUNK
    col_c = lax.broadcasted_iota(jnp.int32, (2 * tq, tq), 1) // CHUNK
    kvs = [(k_ref[...].astype(BF16), v_ref[...].astype(BF16)) for k_ref, v_ref in zip(k_refs, v_refs)]
    for sub in range(BAND_QTILES):
        pieces = []
        for r in range(3):
            before_start = jnp.where(first + sub + r - 2 < 0, 4 * PREV_CHUNKS, 0)
            kc = col_c + (r - 2) * (tq // CHUNK) + before_start
            visible = (kc <= row_c) & (kc >= row_c - PREV_CHUNKS)
            pieces.append((*kvs[sub + r], bias_ref[:, r * tq:(r + 1) * tq], visible))
        rows = slice(sub * tq, (sub + 1) * tq)
        qq = jnp.concatenate(_component_queries(q_ref[rows, :]), axis=0)
        o_ref[rows, :] = _band_finish(_joint_softmax_attend(qq, pieces))


def _band_bias_tiles(table, nq, nk, c0):
    n = nq + nk - 1
    diag = jnp.clip(c0 + nq - 1 - np.arange(n), -REL_CLIP, REL_CLIP) + REL_CLIP
    e = table[:, diag]
    a = jnp.tile(e, (1, nq + 1))[:, :nq * (n + 1)].reshape(-1, nq, n + 1)[:, ::-1, :nk]
    return a.reshape(table.shape[0] // 2, 2 * nq, nk)


def _band_prompt(q, k, v, table, *, batch, seq):
    tq = TQ_BAND
    nsub = BAND_QTILES
    assert BAND_PAST == 2 * tq and seq % (nsub * tq) == 0
    nq = seq // tq
    hp = D_MODEL // LANES
    bias = _band_bias_tiles(table, tq, 3 * tq, 2 * tq)
    qmap = lambda h, b, i: (b * (nq // nsub) + i, h)
    kmap = lambda r: (lambda h, b, i: (b * nq + jnp.maximum(nsub * i + r - 2, 0), h))
    kv_specs = [pl.BlockSpec((tq, LANES), kmap(r)) for r in range(nsub + 2)]
    return pl.pallas_call(
        _band_prompt_kernel,
        grid=(hp, batch, nq // nsub),
        in_specs=[pl.BlockSpec((nsub * tq, LANES), qmap)] + kv_specs * 2
                 + [pl.BlockSpec((None, 2 * tq, 3 * tq), lambda h, b, i: (h, 0, 0))],
        out_specs=pl.BlockSpec((nsub * tq, LANES), qmap),
        out_shape=jax.ShapeDtypeStruct(q.shape, BF16),
        compiler_params=_cparams(("parallel", "parallel", "parallel")),
        name="band_attn_prompt",
    )(q, *([k] * (nsub + 2)), *([v] * (nsub + 2)), bias)


def _band_sample_kernel(q_ref, ck_ref, cv_ref, nk_ref, nv_ref, bc_ref, bn_ref, o_ref):
    pieces = [(ck_ref[...].astype(BF16), cv_ref[...].astype(BF16), bc_ref[...], None),
              (nk_ref[...].astype(BF16), nv_ref[...].astype(BF16), bn_ref[...], None)]
    qq = jnp.concatenate(_component_queries(q_ref[...]), axis=0)
    o_ref[...] = _band_finish(_joint_softmax_attend(qq, pieces))


def _band_sample(q, ck, cv, nk, nv, table, *, batch, t_new, past):
    hp = D_MODEL // LANES
    bias = _band_bias_tiles(table, t_new, past + t_new, past)
    bias_c, bias_n = bias[..., :past], bias[..., past:]
    hb = lambda h, b: (b, h)
    return pl.pallas_call(
        _band_sample_kernel,
        grid=(hp, batch),
        in_specs=[pl.BlockSpec((t_new, LANES), hb),
                  pl.BlockSpec((past, LANES), hb), pl.BlockSpec((past, LANES), hb),
                  pl.BlockSpec((t_new, LANES), hb), pl.BlockSpec((t_new, LANES), hb),
                  pl.BlockSpec((None, 2 * t_new, past), lambda h, b: (h, 0, 0)),
                  pl.BlockSpec((None, 2 * t_new, t_new), lambda h, b: (h, 0, 0))],
        out_specs=pl.BlockSpec((t_new, LANES), hb),
        out_shape=jax.ShapeDtypeStruct(q.shape, BF16),
        compiler_params=_cparams(("parallel", "parallel")),
        name="band_attn_sample",
    )(q, ck, cv, nk, nv, bias_c, bias_n)


def _outproj_kernel(o_ref, w_ref, x_ref, gate_ref, y_ref):
    y_ref[...] = x_ref[...] + gate_ref[...] * _dot(o_ref[...], w_ref[...])


def _outproj(o, w_bf, x, mod, *, tm):
    t = x.shape[0]
    row = lambda i: (i, 0)
    return pl.pallas_call(
        _outproj_kernel,
        grid=(t // tm,),
        in_specs=[pl.BlockSpec((tm, D_MODEL), row), pl.BlockSpec(w_bf.shape, lambda i: (0, 0)),
                  pl.BlockSpec((tm, D_MODEL), row), mod.spec(2, tm)],
        out_specs=pl.BlockSpec((tm, D_MODEL), row),
        out_shape=jax.ShapeDtypeStruct(x.shape, F32),
        compiler_params=_cparams(("parallel",)),
        name="out_proj",
    )(o, w_bf, x, mod.arr)


def _top_rows(s, k):
    if s.shape[1] > LANES:
        cols = [_top_rows(s[:, c:c + LANES], k) for c in range(0, s.shape[1], LANES)]
        return tuple(jnp.concatenate(x, axis=1) for x in zip(*cols))
    rows = s.shape[0]
    rid = lax.broadcasted_iota(jnp.int32, s.shape, 0).astype(F32)
    vals, ids = [], []
    for r in range(k):
        cand = [(s[v:v + _SUB], rid[v:v + _SUB]) for v in range(0, rows, _SUB)]
        while len(cand) > 1:
            nxt = []
            for a in range(0, len(cand) - 1, 2):
                (va, ia), (vb, ib) = cand[a], cand[a + 1]
                keep = va >= vb
                nxt.append((jnp.where(keep, va, vb), jnp.where(keep, ia, ib)))
            cand = nxt + cand[len(cand) - len(cand) % 2:]
        v8, i8 = cand[0]
        m = jnp.max(v8, axis=0, keepdims=True)
        i = jnp.min(jnp.where(v8 == m, i8, float(rows)), axis=0, keepdims=True)
        vals.append(m)
        ids.append(i)
        if r + 1 < k:
            s = jnp.where(rid == i, -jnp.inf, s)
    return jnp.concatenate(vals, axis=0), jnp.concatenate(ids, axis=0)


_SUB = 8
_STAIR_PIECES = ([(0, 1, 0, _SUB), (0, 1, _SUB, _SUB), (1, 1, 0, _SUB)]
                 + [(a, 1, 0, TOPK // (a + 1)) for a in range(2, _SUB)] + [(_SUB, _SUB, 0, 1)])


def _stair_candidates(s1, s2):
    sub = lax.broadcasted_iota(jnp.int32, (_SUB, s1.shape[1]), 0)
    pieces = []
    for a0, na, b0, nb in _STAIR_PIECES:
        if na == 1:
            piece = s1[a0:a0 + 1, :] + s2[b0:b0 + _SUB, :]
            if nb < _SUB:
                piece = jnp.where(sub < nb, piece, -jnp.inf)
        else:
            piece = s1[a0:a0 + na, :] + s2[b0:b0 + 1, :]
        pieces.append(piece)
    return jnp.concatenate(pieces, axis=0)


def _stair_ranks(pos):
    a = jnp.zeros(pos.shape, F32)
    b = pos
    for p, (a0, na, b0, nb) in enumerate(_STAIR_PIECES):
        start = float(p * _SUB)
        inside = pos >= start
        if na == 1:
            a = jnp.where(inside, float(a0), a)
            b = jnp.where(inside, pos - start + float(b0), b)
        else:
            a = jnp.where(inside, pos - start + float(a0), a)
            b = jnp.where(inside, float(b0), b)
    return a, b


def _pick_rows(sel, table):
    out = jnp.zeros(sel.shape, F32)
    for a in range(table.shape[0]):
        out = out + jnp.where(sel == float(a), table[a:a + 1, :], 0.0)
    return out


def _select_unit(q_scr, keys_ref, sel_t_scr, h, part):
    tokens = pl.ds(pl.multiple_of(part * LANES, LANES), LANES)
    top = [_top_rows(_dot_nt(keys_ref[2 * h + c], q_scr[2 * h + c, tokens, :]), TOPK) for c in range(2)]
    (s1, i1), (s2, i2) = top
    top_s, pos = _top_rows(_stair_candidates(s1, s2), TOPK)
    a_sel, b_sel = _stair_ranks(pos)
    e = jnp.exp(top_s - top_s[0:1, :])
    rows = pl.ds(pl.multiple_of(h * TOPK, TOPK), TOPK)
    sel_t_scr[0, part, rows, :] = _pick_rows(a_sel, i1)
    sel_t_scr[1, part, rows, :] = _pick_rows(b_sel, i2)
    sel_t_scr[2, part, rows, :] = e / jnp.sum(e, axis=0, keepdims=True)


def _peer_kernel(xs_ref, sh_ref, sc_ref, ng_ref, wq_ref, keys_ref, ut_ref, v_ref, xr_ref, gate_ref, y_ref,
                 hb_scr, q_scr, sel_t_scr, sel_scr, w3_ref, acc_ref):
    i = pl.program_id(0)
    j = pl.program_id(1)
    tm = xs_ref.shape[0]
    nparts = tm // LANES
    pitch = w3_ref.shape[0] // N_KEYS
    e_tile = v_ref.shape[0]
    slot_new, slot_dense = i % 2, (i + 1) % 2

    @pl.when((i == 0) & (j == 0))
    def _():
        w3_ref[...] = jnp.zeros(w3_ref.shape, w3_ref.dtype)
        hb_scr[1] = jnp.zeros(hb_scr.shape[1:], hb_scr.dtype)

    @pl.when((i > 0) & (j == 0))
    def _():
        for k in range(3):
            for part in range(nparts):
                sel_scr[k, part * LANES:(part + 1) * LANES, :] = sel_t_scr[k, part].T
        ids = lax.broadcasted_iota(jnp.int32, (N_KEYS, N_KEYS), 0).astype(F32)

        def token_pair(p, carry):
            ws = []
            for k in range(2):
                t = 2 * p + k
                arow = sel_scr[0, pl.ds(t, 1), :]
                brow = sel_scr[1, pl.ds(t, 1), :]
                grow = sel_scr[2, pl.ds(t, 1), :]
                oa = jnp.where(arow == ids, 1.0, 0.0).astype(BF16)
                ob = jnp.where(brow == ids, 0.5 * grow, 0.0).astype(BF16)
                ws.append(_dot_nt(oa, ob))
            w3_ref[pl.ds(p, N_KEYS, stride=pitch), :] = pltpu.pack_elementwise(ws, packed_dtype=BF16)
            return carry

        lax.fori_loop(0, tm // 2, token_pair, 0, unroll=TOKEN_UNROLL // 2)

    @pl.when(j == 0)
    def _():
        acc_ref[...] = jnp.zeros(acc_ref.shape, F32)
        hb = _modulated(xs_ref[...], ng_ref[...], sc_ref[...], sh_ref[...]).astype(BF16)
        hb_scr[slot_new] = hb
        q_all = _dot(hb, wq_ref[...])
        for hc in range(2 * PEER_HEADS):
            q_scr[hc] = q_all[:, hc * N_KEYS:(hc + 1) * N_KEYS].astype(BF16)

    _select_unit(q_scr, keys_ref, sel_t_scr, j // nparts, j % nparts)

    n_i1 = e_tile // N_KEYS
    w = jnp.concatenate(
        [pltpu.bitcast(w3_ref[pl.ds(pl.multiple_of((n_i1 * j + q) * pitch, 8), tm // 2), :], BF16)
         for q in range(n_i1)], axis=1)
    hid = _dot(hb_scr[slot_dense], ut_ref[...])
    act = hid * (1.0 + lax.erf(hid * math.sqrt(0.5)))
    acc_ref[...] += _dot(act.astype(BF16) * w, v_ref[...])

    @pl.when((i > 0) & (j == pl.num_programs(1) - 1))
    def _():
        y_ref[...] = xr_ref[...] + gate_ref[...] * acc_ref[...]


def _peer(x, mod, ng, wq_bf, keys_bf, ut_bf, v_bf, *, tm):
    t = x.shape[0]
    nt = t // tm
    units = PEER_HEADS * (tm // LANES)
    e_tile = v_bf.shape[0] // units
    assert e_tile % (2 * N_KEYS) == 0 and tm % TOKEN_UNROLL == 0
    new_tile = lambda i: jnp.minimum(i, nt - 1)
    dense_tile = lambda i: jnp.maximum(i - 1, 0)
    full = lambda i, j: (0, 0)
    return pl.pallas_call(
        _peer_kernel,
        grid=(nt + 1, units),
        in_specs=[pl.BlockSpec((tm, D_MODEL), lambda i, j: (new_tile(i), 0)),
                  mod.spec(3, tm, new_tile), mod.spec(4, tm, new_tile),
                  pl.BlockSpec((1, D_MODEL), full),
                  pl.BlockSpec(wq_bf.shape, full),
                  pl.BlockSpec(keys_bf.shape, lambda i, j: (0, 0, 0)),
                  pl.BlockSpec((D_MODEL, e_tile), lambda i, j: (0, j)),
                  pl.BlockSpec((e_tile, D_MODEL), lambda i, j: (j, 0)),
                  pl.BlockSpec((tm, D_MODEL), lambda i, j: (dense_tile(i), 0)),
                  mod.spec(5, tm, dense_tile)],
        out_specs=pl.BlockSpec((tm, D_MODEL), lambda i, j: (dense_tile(i), 0)),
        out_shape=jax.ShapeDtypeStruct(x.shape, F32),
        scratch_shapes=[pltpu.VMEM((2, tm, D_MODEL), BF16),
                        pltpu.VMEM((2 * PEER_HEADS, tm, N_KEYS), BF16),
                        pltpu.VMEM((3, tm // LANES, PEER_HEADS * TOPK, LANES), F32),
                        pltpu.VMEM((3, tm, PEER_HEADS * TOPK), F32),
                        pltpu.VMEM((N_KEYS * (tm // 2 + W3_PAD), N_KEYS), jnp.int32),
                        pltpu.VMEM((tm, D_MODEL), F32)],
        compiler_params=_cparams(("arbitrary", "arbitrary")),
        name="peer",
    )(x, mod.arr, mod.arr, ng, wq_bf, keys_bf, ut_bf, v_bf, x, mod.arr)


def _rope_tables(pos):
    half = DH // 2
    inv = ROPE_THETA ** (-jnp.arange(half, dtype=F32) / half)
    ang = pos.astype(F32)[:, None] * inv[None, :]
    cos, sin = jnp.cos(ang), jnp.sin(ang)
    return jnp.tile(cos, (1, 4)), jnp.concatenate([-sin, sin, -sin, sin], axis=1)


def _pair_tile(g):
    return jnp.tile(g.reshape(1, -1), (1, LANES // g.shape[-1]))


def kernel(x_prompt, x_sample, c_prompt, c_sample, cache_a_k, cache_a_v, cache_b_k, cache_b_v, ada_w, ada_b, norm_g, a_w_in, a_g_q, a_g_k, a_lq1, a_lk1, a_lq2, a_lk2, a_g_sub, a_w_out, b_w_in, b_g_q, b_g_k, b_rel_bias, b_w_out, peer_w_q, peer_sub_keys, peer_u, peer_v):
    batch, seq, _ = x_prompt.shape
    dbatch, t_new, _ = x_sample.shape
    past_a = cache_a_k.shape[2]
    past_b = cache_b_k.shape[2]
    depth = ada_w.shape[0]
    tp, ts = batch * seq, dbatch * t_new
    tm_s = min(TM_PROJ, ts)
    tmd_s = min(TM_DENSE // 2, ts)

    xp = x_prompt.reshape(tp, D_MODEL)
    xs = x_sample.reshape(ts, D_MODEL)
    c_all = jnp.concatenate([c_prompt, c_sample], axis=0)

    cos_p, sin_p = _rope_tables(jnp.arange(seq))
    cos_s, sin_s = _rope_tables(past_a + jnp.arange(t_new))
    cos_s, sin_s = jnp.tile(cos_s, (tm_s // t_new, 1)), jnp.tile(sin_s, (tm_s // t_new, 1))

    outs = {n: [] for n in ("akp", "avp", "aks", "avs", "bkp", "bvp", "bks", "bvs")}
    for i in range(depth):
        j = i // 2
        mod = _ada_mod(c_all, ada_w[i], ada_b[i])
        mod_p = _Mod(mod[:batch].reshape(batch * 6, 1, D_MODEL), False, lambda tm: seq // tm)
        mod_s = _Mod(jnp.repeat(mod[batch:].reshape(dbatch, 6, D_MODEL).transpose(1, 0, 2), t_new, axis=1),
                     True, None)
        ng0, ng1 = norm_g[i, 0].reshape(1, -1), norm_g[i, 1].reshape(1, -1)

        if i % 2 == 0:
            lam_init = 0.8 - 0.6 * math.exp(-0.3 * i)
            w_in = a_w_in[j].astype(BF16)
            gq, gk = _pair_tile(a_g_q[j]), _pair_tile(a_g_k[j])
            lamv = jnp.stack([a_lq1[j], a_lk1[j], a_lq2[j], a_lk2[j]])
            gsub = a_g_sub[j].reshape(1, -1)
            qp, kp, vp = _inproj(xp, mod_p, ng0, w_in, gq, gk, cos_p, sin_p, rope=True, tm=TM_PROJ)
            qs, ks, vs = _inproj(xs, mod_s, ng0, w_in, gq, gk, cos_s, sin_s, rope=True, tm=tm_s)
            op = _flash_diff(qp, kp, vp, lamv, gsub, batch=batch, seq=seq, lam_init=lam_init)
            os_ = _diff_sample(qs, cache_a_k[j].reshape(dbatch * past_a, D_MODEL),
                               cache_a_v[j].reshape(dbatch * past_a, D_MODEL), ks, vs, lamv, gsub,
                               batch=dbatch, t_new=t_new, past=past_a, lam_init=lam_init)
            w_out = a_w_out[j].astype(BF16)
            heads = D_MODEL // LANES
            k4p, v4p = _cache_layout(kp, vp, tm=TM_PROJ)
            k4s, v4s = _cache_layout(ks, vs, tm=tm_s)
            outs["akp"].append(k4p.reshape(batch, seq, heads, 2, DH))
            outs["avp"].append(v4p.reshape(batch, seq, heads, 2 * DH))
            outs["aks"].append(k4s.reshape(dbatch, t_new, heads, 2, DH))
            outs["avs"].append(v4s.reshape(dbatch, t_new, heads, 2 * DH))
        else:
            w_in = b_w_in[j].astype(BF16)
            gq, gk = _pair_tile(b_g_q[j]), _pair_tile(b_g_k[j])
            qp, kp, vp = _inproj(xp, mod_p, ng0, w_in, gq, gk, cos_p, sin_p, rope=False, tm=TM_PROJ)
            qs, ks, vs = _inproj(xs, mod_s, ng0, w_in, gq, gk, cos_s, sin_s, rope=False, tm=tm_s)
            op = _band_prompt(qp, kp, vp, b_rel_bias[j], batch=batch, seq=seq)
            os_ = _band_sample(qs, cache_b_k[j].reshape(dbatch * past_b, D_MODEL),
                               cache_b_v[j].reshape(dbatch * past_b, D_MODEL), ks, vs, b_rel_bias[j],
                               batch=dbatch, t_new=t_new, past=past_b)
            w_out = b_w_out[j].astype(BF16)
            heads = D_MODEL // DH
            keep = min(BAND_PAST, seq)
            k4 = kp.reshape(batch, seq, heads, DH)
            v4 = vp.reshape(batch, seq, heads, DH)
            outs["bkp"].append(k4[:, seq - keep:])
            outs["bvp"].append(v4[:, seq - keep:])
            outs["bks"].append(jnp.concatenate([cache_b_k[j], ks.reshape(dbatch, t_new, heads, DH)], axis=1)[:, t_new:])
            outs["bvs"].append(jnp.concatenate([cache_b_v[j], vs.reshape(dbatch, t_new, heads, DH)], axis=1)[:, t_new:])

        xp = _outproj(op, w_out, xp, mod_p, tm=TM_PROJ)
        xs = _outproj(os_, w_out, xs, mod_s, tm=tm_s)

        wq = peer_w_q[i].astype(BF16)
        keys = peer_sub_keys[i].astype(BF16).reshape(PEER_HEADS * 2, N_KEYS, -1)
        ut_bf, v_bf = peer_u[i].astype(BF16).T, peer_v[i].astype(BF16)
        xp = _peer(xp, mod_p, ng1, wq, keys, ut_bf, v_bf, tm=TM_DENSE)
        xs = _peer(xs, mod_s, ng1, wq, keys, ut_bf, v_bf, tm=tmd_s)

    st = lambda n: jnp.stack(outs[n], 0)
    return (xp.reshape(x_prompt.shape), xs.reshape(x_sample.shape),
            st("akp"), st("avp"), st("aks"), st("avs"), st("bkp"), st("bvp"), st("bks"), st("bvs"))
```

```python
import functools
import math

import numpy as np
import jax
import jax.numpy as jnp
from jax import lax
from jax.experimental import pallas as pl
from jax.experimental.pallas import tpu as pltpu

F32 = jnp.float32
BF16 = jnp.bfloat16

D_MODEL = 1024
CHUNK = 64
EPS = 1e-6
NEG = -1e30
ROPE_THETA = 10000.0
DH = 64
LANES = 128
PREV_CHUNKS = 8
BAND_PAST = PREV_CHUNKS * CHUNK
REL_CLIP = 128
PEER_HEADS = 8
N_KEYS = 128
TOPK = 16
VMEM_LIMIT = 48 * 1024 * 1024

TM_PROJ = 256
TQ_FLASH = 1024
FLASH_ROWS = 1024
TQ_BAND = 256
BAND_QTILES = 4
TM_DENSE = 512
TOKEN_UNROLL = 64
W3_PAD = 8

_NT = (((1,), (1,)), ((), ()))


def _cparams(sem):
    return pltpu.CompilerParams(dimension_semantics=sem, vmem_limit_bytes=VMEM_LIMIT)


def _dot(a, b):
    return jnp.dot(a, b, preferred_element_type=F32)


def _dot_nt(a, b):
    return lax.dot_general(a, b, _NT, preferred_element_type=F32)


def _split(a):
    hi = a.astype(BF16)
    lo = (a - hi.astype(F32)).astype(BF16)
    return hi, lo


def _ada_kernel(c_ref, w_ref, b_ref, o_ref):
    c = c_ref[...]
    a = c * (1.0 / (1.0 + jnp.exp(-c)))
    ah, al = _split(a)
    wh, wl = _split(w_ref[...])
    o_ref[...] = _dot(ah, wh) + _dot(al, wh) + _dot(ah, wl) + b_ref[...]


def _ada_mod(c_all, w, b):
    n, d = c_all.shape
    nout = w.shape[1]
    tn = 512
    return pl.pallas_call(
        _ada_kernel,
        grid=(nout // tn,),
        in_specs=[pl.BlockSpec((n, d), lambda j: (0, 0)),
                  pl.BlockSpec((d, tn), lambda j: (0, j)),
                  pl.BlockSpec((1, tn), lambda j: (0, j))],
        out_specs=pl.BlockSpec((n, tn), lambda j: (0, j)),
        out_shape=jax.ShapeDtypeStruct((n, nout), F32),
        compiler_params=_cparams(("parallel",)),
        name="ada_mod",
    )(c_all, w, b.reshape(1, nout))


class _Mod:
    def __init__(self, arr, per_row, tiles_per_batch):
        self.arr, self.per_row, self.tpb = arr, per_row, tiles_per_batch

    def spec(self, k, tm, tile=lambda i: i):
        if self.per_row:
            return pl.BlockSpec((None, tm, D_MODEL), lambda i, *_: (k, tile(i), 0))
        tpb = self.tpb(tm)
        return pl.BlockSpec((None, 1, D_MODEL), lambda i, *_: ((tile(i) // tpb) * 6 + k, 0, 0))


def _modulated(x, ng, scale, shift):
    ms = jnp.mean(x * x, axis=-1, keepdims=True)
    return (x * lax.rsqrt(ms + EPS) * ng) * (1.0 + scale) + shift


def _inproj_kernel(x_ref, sh_ref, sc_ref, ng_ref, w_ref, gq_ref, gk_ref, cos_ref, sin_ref,
                   q_ref, k_ref, v_ref, *, rope, qscale):
    tm = x_ref.shape[0]
    hb = _modulated(x_ref[...], ng_ref[...], sc_ref[...], sh_ref[...]).astype(BF16)
    lane = lax.broadcasted_iota(jnp.int32, (tm, LANES), 1)
    lo = lane < DH
    swap_sel = (lane & (DH // 2)) != 0

    def norm_rope(xb, g):
        x2 = xb * xb
        slo = jnp.sum(jnp.where(lo, x2, 0.0), axis=-1, keepdims=True)
        shi = jnp.sum(jnp.where(lo, 0.0, x2), axis=-1, keepdims=True)
        ms = jnp.where(lo, slo, shi) * (1.0 / DH)
        y = xb * lax.rsqrt(ms + EPS) * g
        if rope:
            sw = jnp.where(swap_sel, pltpu.roll(y, DH // 2, 1), pltpu.roll(y, LANES - DH // 2, 1))
            y = y * cos_ref[...] + sw * sin_ref[...]
        return y

    nblk = w_ref.shape[1] // (2 * LANES)
    for j in range(nblk):
        acc = _dot(hb, w_ref[:, j * 2 * LANES:(j + 1) * 2 * LANES])
        for half in range(2):
            blk = acc[:, half * LANES:(half + 1) * LANES]
            col = j * 2 * LANES + half * LANES
            if col < D_MODEL:
                q_ref[:, col:col + LANES] = (norm_rope(blk, gq_ref[...]) * qscale).astype(BF16)
            elif col < 2 * D_MODEL:
                k_ref[:, col - D_MODEL:col - D_MODEL + LANES] = norm_rope(blk, gk_ref[...])
            else:
                v_ref[:, col - 2 * D_MODEL:col - 2 * D_MODEL + LANES] = blk


def _inproj(x, mod, ng, w_bf, gq, gk, cos, sin, *, rope, tm):
    t = x.shape[0]
    nrep = cos.shape[0] // tm
    row = lambda i: (i, 0)
    full = lambda i: (0, 0)
    tab = lambda i: (i % nrep, 0)
    return pl.pallas_call(
        functools.partial(_inproj_kernel, rope=rope, qscale=DH ** -0.5),
        grid=(t // tm,),
        in_specs=[pl.BlockSpec((tm, D_MODEL), row), mod.spec(0, tm), mod.spec(1, tm),
                  pl.BlockSpec((1, D_MODEL), full),
                  pl.BlockSpec(w_bf.shape, full),
                  pl.BlockSpec((1, LANES), full), pl.BlockSpec((1, LANES), full),
                  pl.BlockSpec((tm, LANES), tab), pl.BlockSpec((tm, LANES), tab)],
        out_specs=[pl.BlockSpec((tm, D_MODEL), row)] * 3,
        out_shape=[jax.ShapeDtypeStruct((t, D_MODEL), BF16),
                   jax.ShapeDtypeStruct((t, D_MODEL), F32),
                   jax.ShapeDtypeStruct((t, D_MODEL), F32)],
        compiler_params=_cparams(("parallel",)),
        name="qkv_proj",
    )(x, mod.arr, mod.arr, ng, w_bf, gq, gk, cos, sin)


def _cache_layout_kernel(k_ref, v_ref, k4_ref, v4_ref):
    heads = v4_ref.shape[1]
    for h in range(heads):
        kb = k_ref[:, h * LANES:(h + 1) * LANES]
        v4_ref[:, h, :] = v_ref[:, h * LANES:(h + 1) * LANES]
        for c in range(2):
            k4_ref[:, h, c, :] = kb[:, c * DH:(c + 1) * DH]


def _cache_layout(k, v, *, tm):
    t = k.shape[0]
    heads = D_MODEL // LANES
    row = lambda i: (i, 0)
    return pl.pallas_call(
        _cache_layout_kernel,
        grid=(t // tm,),
        in_specs=[pl.BlockSpec((tm, D_MODEL), row)] * 2,
        out_specs=[pl.BlockSpec((tm, heads, 2, DH), lambda i: (i, 0, 0, 0)),
                   pl.BlockSpec((tm, heads, LANES), lambda i: (i, 0, 0))],
        out_shape=[jax.ShapeDtypeStruct((t, heads, 2, DH), F32), jax.ShapeDtypeStruct((t, heads, LANES), F32)],
        compiler_params=_cparams(("parallel",)),
        name="cache_layout",
    )(k, v)


def _diff_lambda(lamv_ref, lam_init):
    lv = lamv_ref[...]
    e1 = jnp.exp(jnp.sum(lv[0:1] * lv[1:2], axis=-1, keepdims=True))
    e2 = jnp.exp(jnp.sum(lv[2:3] * lv[3:4], axis=-1, keepdims=True))
    return e1 - e2 + lam_init


def _diff_finish(o0, o1, lamv_ref, gsub_ref, lam_init):
    o = o0 - _diff_lambda(lamv_ref, lam_init) * o1
    ms = jnp.mean(o * o, axis=-1, keepdims=True)
    return ((o * lax.rsqrt(ms + EPS) * gsub_ref[...]) * (1.0 - lam_init)).astype(BF16)


def _component_queries(q):
    lane = lax.broadcasted_iota(jnp.int32, q.shape, 1)
    zero = jnp.zeros_like(q)
    return jnp.where(lane < DH, q, zero), jnp.where(lane < DH, zero, q)


def _flash_diff_kernel(qt_ref, kt_ref, q_ref, k_ref, v_ref, lamv_ref, gsub_ref, o_ref,
                       m_ref, l_ref, a_ref, *, lam_init):
    p = pl.program_id(2)
    qi = qt_ref[p]
    ki = kt_ref[p]
    tq = q_ref.shape[0]

    @pl.when(ki == 0)
    def _():
        m_ref[...] = jnp.full(m_ref.shape, NEG, F32)
        l_ref[...] = jnp.zeros(l_ref.shape, F32)
        a_ref[...] = jnp.zeros(a_ref.shape, F32)

    def update(masked):
        kb = k_ref[...].astype(BF16)
        vb = v_ref[...].astype(BF16)
        qq = jnp.concatenate(_component_queries(q_ref[...]), axis=0)
        for r in range(2 * tq // FLASH_ROWS):
            rows = slice(r * FLASH_ROWS, (r + 1) * FLASH_ROWS)
            q0 = (r * FLASH_ROWS) % tq
            ncol = min(q0 + FLASH_ROWS, tq) if masked else tq
            s = _dot_nt(qq[rows], kb[:ncol])
            if masked:
                row_c = ((lax.broadcasted_iota(jnp.int32, (FLASH_ROWS, ncol), 0) + q0) % tq) // CHUNK
                col_c = lax.broadcasted_iota(jnp.int32, (FLASH_ROWS, ncol), 1) // CHUNK
                s = jnp.where(col_c <= row_c, s, NEG)
            m_prev = m_ref[rows, :]
            m_new = jnp.maximum(m_prev, jnp.max(s, axis=-1, keepdims=True))
            alpha = jnp.exp(m_prev - m_new)
            pm = jnp.exp(s - jnp.tile(m_new, (1, ncol // LANES)))
            l_ref[rows, :] = alpha * l_ref[rows, :] + jnp.sum(pm, axis=-1, keepdims=True)
            a_ref[rows, :] = alpha * a_ref[rows, :] + _dot(pm.astype(BF16), vb[:ncol])
            m_ref[rows, :] = m_new

    @pl.when(ki < qi)
    def _():
        update(False)

    @pl.when(ki == qi)
    def _():
        update(True)
        o = a_ref[...] / l_ref[...]
        o_ref[...] = _diff_finish(o[:tq], o[tq:], lamv_ref, gsub_ref, lam_init)


def _flash_diff(q, k, v, lamv, gsub, *, batch, seq, lam_init):
    tq = min(TQ_FLASH, seq)
    nq = seq // tq
    heads = D_MODEL // LANES
    pairs = [(a, b) for a in range(nq) for b in range(a + 1)]
    qt = jnp.asarray([a for a, _ in pairs], jnp.int32)
    kt = jnp.asarray([b for _, b in pairs], jnp.int32)
    qmap = lambda b, h, p, qt, kt: (b * nq + qt[p], h)
    kmap = lambda b, h, p, qt, kt: (b * nq + kt[p], h)
    return pl.pallas_call(
        functools.partial(_flash_diff_kernel, lam_init=lam_init),
        grid_spec=pltpu.PrefetchScalarGridSpec(
            num_scalar_prefetch=2,
            grid=(batch, heads, len(pairs)),
            in_specs=[pl.BlockSpec((tq, LANES), qmap),
                      pl.BlockSpec((tq, LANES), kmap),
                      pl.BlockSpec((tq, LANES), kmap),
                      pl.BlockSpec(lamv.shape, lambda *_: (0, 0)),
                      pl.BlockSpec((1, LANES), lambda *_: (0, 0))],
            out_specs=pl.BlockSpec((tq, LANES), qmap),
            scratch_shapes=[pltpu.VMEM((2 * tq, LANES), F32)] * 3),
        out_shape=jax.ShapeDtypeStruct(q.shape, BF16),
        compiler_params=_cparams(("parallel", "parallel", "arbitrary")),
        name="flash_diff_attn",
    )(qt, kt, q, k, v, lamv, gsub)


def _joint_softmax_attend(qc, pieces):
    ss = []
    for kb, _, bias, visible in pieces:
        s = _dot_nt(qc, kb)
        if bias is not None:
            s = s + bias
        if visible is not None:
            s = jnp.where(visible, s, NEG)
        ss.append(s)
    m = functools.reduce(jnp.maximum, [jnp.max(s, axis=-1, keepdims=True) for s in ss])
    l = 0.0
    o = 0.0
    for s, (_, vb, _, _) in zip(ss, pieces):
        pm = jnp.exp(s - m)
        l = l + jnp.sum(pm, axis=-1, keepdims=True)
        o = o + _dot(pm.astype(BF16), vb)
    return o / l


def _diff_sample_kernel(q_ref, ck_ref, cv_ref, nk_ref, nv_ref, lamv_ref, gsub_ref, o_ref, *, lam_init):
    t_new = q_ref.shape[0]
    pieces = [(ck_ref[...].astype(BF16), cv_ref[...].astype(BF16), None, None),
              (nk_ref[...].astype(BF16), nv_ref[...].astype(BF16), None, None)]
    o = _joint_softmax_attend(jnp.concatenate(_component_queries(q_ref[...]), axis=0), pieces)
    o_ref[...] = _diff_finish(o[:t_new], o[t_new:], lamv_ref, gsub_ref, lam_init)


def _diff_sample(q, ck, cv, nk, nv, lamv, gsub, *, batch, t_new, past, lam_init):
    heads = D_MODEL // LANES
    bh = lambda b, h: (b, h)
    return pl.pallas_call(
        functools.partial(_diff_sample_kernel, lam_init=lam_init),
        grid=(batch, heads),
        in_specs=[pl.BlockSpec((t_new, LANES), bh),
                  pl.BlockSpec((past, LANES), bh), pl.BlockSpec((past, LANES), bh),
                  pl.BlockSpec((t_new, LANES), bh), pl.BlockSpec((t_new, LANES), bh),
                  pl.BlockSpec(lamv.shape, lambda b, h: (0, 0)),
                  pl.BlockSpec((1, LANES), lambda b, h: (0, 0))],
        out_specs=pl.BlockSpec((t_new, LANES), bh),
        out_shape=jax.ShapeDtypeStruct(q.shape, BF16),
        compiler_params=_cparams(("parallel", "parallel")),
        name="diff_attn_sample",
    )(q, ck, cv, nk, nv, lamv, gsub)


def _band_finish(o):
    tq = o.shape[0] // 2
    lane = lax.broadcasted_iota(jnp.int32, (tq, LANES), 1)
    return jnp.where(lane < DH, o[:tq], o[tq:]).astype(BF16)


def _band_prompt_kernel(q_ref, *refs):
    nkv = BAND_QTILES + 2
    k_refs, v_refs, bias_ref, o_ref = refs[:nkv], refs[nkv:2 * nkv], refs[2 * nkv], refs[2 * nkv + 1]
    first = pl.program_id(2) * BAND_QTILES
    tq = q_ref.shape[0] // BAND_QTILES
    row_c = (lax.broadcasted_iota(jnp.int32, (2 * tq, tq), 0) % tq) // CHUNK
    col_c = lax.broadcasted_iota(jnp.int32, (2 * tq, tq), 1) // CHUNK
    kvs = [(k_ref[...].astype(BF16), v_ref[...].astype(BF16)) for k_ref, v_ref in zip(k_refs, v_refs)]
    for sub in range(BAND_QTILES):
        pieces = []
        for r in range(3):
            before_start = jnp.where(first + sub + r - 2 < 0, 4 * PREV_CHUNKS, 0)
            kc = col_c + (r - 2) * (tq // CHUNK) + before_start
            visible = (kc <= row_c) & (kc >= row_c - PREV_CHUNKS)
            pieces.append((*kvs[sub + r], bias_ref[:, r * tq:(r + 1) * tq], visible))
        rows = slice(sub * tq, (sub + 1) * tq)
        qq = jnp.concatenate(_component_queries(q_ref[rows, :]), axis=0)
        o_ref[rows, :] = _band_finish(_joint_softmax_attend(qq, pieces))


def _band_bias_tiles(table, nq, nk, c0):
    n = nq + nk - 1
    diag = jnp.clip(c0 + nq - 1 - np.arange(n), -REL_CLIP, REL_CLIP) + REL_CLIP
    e = table[:, diag]
    a = jnp.tile(e, (1, nq + 1))[:, :nq * (n + 1)].reshape(-1, nq, n + 1)[:, ::-1, :nk]
    return a.reshape(table.shape[0] // 2, 2 * nq, nk)


def _band_prompt(q, k, v, table, *, batch, seq):
    tq = TQ_BAND
    nsub = BAND_QTILES
    assert BAND_PAST == 2 * tq and seq % (nsub * tq) == 0
    nq = seq // tq
    hp = D_MODEL // LANES
    bias = _band_bias_tiles(table, tq, 3 * tq, 2 * tq)
    qmap = lambda h, b, i: (b * (nq // nsub) + i, h)
    kmap = lambda r: (lambda h, b, i: (b * nq + jnp.maximum(nsub * i + r - 2, 0), h))
    kv_specs = [pl.BlockSpec((tq, LANES), kmap(r)) for r in range(nsub + 2)]
    return pl.pallas_call(
        _band_prompt_kernel,
        grid=(hp, batch, nq // nsub),
        in_specs=[pl.BlockSpec((nsub * tq, LANES), qmap)] + kv_specs * 2
                 + [pl.BlockSpec((None, 2 * tq, 3 * tq), lambda h, b, i: (h, 0, 0))],
        out_specs=pl.BlockSpec((nsub * tq, LANES), qmap),
        out_shape=jax.ShapeDtypeStruct(q.shape, BF16),
        compiler_params=_cparams(("parallel", "parallel", "parallel")),
        name="band_attn_prompt",
    )(q, *([k] * (nsub + 2)), *([v] * (nsub + 2)), bias)


def _band_sample_kernel(q_ref, ck_ref, cv_ref, nk_ref, nv_ref, bc_ref, bn_ref, o_ref):
    pieces = [(ck_ref[...].astype(BF16), cv_ref[...].astype(BF16), bc_ref[...], None),
              (nk_ref[...].astype(BF16), nv_ref[...].astype(BF16), bn_ref[...], None)]
    qq = jnp.concatenate(_component_queries(q_ref[...]), axis=0)
    o_ref[...] = _band_finish(_joint_softmax_attend(qq, pieces))


def _band_sample(q, ck, cv, nk, nv, table, *, batch, t_new, past):
    hp = D_MODEL // LANES
    bias = _band_bias_tiles(table, t_new, past + t_new, past)
    bias_c, bias_n = bias[..., :past], bias[..., past:]
    hb = lambda h, b: (b, h)
    return pl.pallas_call(
        _band_sample_kernel,
        grid=(hp, batch),
        in_specs=[pl.BlockSpec((t_new, LANES), hb),
                  pl.BlockSpec((past, LANES), hb), pl.BlockSpec((past, LANES), hb),
                  pl.BlockSpec((t_new, LANES), hb), pl.BlockSpec((t_new, LANES), hb),
                  pl.BlockSpec((None, 2 * t_new, past), lambda h, b: (h, 0, 0)),
                  pl.BlockSpec((None, 2 * t_new, t_new), lambda h, b: (h, 0, 0))],
        out_specs=pl.BlockSpec((t_new, LANES), hb),
        out_shape=jax.ShapeDtypeStruct(q.shape, BF16),
        compiler_params=_cparams(("parallel", "parallel")),
        name="band_attn_sample",
    )(q, ck, cv, nk, nv, bias_c, bias_n)


def _outproj_kernel(o_ref, w_ref, x_ref, gate_ref, y_ref):
    y_ref[...] = x_ref[...] + gate_ref[...] * _dot(o_ref[...], w_ref[...])


def _outproj(o, w_bf, x, mod, *, tm):
    t = x.shape[0]
    row = lambda i: (i, 0)
    return pl.pallas_call(
        _outproj_kernel,
        grid=(t // tm,),
        in_specs=[pl.BlockSpec((tm, D_MODEL), row), pl.BlockSpec(w_bf.shape, lambda i: (0, 0)),
                  pl.BlockSpec((tm, D_MODEL), row), mod.spec(2, tm)],
        out_specs=pl.BlockSpec((tm, D_MODEL), row),
        out_shape=jax.ShapeDtypeStruct(x.shape, F32),
        compiler_params=_cparams(("parallel",)),
        name="out_proj",
    )(o, w_bf, x, mod.arr)


def _top_rows(s, k):
    if s.shape[1] > LANES:
        cols = [_top_rows(s[:, c:c + LANES], k) for c in range(0, s.shape[1], LANES)]
        return tuple(jnp.concatenate(x, axis=1) for x in zip(*cols))
    rows = s.shape[0]
    rid = lax.broadcasted_iota(jnp.int32, s.shape, 0).astype(F32)
    vals, ids = [], []
    for r in range(k):
        cand = [(s[v:v + _SUB], rid[v:v + _SUB]) for v in range(0, rows, _SUB)]
        while len(cand) > 1:
            nxt = []
            for a in range(0, len(cand) - 1, 2):
                (va, ia), (vb, ib) = cand[a], cand[a + 1]
                keep = va >= vb
                nxt.append((jnp.where(keep, va, vb), jnp.where(keep, ia, ib)))
            cand = nxt + cand[len(cand) - len(cand) % 2:]
        v8, i8 = cand[0]
        m = jnp.max(v8, axis=0, keepdims=True)
        i = jnp.min(jnp.where(v8 == m, i8, float(rows)), axis=0, keepdims=True)
        vals.append(m)
        ids.append(i)
        if r + 1 < k:
            s = jnp.where(rid == i, -jnp.inf, s)
    return jnp.concatenate(vals, axis=0), jnp.concatenate(ids, axis=0)


_SUB = 8
_STAIR_PIECES = ([(0, 1, 0, _SUB), (0, 1, _SUB, _SUB), (1, 1, 0, _SUB)]
                 + [(a, 1, 0, TOPK // (a + 1)) for a in range(2, _SUB)] + [(_SUB, _SUB, 0, 1)])


def _stair_candidates(s1, s2):
    sub = lax.broadcasted_iota(jnp.int32, (_SUB, s1.shape[1]), 0)
    pieces = []
    for a0, na, b0, nb in _STAIR_PIECES:
        if na == 1:
            piece = s1[a0:a0 + 1, :] + s2[b0:b0 + _SUB, :]
            if nb < _SUB:
                piece = jnp.where(sub < nb, piece, -jnp.inf)
        else:
            piece = s1[a0:a0 + na, :] + s2[b0:b0 + 1, :]
        pieces.append(piece)
    return jnp.concatenate(pieces, axis=0)


def _stair_ranks(pos):
    a = jnp.zeros(pos.shape, F32)
    b = pos
    for p, (a0, na, b0, nb) in enumerate(_STAIR_PIECES):
        start = float(p * _SUB)
        inside = pos >= start
        if na == 1:
            a = jnp.where(inside, float(a0), a)
            b = jnp.where(inside, pos - start + float(b0), b)
        else:
            a = jnp.where(inside, pos - start + float(a0), a)
            b = jnp.where(inside, float(b0), b)
    return a, b


def _pick_rows(sel, table):
    out = jnp.zeros(sel.shape, F32)
    for a in range(table.shape[0]):
        out = out + jnp.where(sel == float(a), table[a:a + 1, :], 0.0)
    return out


def _select_unit(q_scr, keys_ref, sel_t_scr, h, part):
    tokens = pl.ds(pl.multiple_of(part * LANES, LANES), LANES)
    top = [_top_rows(_dot_nt(keys_ref[2 * h + c], q_scr[2 * h + c, tokens, :]), TOPK) for c in range(2)]
    (s1, i1), (s2, i2) = top
    top_s, pos = _top_rows(_stair_candidates(s1, s2), TOPK)
    a_sel, b_sel = _stair_ranks(pos)
    e = jnp.exp(top_s - top_s[0:1, :])
    rows = pl.ds(pl.multiple_of(h * TOPK, TOPK), TOPK)
    sel_t_scr[0, part, rows, :] = _pick_rows(a_sel, i1)
    sel_t_scr[1, part, rows, :] = _pick_rows(b_sel, i2)
    sel_t_scr[2, part, rows, :] = e / jnp.sum(e, axis=0, keepdims=True)


def _peer_kernel(xs_ref, sh_ref, sc_ref, ng_ref, wq_ref, keys_ref, u_ref, v_ref, xr_ref, gate_ref, y_ref,
                 hb_scr, q_scr, sel_t_scr, sel_scr, w3_ref, acc_ref):
    i = pl.program_id(0)
    j = pl.program_id(1)
    tm = xs_ref.shape[0]
    nparts = tm // LANES
    pitch = w3_ref.shape[0] // N_KEYS
    e_tile = v_ref.shape[0]
    slot_new, slot_dense = i % 2, (i + 1) % 2

    @pl.when((i == 0) & (j == 0))
    def _():
        w3_ref[...] = jnp.zeros(w3_ref.shape, w3_ref.dtype)
        hb_scr[1] = jnp.zeros(hb_scr.shape[1:], hb_scr.dtype)

    @pl.when((i > 0) & (j == 0))
    def _():
        for k in range(3):
            for part in range(nparts):
                sel_scr[k, part * LANES:(part + 1) * LANES, :] = sel_t_scr[k, part].T
        ids = lax.broadcasted_iota(jnp.int32, (N_KEYS, N_KEYS), 0).astype(F32)

        def token_pair(p, carry):
            ws = []
            for k in range(2):
                t = 2 * p + k
                arow = sel_scr[0, pl.ds(t, 1), :]
                brow = sel_scr[1, pl.ds(t, 1), :]
                grow = sel_scr[2, pl.ds(t, 1), :]
                oa = jnp.where(arow == ids, 1.0, 0.0).astype(BF16)
                ob = jnp.where(brow == ids, 0.5 * grow, 0.0).astype(BF16)
                ws.append(_dot_nt(oa, ob))
            w3_ref[pl.ds(p, N_KEYS, stride=pitch), :] = pltpu.pack_elementwise(ws, packed_dtype=BF16)
            return carry

        lax.fori_loop(0, tm // 2, token_pair, 0, unroll=TOKEN_UNROLL // 2)

    @pl.when(j == 0)
    def _():
        acc_ref[...] = jnp.zeros(acc_ref.shape, F32)
        hb = _modulated(xs_ref[...], ng_ref[...], sc_ref[...], sh_ref[...]).astype(BF16)
        hb_scr[slot_new] = hb
        q_all = _dot(hb, wq_ref[...])
        for hc in range(2 * PEER_HEADS):
            q_scr[hc] = q_all[:, hc * N_KEYS:(hc + 1) * N_KEYS].astype(BF16)

    _select_unit(q_scr, keys_ref, sel_t_scr, j // nparts, j % nparts)

    n_i1 = e_tile // N_KEYS
    w = jnp.concatenate(
        [pltpu.bitcast(w3_ref[pl.ds(pl.multiple_of((n_i1 * j + q) * pitch, 8), tm // 2), :], BF16)
         for q in range(n_i1)], axis=1)
    hid = _dot_nt(hb_scr[slot_dense], u_ref[...])
    act = hid * (1.0 + lax.erf(hid * math.sqrt(0.5)))
    acc_ref[...] += _dot(act.astype(BF16) * w, v_ref[...])

    @pl.when((i > 0) & (j == pl.num_programs(1) - 1))
    def _():
        y_ref[...] = xr_ref[...] + gate_ref[...] * acc_ref[...]


def _peer(x, mod, ng, wq_bf, keys_bf, u_bf, v_bf, *, tm):
    t = x.shape[0]
    nt = t // tm
    units = PEER_HEADS * (tm // LANES)
    e_tile = v_bf.shape[0] // units
    assert e_tile % (2 * N_KEYS) == 0 and tm % TOKEN_UNROLL == 0
    new_tile = lambda i: jnp.minimum(i, nt - 1)
    dense_tile = lambda i: jnp.maximum(i - 1, 0)
    full = lambda i, j: (0, 0)
    return pl.pallas_call(
        _peer_kernel,
        grid=(nt + 1, units),
        in_specs=[pl.BlockSpec((tm, D_MODEL), lambda i, j: (new_tile(i), 0)),
                  mod.spec(3, tm, new_tile), mod.spec(4, tm, new_tile),
                  pl.BlockSpec((1, D_MODEL), full),
                  pl.BlockSpec(wq_bf.shape, full),
                  pl.BlockSpec(keys_bf.shape, lambda i, j: (0, 0, 0)),
                  pl.BlockSpec((e_tile, D_MODEL), lambda i, j: (j, 0)),
                  pl.BlockSpec((e_tile, D_MODEL), lambda i, j: (j, 0)),
                  pl.BlockSpec((tm, D_MODEL), lambda i, j: (dense_tile(i), 0)),
                  mod.spec(5, tm, dense_tile)],
        out_specs=pl.BlockSpec((tm, D_MODEL), lambda i, j: (dense_tile(i), 0)),
        out_shape=jax.ShapeDtypeStruct(x.shape, F32),
        scratch_shapes=[pltpu.VMEM((2, tm, D_MODEL), BF16),
                        pltpu.VMEM((2 * PEER_HEADS, tm, N_KEYS), BF16),
                        pltpu.VMEM((3, tm // LANES, PEER_HEADS * TOPK, LANES), F32),
                        pltpu.VMEM((3, tm, PEER_HEADS * TOPK), F32),
                        pltpu.VMEM((N_KEYS * (tm // 2 + W3_PAD), N_KEYS), jnp.int32),
                        pltpu.VMEM((tm, D_MODEL), F32)],
        compiler_params=_cparams(("arbitrary", "arbitrary")),
        name="peer",
    )(x, mod.arr, mod.arr, ng, wq_bf, keys_bf, u_bf, v_bf, x, mod.arr)


def _rope_tables(pos):
    half = DH // 2
    inv = ROPE_THETA ** (-jnp.arange(half, dtype=F32) / half)
    ang = pos.astype(F32)[:, None] * inv[None, :]
    cos, sin = jnp.cos(ang), jnp.sin(ang)
    return jnp.tile(cos, (1, 4)), jnp.concatenate([-sin, sin, -sin, sin], axis=1)


def _pair_tile(g):
    return jnp.tile(g.reshape(1, -1), (1, LANES // g.shape[-1]))


def kernel(x_prompt, x_sample, c_prompt, c_sample, cache_a_k, cache_a_v, cache_b_k, cache_b_v, ada_w, ada_b, norm_g, a_w_in, a_g_q, a_g_k, a_lq1, a_lk1, a_lq2, a_lk2, a_g_sub, a_w_out, b_w_in, b_g_q, b_g_k, b_rel_bias, b_w_out, peer_w_q, peer_sub_keys, peer_u, peer_v):
    batch, seq, _ = x_prompt.shape
    dbatch, t_new, _ = x_sample.shape
    past_a = cache_a_k.shape[2]
    past_b = cache_b_k.shape[2]
    depth = ada_w.shape[0]
    tp, ts = batch * seq, dbatch * t_new
    tm_s = min(TM_PROJ, ts)
    tmd_s = min(TM_DENSE // 2, ts)

    xp = x_prompt.reshape(tp, D_MODEL)
    xs = x_sample.reshape(ts, D_MODEL)
    c_all = jnp.concatenate([c_prompt, c_sample], axis=0)

    cos_p, sin_p = _rope_tables(jnp.arange(seq))
    cos_s, sin_s = _rope_tables(past_a + jnp.arange(t_new))
    cos_s, sin_s = jnp.tile(cos_s, (tm_s // t_new, 1)), jnp.tile(sin_s, (tm_s // t_new, 1))

    outs = {n: [] for n in ("akp", "avp", "aks", "avs", "bkp", "bvp", "bks", "bvs")}
    for i in range(depth):
        j = i // 2
        mod = _ada_mod(c_all, ada_w[i], ada_b[i])
        mod_p = _Mod(mod[:batch].reshape(batch * 6, 1, D_MODEL), False, lambda tm: seq // tm)
        mod_s = _Mod(jnp.repeat(mod[batch:].reshape(dbatch, 6, D_MODEL).transpose(1, 0, 2), t_new, axis=1),
                     True, None)
        ng0, ng1 = norm_g[i, 0].reshape(1, -1), norm_g[i, 1].reshape(1, -1)

        if i % 2 == 0:
            lam_init = 0.8 - 0.6 * math.exp(-0.3 * i)
            w_in = a_w_in[j].astype(BF16)
            gq, gk = _pair_tile(a_g_q[j]), _pair_tile(a_g_k[j])
            lamv = jnp.stack([a_lq1[j], a_lk1[j], a_lq2[j], a_lk2[j]])
            gsub = a_g_sub[j].reshape(1, -1)
            qp, kp, vp = _inproj(xp, mod_p, ng0, w_in, gq, gk, cos_p, sin_p, rope=True, tm=TM_PROJ)
            qs, ks, vs = _inproj(xs, mod_s, ng0, w_in, gq, gk, cos_s, sin_s, rope=True, tm=tm_s)
            op = _flash_diff(qp, kp, vp, lamv, gsub, batch=batch, seq=seq, lam_init=lam_init)
            os_ = _diff_sample(qs, cache_a_k[j].reshape(dbatch * past_a, D_MODEL),
                               cache_a_v[j].reshape(dbatch * past_a, D_MODEL), ks, vs, lamv, gsub,
                               batch=dbatch, t_new=t_new, past=past_a, lam_init=lam_init)
            w_out = a_w_out[j].astype(BF16)
            heads = D_MODEL // LANES
            k4p, v4p = _cache_layout(kp, vp, tm=TM_PROJ)
            k4s, v4s = _cache_layout(ks, vs, tm=tm_s)
            outs["akp"].append(k4p.reshape(batch, seq, heads, 2, DH))
            outs["avp"].append(v4p.reshape(batch, seq, heads, 2 * DH))
            outs["aks"].append(k4s.reshape(dbatch, t_new, heads, 2, DH))
            outs["avs"].append(v4s.reshape(dbatch, t_new, heads, 2 * DH))
        else:
            w_in = b_w_in[j].astype(BF16)
            gq, gk = _pair_tile(b_g_q[j]), _pair_tile(b_g_k[j])
            qp, kp, vp = _inproj(xp, mod_p, ng0, w_in, gq, gk, cos_p, sin_p, rope=False, tm=TM_PROJ)
            qs, ks, vs = _inproj(xs, mod_s, ng0, w_in, gq, gk, cos_s, sin_s, rope=False, tm=tm_s)
            op = _band_prompt(qp, kp, vp, b_rel_bias[j], batch=batch, seq=seq)
            os_ = _band_sample(qs, cache_b_k[j].reshape(dbatch * past_b, D_MODEL),
                               cache_b_v[j].reshape(dbatch * past_b, D_MODEL), ks, vs, b_rel_bias[j],
                               batch=dbatch, t_new=t_new, past=past_b)
            w_out = b_w_out[j].astype(BF16)
            heads = D_MODEL // DH
            keep = min(BAND_PAST, seq)
            k4 = kp.reshape(batch, seq, heads, DH)
            v4 = vp.reshape(batch, seq, heads, DH)
            outs["bkp"].append(k4[:, seq - keep:])
            outs["bvp"].append(v4[:, seq - keep:])
            outs["bks"].append(jnp.concatenate([cache_b_k[j], ks.reshape(dbatch, t_new, heads, DH)], axis=1)[:, t_new:])
            outs["bvs"].append(jnp.concatenate([cache_b_v[j], vs.reshape(dbatch, t_new, heads, DH)], axis=1)[:, t_new:])

        xp = _outproj(op, w_out, xp, mod_p, tm=TM_PROJ)
        xs = _outproj(os_, w_out, xs, mod_s, tm=tm_s)

        wq = peer_w_q[i].astype(BF16)
        keys = peer_sub_keys[i].astype(BF16).reshape(PEER_HEADS * 2, N_KEYS, -1)
        u_bf, v_bf = peer_u[i].astype(BF16), peer_v[i].astype(BF16)
        xp = _peer(xp, mod_p, ng1, wq, keys, u_bf, v_bf, tm=TM_DENSE)
        xs = _peer(xs, mod_s, ng1, wq, keys, u_bf, v_bf, tm=tmd_s)

    st = lambda n: jnp.stack(outs[n], 0)
    return (xp.reshape(x_prompt.shape), xs.reshape(x_sample.shape),
            st("akp"), st("avp"), st("aks"), st("avs"), st("bkp"), st("bvp"), st("bks"), st("bvs"))
```

```python
import functools
import math

import numpy as np
import jax
import jax.numpy as jnp
from jax import lax
from jax.experimental import pallas as pl
from jax.experimental.pallas import tpu as pltpu

F32 = jnp.float32
BF16 = jnp.bfloat16

D_MODEL = 1024
CHUNK = 64
EPS = 1e-6
NEG = -1e30
ROPE_THETA = 10000.0
DH = 64
LANES = 128
PREV_CHUNKS = 8
BAND_PAST = PREV_CHUNKS * CHUNK
REL_CLIP = 128
PEER_HEADS = 8
N_KEYS = 128
TOPK = 16
VMEM_LIMIT = 48 * 1024 * 1024

TM_PROJ = 256
TQ_FLASH = 1024
FLASH_ROWS = 1024
TQ_BAND = 256
BAND_QTILES = 4
TM_DENSE = 512
TOKEN_UNROLL = 64
W3_PAD = 8

_NT = (((1,), (1,)), ((), ()))
_LOG2E = math.log2(math.e)


def _cparams(sem):
    return pltpu.CompilerParams(dimension_semantics=sem, vmem_limit_bytes=VMEM_LIMIT)


def _dot(a, b):
    return jnp.dot(a, b, preferred_element_type=F32)


def _dot_nt(a, b):
    return lax.dot_general(a, b, _NT, preferred_element_type=F32)


def _split(a):
    hi = a.astype(BF16)
    lo = (a - hi.astype(F32)).astype(BF16)
    return hi, lo


def _ada_kernel(c_ref, w_ref, b_ref, o_ref):
    c = c_ref[...]
    a = c * (1.0 / (1.0 + jnp.exp(-c)))
    ah, al = _split(a)
    wh, wl = _split(w_ref[...])
    o_ref[...] = _dot(ah, wh) + _dot(al, wh) + _dot(ah, wl) + b_ref[...]


def _ada_mod(c_all, w, b):
    n, d = c_all.shape
    nout = w.shape[1]
    tn = 512
    return pl.pallas_call(
        _ada_kernel,
        grid=(nout // tn,),
        in_specs=[pl.BlockSpec((n, d), lambda j: (0, 0)),
                  pl.BlockSpec((d, tn), lambda j: (0, j)),
                  pl.BlockSpec((1, tn), lambda j: (0, j))],
        out_specs=pl.BlockSpec((n, tn), lambda j: (0, j)),
        out_shape=jax.ShapeDtypeStruct((n, nout), F32),
        compiler_params=_cparams(("parallel",)),
        name="ada_mod",
    )(c_all, w, b.reshape(1, nout))


class _Mod:
    def __init__(self, arr, per_row, tiles_per_batch):
        self.arr, self.per_row, self.tpb = arr, per_row, tiles_per_batch

    def spec(self, k, tm, tile=lambda i: i):
        if self.per_row:
            return pl.BlockSpec((None, tm, D_MODEL), lambda i, *_: (k, tile(i), 0))
        tpb = self.tpb(tm)
        return pl.BlockSpec((None, 1, D_MODEL), lambda i, *_: ((tile(i) // tpb) * 6 + k, 0, 0))


def _modulated(x, ng, scale, shift):
    ms = jnp.mean(x * x, axis=-1, keepdims=True)
    return (x * lax.rsqrt(ms + EPS) * ng) * (1.0 + scale) + shift


def _inproj_kernel(x_ref, sh_ref, sc_ref, ng_ref, w_ref, gq_ref, gk_ref, cos_ref, sin_ref,
                   q_ref, k_ref, v_ref, *, rope, qscale):
    tm = x_ref.shape[0]
    hb = _modulated(x_ref[...], ng_ref[...], sc_ref[...], sh_ref[...]).astype(BF16)
    lane = lax.broadcasted_iota(jnp.int32, (tm, LANES), 1)
    lo = lane < DH
    swap_sel = (lane & (DH // 2)) != 0

    def norm_rope(xb, g):
        x2 = xb * xb
        slo = jnp.sum(jnp.where(lo, x2, 0.0), axis=-1, keepdims=True)
        shi = jnp.sum(jnp.where(lo, 0.0, x2), axis=-1, keepdims=True)
        ms = jnp.where(lo, slo, shi) * (1.0 / DH)
        y = xb * lax.rsqrt(ms + EPS) * g
        if rope:
            sw = jnp.where(swap_sel, pltpu.roll(y, DH // 2, 1), pltpu.roll(y, LANES - DH // 2, 1))
            y = y * cos_ref[...] + sw * sin_ref[...]
        return y

    nblk = w_ref.shape[1] // (2 * LANES)
    for j in range(nblk):
        acc = _dot(hb, w_ref[:, j * 2 * LANES:(j + 1) * 2 * LANES])
        for half in range(2):
            blk = acc[:, half * LANES:(half + 1) * LANES]
            col = j * 2 * LANES + half * LANES
            if col < D_MODEL:
                q_ref[:, col:col + LANES] = (norm_rope(blk, gq_ref[...]) * qscale).astype(BF16)
            elif col < 2 * D_MODEL:
                k_ref[:, col - D_MODEL:col - D_MODEL + LANES] = norm_rope(blk, gk_ref[...])
            else:
                v_ref[:, col - 2 * D_MODEL:col - 2 * D_MODEL + LANES] = blk


def _inproj(x, mod, ng, w_bf, gq, gk, cos, sin, *, rope, tm, qscale=DH ** -0.5):
    t = x.shape[0]
    nrep = cos.shape[0] // tm
    row = lambda i: (i, 0)
    full = lambda i: (0, 0)
    tab = lambda i: (i % nrep, 0)
    return pl.pallas_call(
        functools.partial(_inproj_kernel, rope=rope, qscale=qscale),
        grid=(t // tm,),
        in_specs=[pl.BlockSpec((tm, D_MODEL), row), mod.spec(0, tm), mod.spec(1, tm),
                  pl.BlockSpec((1, D_MODEL), full),
                  pl.BlockSpec(w_bf.shape, full),
                  pl.BlockSpec((1, LANES), full), pl.BlockSpec((1, LANES), full),
                  pl.BlockSpec((tm, LANES), tab), pl.BlockSpec((tm, LANES), tab)],
        out_specs=[pl.BlockSpec((tm, D_MODEL), row)] * 3,
        out_shape=[jax.ShapeDtypeStruct((t, D_MODEL), BF16),
                   jax.ShapeDtypeStruct((t, D_MODEL), F32),
                   jax.ShapeDtypeStruct((t, D_MODEL), F32)],
        compiler_params=_cparams(("parallel",)),
        name="qkv_proj",
    )(x, mod.arr, mod.arr, ng, w_bf, gq, gk, cos, sin)


def _cache_layout_kernel(k_ref, v_ref, k4_ref, v4_ref):
    heads = v4_ref.shape[1]
    for h in range(heads):
        kb = k_ref[:, h * LANES:(h + 1) * LANES]
        v4_ref[:, h, :] = v_ref[:, h * LANES:(h + 1) * LANES]
        for c in range(2):
            k4_ref[:, h, c, :] = kb[:, c * DH:(c + 1) * DH]


def _cache_layout(k, v, *, tm):
    t = k.shape[0]
    heads = D_MODEL // LANES
    row = lambda i: (i, 0)
    return pl.pallas_call(
        _cache_layout_kernel,
        grid=(t // tm,),
        in_specs=[pl.BlockSpec((tm, D_MODEL), row)] * 2,
        out_specs=[pl.BlockSpec((tm, heads, 2, DH), lambda i: (i, 0, 0, 0)),
                   pl.BlockSpec((tm, heads, LANES), lambda i: (i, 0, 0))],
        out_shape=[jax.ShapeDtypeStruct((t, heads, 2, DH), F32), jax.ShapeDtypeStruct((t, heads, LANES), F32)],
        compiler_params=_cparams(("parallel",)),
        name="cache_layout",
    )(k, v)


def _diff_lambda(lamv_ref, lam_init):
    lv = lamv_ref[...]
    e1 = jnp.exp(jnp.sum(lv[0:1] * lv[1:2], axis=-1, keepdims=True))
    e2 = jnp.exp(jnp.sum(lv[2:3] * lv[3:4], axis=-1, keepdims=True))
    return e1 - e2 + lam_init


def _diff_finish(o0, o1, lamv_ref, gsub_ref, lam_init):
    o = o0 - _diff_lambda(lamv_ref, lam_init) * o1
    ms = jnp.mean(o * o, axis=-1, keepdims=True)
    return ((o * lax.rsqrt(ms + EPS) * gsub_ref[...]) * (1.0 - lam_init)).astype(BF16)


def _component_queries(q):
    lane = lax.broadcasted_iota(jnp.int32, q.shape, 1)
    zero = jnp.zeros_like(q)
    return jnp.where(lane < DH, q, zero), jnp.where(lane < DH, zero, q)


def _flash_diff_kernel(qt_ref, kt_ref, q_ref, k_ref, v_ref, lamv_ref, gsub_ref, o_ref,
                       m_ref, l_ref, a_ref, *, lam_init):
    p = pl.program_id(2)
    qi = qt_ref[p]
    ki = kt_ref[p]
    tq = q_ref.shape[0]

    @pl.when(ki == 0)
    def _():
        m_ref[...] = jnp.full(m_ref.shape, NEG, F32)
        l_ref[...] = jnp.zeros(l_ref.shape, F32)
        a_ref[...] = jnp.zeros(a_ref.shape, F32)

    def update(masked):
        kb = k_ref[...].astype(BF16)
        vb = v_ref[...].astype(BF16)
        qq = jnp.concatenate(_component_queries(q_ref[...]), axis=0)
        for r in range(2 * tq // FLASH_ROWS):
            rows = slice(r * FLASH_ROWS, (r + 1) * FLASH_ROWS)
            q0 = (r * FLASH_ROWS) % tq
            ncol = min(q0 + FLASH_ROWS, tq) if masked else tq
            s = _dot_nt(qq[rows], kb[:ncol])
            if masked:
                row_c = ((lax.broadcasted_iota(jnp.int32, (FLASH_ROWS, ncol), 0) + q0) % tq) // CHUNK
                col_c = lax.broadcasted_iota(jnp.int32, (FLASH_ROWS, ncol), 1) // CHUNK
                s = jnp.where(col_c <= row_c, s, NEG)
            m_prev = m_ref[rows, :]
            m_new = jnp.maximum(m_prev, jnp.max(s, axis=-1, keepdims=True))
            alpha = jnp.exp2(m_prev - m_new)
            pm = jnp.exp2(s - jnp.tile(m_new, (1, ncol // LANES)))
            l_ref[rows, :] = alpha * l_ref[rows, :] + jnp.sum(pm, axis=-1, keepdims=True)
            a_ref[rows, :] = alpha * a_ref[rows, :] + _dot(pm.astype(BF16), vb[:ncol])
            m_ref[rows, :] = m_new

    @pl.when(ki < qi)
    def _():
        update(False)

    @pl.when(ki == qi)
    def _():
        update(True)
        o = a_ref[...] / l_ref[...]
        o_ref[...] = _diff_finish(o[:tq], o[tq:], lamv_ref, gsub_ref, lam_init)


def _flash_diff(q, k, v, lamv, gsub, *, batch, seq, lam_init):
    tq = min(TQ_FLASH, seq)
    nq = seq // tq
    heads = D_MODEL // LANES
    pairs = [(a, b) for a in range(nq) for b in range(a + 1)]
    qt = jnp.asarray([a for a, _ in pairs], jnp.int32)
    kt = jnp.asarray([b for _, b in pairs], jnp.int32)
    qmap = lambda b, h, p, qt, kt: (b * nq + qt[p], h)
    kmap = lambda b, h, p, qt, kt: (b * nq + kt[p], h)
    return pl.pallas_call(
        functools.partial(_flash_diff_kernel, lam_init=lam_init),
        grid_spec=pltpu.PrefetchScalarGridSpec(
            num_scalar_prefetch=2,
            grid=(batch, heads, len(pairs)),
            in_specs=[pl.BlockSpec((tq, LANES), qmap),
                      pl.BlockSpec((tq, LANES), kmap),
                      pl.BlockSpec((tq, LANES), kmap),
                      pl.BlockSpec(lamv.shape, lambda *_: (0, 0)),
                      pl.BlockSpec((1, LANES), lambda *_: (0, 0))],
            out_specs=pl.BlockSpec((tq, LANES), qmap),
            scratch_shapes=[pltpu.VMEM((2 * tq, LANES), F32)] * 3),
        out_shape=jax.ShapeDtypeStruct(q.shape, BF16),
        compiler_params=_cparams(("parallel", "parallel", "arbitrary")),
        name="flash_diff_attn",
    )(qt, kt, q, k, v, lamv, gsub)


def _joint_softmax_attend(qc, pieces):
    ss = []
    for kb, _, bias, visible in pieces:
        s = _dot_nt(qc, kb)
        if bias is not None:
            s = s + bias
        if visible is not None:
            s = jnp.where(visible, s, NEG)
        ss.append(s)
    m = functools.reduce(jnp.maximum, [jnp.max(s, axis=-1, keepdims=True) for s in ss])
    l = 0.0
    o = 0.0
    for s, (_, vb, _, _) in zip(ss, pieces):
        pm = jnp.exp(s - m)
        l = l + jnp.sum(pm, axis=-1, keepdims=True)
        o = o + _dot(pm.astype(BF16), vb)
    return o / l


def _diff_sample_kernel(q_ref, ck_ref, cv_ref, nk_ref, nv_ref, lamv_ref, gsub_ref, o_ref, *, lam_init):
    t_new = q_ref.shape[0]
    pieces = [(ck_ref[...].astype(BF16), cv_ref[...].astype(BF16), None, None),
              (nk_ref[...].astype(BF16), nv_ref[...].astype(BF16), None, None)]
    o = _joint_softmax_attend(jnp.concatenate(_component_queries(q_ref[...]), axis=0), pieces)
    o_ref[...] = _diff_finish(o[:t_new], o[t_new:], lamv_ref, gsub_ref, lam_init)


def _diff_sample(q, ck, cv, nk, nv, lamv, gsub, *, batch, t_new, past, lam_init):
    heads = D_MODEL // LANES
    bh = lambda b, h: (b, h)
    return pl.pallas_call(
        functools.partial(_diff_sample_kernel, lam_init=lam_init),
        grid=(batch, heads),
        in_specs=[pl.BlockSpec((t_new, LANES), bh),
                  pl.BlockSpec((past, LANES), bh), pl.BlockSpec((past, LANES), bh),
                  pl.BlockSpec((t_new, LANES), bh), pl.BlockSpec((t_new, LANES), bh),
                  pl.BlockSpec(lamv.shape, lambda b, h: (0, 0)),
                  pl.BlockSpec((1, LANES), lambda b, h: (0, 0))],
        out_specs=pl.BlockSpec((t_new, LANES), bh),
        out_shape=jax.ShapeDtypeStruct(q.shape, BF16),
        compiler_params=_cparams(("parallel", "parallel")),
        name="diff_attn_sample",
    )(q, ck, cv, nk, nv, lamv, gsub)


def _band_finish(o):
    tq = o.shape[0] // 2
    lane = lax.broadcasted_iota(jnp.int32, (tq, LANES), 1)
    return jnp.where(lane < DH, o[:tq], o[tq:]).astype(BF16)


def _band_prompt_kernel(q_ref, *refs):
    nkv = BAND_QTILES + 2
    k_refs, v_refs, bias_ref, o_ref = refs[:nkv], refs[nkv:2 * nkv], refs[2 * nkv], refs[2 * nkv + 1]
    first = pl.program_id(2) * BAND_QTILES
    tq = q_ref.shape[0] // BAND_QTILES
    row_c = (lax.broadcasted_iota(jnp.int32, (2 * tq, tq), 0) % tq) // CHUNK
    col_c = lax.broadcasted_iota(jnp.int32, (2 * tq, tq), 1) // CHUNK
    kvs = [(k_ref[...].astype(BF16), v_ref[...].astype(BF16)) for k_ref, v_ref in zip(k_refs, v_refs)]
    for sub in range(BAND_QTILES):
        pieces = []
        for r in range(3):
            before_start = jnp.where(first + sub + r - 2 < 0, 4 * PREV_CHUNKS, 0)
            kc = col_c + (r - 2) * (tq // CHUNK) + before_start
            visible = (kc <= row_c) & (kc >= row_c - PREV_CHUNKS)
            pieces.append((*kvs[sub + r], bias_ref[:, r * tq:(r + 1) * tq], visible))
        rows = slice(sub * tq, (sub + 1) * tq)
        qq = jnp.concatenate(_component_queries(q_ref[rows, :]), axis=0)
        o_ref[rows, :] = _band_finish(_joint_softmax_attend(qq, pieces))


def _band_bias_tiles(table, nq, nk, c0):
    n = nq + nk - 1
    diag = jnp.clip(c0 + nq - 1 - np.arange(n), -REL_CLIP, REL_CLIP) + REL_CLIP
    e = table[:, diag]
    a = jnp.tile(e, (1, nq + 1))[:, :nq * (n + 1)].reshape(-1, nq, n + 1)[:, ::-1, :nk]
    return a.reshape(table.shape[0] // 2, 2 * nq, nk)


def _band_prompt(q, k, v, table, *, batch, seq):
    tq = TQ_BAND
    nsub = BAND_QTILES
    assert BAND_PAST == 2 * tq and seq % (nsub * tq) == 0
    nq = seq // tq
    hp = D_MODEL // LANES
    bias = _band_bias_tiles(table, tq, 3 * tq, 2 * tq)
    qmap = lambda h, b, i: (b * (nq // nsub) + i, h)
    kmap = lambda r: (lambda h, b, i: (b * nq + jnp.maximum(nsub * i + r - 2, 0), h))
    kv_specs = [pl.BlockSpec((tq, LANES), kmap(r)) for r in range(nsub + 2)]
    return pl.pallas_call(
        _band_prompt_kernel,
        grid=(hp, batch, nq // nsub),
        in_specs=[pl.BlockSpec((nsub * tq, LANES), qmap)] + kv_specs * 2
                 + [pl.BlockSpec((None, 2 * tq, 3 * tq), lambda h, b, i: (h, 0, 0))],
        out_specs=pl.BlockSpec((nsub * tq, LANES), qmap),
        out_shape=jax.ShapeDtypeStruct(q.shape, BF16),
        compiler_params=_cparams(("parallel", "parallel", "parallel")),
        name="band_attn_prompt",
    )(q, *([k] * (nsub + 2)), *([v] * (nsub + 2)), bias)


def _band_sample_kernel(q_ref, ck_ref, cv_ref, nk_ref, nv_ref, bc_ref, bn_ref, o_ref):
    pieces = [(ck_ref[...].astype(BF16), cv_ref[...].astype(BF16), bc_ref[...], None),
              (nk_ref[...].astype(BF16), nv_ref[...].astype(BF16), bn_ref[...], None)]
    qq = jnp.concatenate(_component_queries(q_ref[...]), axis=0)
    o_ref[...] = _band_finish(_joint_softmax_attend(qq, pieces))


def _band_sample(q, ck, cv, nk, nv, table, *, batch, t_new, past):
    hp = D_MODEL // LANES
    bias = _band_bias_tiles(table, t_new, past + t_new, past)
    bias_c, bias_n = bias[..., :past], bias[..., past:]
    hb = lambda h, b: (b, h)
    return pl.pallas_call(
        _band_sample_kernel,
        grid=(hp, batch),
        in_specs=[pl.BlockSpec((t_new, LANES), hb),
                  pl.BlockSpec((past, LANES), hb), pl.BlockSpec((past, LANES), hb),
                  pl.BlockSpec((t_new, LANES), hb), pl.BlockSpec((t_new, LANES), hb),
                  pl.BlockSpec((None, 2 * t_new, past), lambda h, b: (h, 0, 0)),
                  pl.BlockSpec((None, 2 * t_new, t_new), lambda h, b: (h, 0, 0))],
        out_specs=pl.BlockSpec((t_new, LANES), hb),
        out_shape=jax.ShapeDtypeStruct(q.shape, BF16),
        compiler_params=_cparams(("parallel", "parallel")),
        name="band_attn_sample",
    )(q, ck, cv, nk, nv, bias_c, bias_n)


def _outproj_kernel(o_ref, w_ref, x_ref, gate_ref, y_ref):
    y_ref[...] = x_ref[...] + gate_ref[...] * _dot(o_ref[...], w_ref[...])


def _outproj(o, w_bf, x, mod, *, tm):
    t = x.shape[0]
    row = lambda i: (i, 0)
    return pl.pallas_call(
        _outproj_kernel,
        grid=(t // tm,),
        in_specs=[pl.BlockSpec((tm, D_MODEL), row), pl.BlockSpec(w_bf.shape, lambda i: (0, 0)),
                  pl.BlockSpec((tm, D_MODEL), row), mod.spec(2, tm)],
        out_specs=pl.BlockSpec((tm, D_MODEL), row),
        out_shape=jax.ShapeDtypeStruct(x.shape, F32),
        compiler_params=_cparams(("parallel",)),
        name="out_proj",
    )(o, w_bf, x, mod.arr)


def _top_rows(s, k, ids=None):
    if s.shape[1] > LANES:
        cols = [_top_rows(s[:, c:c + LANES], k, None if ids is None else ids[:, c:c + LANES])
                for c in range(0, s.shape[1], LANES)]
        return tuple(jnp.concatenate(x, axis=1) for x in zip(*cols))
    rows = s.shape[0]
    rid = lax.broadcasted_iota(jnp.int32, s.shape, 0).astype(F32) if ids is None else ids
    vals, ids = [], []
    for r in range(k):
        cand = [(s[v:v + _SUB], rid[v:v + _SUB]) for v in range(0, rows, _SUB)]
        while len(cand) > 1:
            nxt = []
            for a in range(0, len(cand) - 1, 2):
                (va, ia), (vb, ib) = cand[a], cand[a + 1]
                keep = va >= vb
                nxt.append((jnp.where(keep, va, vb), jnp.where(keep, ia, ib)))
            cand = nxt + cand[len(cand) - len(cand) % 2:]
        v8, i8 = cand[0]
        m = jnp.max(v8, axis=0, keepdims=True)
        i = jnp.min(jnp.where(v8 == m, i8, _NO_ID), axis=0, keepdims=True)
        vals.append(m)
        ids.append(i)
        if r + 1 < k:
            s = jnp.where(rid == i, -jnp.inf, s)
    return jnp.concatenate(vals, axis=0), jnp.concatenate(ids, axis=0)


_SUB = 8
_NO_ID = 4096.0
_STAIR_GROUPS = ([(0, 0, 8)], [(0, 8, 8)], [(1, 0, 8)], [(2, 0, 5), (4, 0, 3)], [(3, 0, 4), (5, 0, 2), (6, 0, 2)],
                 [(7, 0, 2)])
assert all(n == TOPK // (a + 1) for g in _STAIR_GROUPS for a, _, n in g if a > 0)


def _stair_candidates(s1, s2):
    n = s1.shape[1]
    sub = lax.broadcasted_iota(jnp.int32, (_SUB, n), 0)
    subf = sub.astype(F32)
    vals, ids = [], []
    for group in _STAIR_GROUPS:
        v = jnp.full((_SUB, n), -jnp.inf, F32)
        d = jnp.full((_SUB, n), _NO_ID, F32)
        at = 0
        for a, b0, count in group:
            blk = s2[b0:b0 + _SUB, :]
            seg_v = s1[a:a + 1, :] + (blk if at == 0 else pltpu.roll(blk, at, 0))
            seg_d = subf + float(a * TOPK + b0 - at)
            inside = sub < at + count if at == 0 else (sub >= at) & (sub < at + count)
            v = jnp.where(inside, seg_v, v)
            d = jnp.where(inside, seg_d, d)
            at += count
        vals.append(v)
        ids.append(d)
    vals.append(s1[_SUB:, :] + s2[0:1, :])
    ids.append(float(TOPK) * subf + float(_SUB * TOPK))
    return jnp.concatenate(vals, axis=0), jnp.concatenate(ids, axis=0)


def _pick_rows(sel, table):
    out = jnp.zeros(sel.shape, F32)
    for a in range(table.shape[0]):
        out = out + jnp.where(sel == float(a), table[a:a + 1, :], 0.0)
    return out


def _select_unit(q_scr, keys_ref, sel_t_scr, h, part):
    tokens = pl.ds(pl.multiple_of(part * LANES, LANES), LANES)
    top = [_top_rows(_dot_nt(keys_ref[2 * h + c], q_scr[2 * h + c, tokens, :]), TOPK) for c in range(2)]
    (s1, i1), (s2, i2) = top
    cand, cand_ids = _stair_candidates(s1, s2)
    top_s, flat = _top_rows(cand, TOPK, cand_ids)
    a_sel = jnp.floor(flat * (1.0 / TOPK))
    b_sel = flat - a_sel * TOPK
    e = jnp.exp(top_s - top_s[0:1, :])
    rows = pl.ds(pl.multiple_of(h * TOPK, TOPK), TOPK)
    sel_t_scr[0, part, rows, :] = _pick_rows(a_sel, i1)
    sel_t_scr[1, part, rows, :] = _pick_rows(b_sel, i2)
    sel_t_scr[2, part, rows, :] = e / jnp.sum(e, axis=0, keepdims=True)


def _peer_kernel(xs_ref, sh_ref, sc_ref, ng_ref, wq_ref, keys_ref, u_ref, v_ref, xr_ref, gate_ref, y_ref,
                 hb_scr, q_scr, sel_t_scr, sel_scr, w3_ref, acc_ref):
    i = pl.program_id(0)
    j = pl.program_id(1)
    tm = xs_ref.shape[0]
    nparts = tm // LANES
    half = N_KEYS // 2
    pitch = w3_ref.shape[0] // half
    e_tile = u_ref.shape[0]
    slot_new, slot_dense = i % 2, (i + 1) % 2

    @pl.when((i == 0) & (j == 0))
    def _():
        w3_ref[...] = jnp.zeros(w3_ref.shape, w3_ref.dtype)
        hb_scr[1] = jnp.zeros(hb_scr.shape[1:], hb_scr.dtype)

    @pl.when((i > 0) & (j == 0))
    def _():
        for k in range(3):
            for part in range(nparts):
                sel_scr[k, part * LANES:(part + 1) * LANES, :] = sel_t_scr[k, part].T
        row = lax.broadcasted_iota(jnp.int32, (N_KEYS, N_KEYS), 0)
        i1_ids = jnp.where(row < half, 2 * row, 2 * (row - half) + 1).astype(F32)
        i2_ids = row.astype(F32)

        def token(t, carry):
            arow = sel_scr[0, pl.ds(t, 1), :]
            brow = sel_scr[1, pl.ds(t, 1), :]
            grow = sel_scr[2, pl.ds(t, 1), :]
            oa = jnp.where(arow == i1_ids, 1.0, 0.0).astype(BF16)
            ob = jnp.where(brow == i2_ids, 0.5 * grow, 0.0).astype(BF16)
            w = _dot_nt(oa, ob)
            w3_ref[pl.ds(t, half, stride=pitch), :] = pltpu.pack_elementwise([w[:half], w[half:]], packed_dtype=BF16)
            return carry

        lax.fori_loop(0, tm, token, 0, unroll=TOKEN_UNROLL)

    @pl.when(j == 0)
    def _():
        acc_ref[...] = jnp.zeros(acc_ref.shape, F32)
        hb = _modulated(xs_ref[...], ng_ref[...], sc_ref[...], sh_ref[...]).astype(BF16)
        hb_scr[slot_new] = hb
        q_all = _dot(hb, wq_ref[...])
        for hc in range(2 * PEER_HEADS):
            q_scr[hc] = q_all[:, hc * N_KEYS:(hc + 1) * N_KEYS].astype(BF16)

    _select_unit(q_scr, keys_ref, sel_t_scr, j // nparts, j % nparts)

    cols = []
    for q in range(e_tile // (2 * N_KEYS)):
        word = w3_ref[pl.ds(pl.multiple_of((e_tile // (2 * N_KEYS) * j + q) * pitch, 8), tm), :]
        cols += [pltpu.unpack_elementwise(word, index=k, packed_dtype=BF16, unpacked_dtype=F32) for k in range(2)]
    w = jnp.concatenate(cols, axis=1)
    hid = _dot_nt(hb_scr[slot_dense], u_ref[...])
    act = hid * (1.0 + lax.erf(hid * math.sqrt(0.5)))
    acc_ref[...] += _dot((w * act).astype(BF16), v_ref[...])

    @pl.when((i > 0) & (j == pl.num_programs(1) - 1))
    def _():
        y_ref[...] = xr_ref[...] + gate_ref[...] * acc_ref[...]


def _peer(x, mod, ng, wq_bf, keys_bf, u_bf, v_bf, *, tm):
    t = x.shape[0]
    nt = t // tm
    units = PEER_HEADS * (tm // LANES)
    e_tile = u_bf.shape[0] // units
    assert e_tile % (2 * N_KEYS) == 0 and tm % TOKEN_UNROLL == 0
    new_tile = lambda i: jnp.minimum(i, nt - 1)
    dense_tile = lambda i: jnp.maximum(i - 1, 0)
    full = lambda i, j: (0, 0)
    return pl.pallas_call(
        _peer_kernel,
        grid=(nt + 1, units),
        in_specs=[pl.BlockSpec((tm, D_MODEL), lambda i, j: (new_tile(i), 0)),
                  mod.spec(3, tm, new_tile), mod.spec(4, tm, new_tile),
                  pl.BlockSpec((1, D_MODEL), full),
                  pl.BlockSpec(wq_bf.shape, full),
                  pl.BlockSpec(keys_bf.shape, lambda i, j: (0, 0, 0)),
                  pl.BlockSpec((e_tile, D_MODEL), lambda i, j: (j, 0)),
                  pl.BlockSpec((e_tile, D_MODEL), lambda i, j: (j, 0)),
                  pl.BlockSpec((tm, D_MODEL), lambda i, j: (dense_tile(i), 0)),
                  mod.spec(5, tm, dense_tile)],
        out_specs=pl.BlockSpec((tm, D_MODEL), lambda i, j: (dense_tile(i), 0)),
        out_shape=jax.ShapeDtypeStruct(x.shape, F32),
        scratch_shapes=[pltpu.VMEM((2, tm, D_MODEL), BF16),
                        pltpu.VMEM((2 * PEER_HEADS, tm, N_KEYS), BF16),
                        pltpu.VMEM((3, tm // LANES, PEER_HEADS * TOPK, LANES), F32),
                        pltpu.VMEM((3, tm, PEER_HEADS * TOPK), F32),
                        pltpu.VMEM((N_KEYS // 2 * (tm + W3_PAD), N_KEYS), jnp.int32),
                        pltpu.VMEM((tm, D_MODEL), F32)],
        compiler_params=_cparams(("arbitrary", "arbitrary")),
        name="peer",
    )(x, mod.arr, mod.arr, ng, wq_bf, keys_bf, u_bf, v_bf, x, mod.arr)


def _rope_tables(pos):
    half = DH // 2
    inv = ROPE_THETA ** (-jnp.arange(half, dtype=F32) / half)
    ang = pos.astype(F32)[:, None] * inv[None, :]
    cos, sin = jnp.cos(ang), jnp.sin(ang)
    return jnp.tile(cos, (1, 4)), jnp.concatenate([-sin, sin, -sin, sin], axis=1)


def _pair_tile(g):
    return jnp.tile(g.reshape(1, -1), (1, LANES // g.shape[-1]))


def kernel(x_prompt, x_sample, c_prompt, c_sample, cache_a_k, cache_a_v, cache_b_k, cache_b_v, ada_w, ada_b, norm_g, a_w_in, a_g_q, a_g_k, a_lq1, a_lk1, a_lq2, a_lk2, a_g_sub, a_w_out, b_w_in, b_g_q, b_g_k, b_rel_bias, b_w_out, peer_w_q, peer_sub_keys, peer_u, peer_v):
    batch, seq, _ = x_prompt.shape
    dbatch, t_new, _ = x_sample.shape
    past_a = cache_a_k.shape[2]
    past_b = cache_b_k.shape[2]
    depth = ada_w.shape[0]
    tp, ts = batch * seq, dbatch * t_new
    tm_s = min(TM_PROJ, ts)
    tmd_s = min(TM_DENSE // 2, ts)

    xp = x_prompt.reshape(tp, D_MODEL)
    xs = x_sample.reshape(ts, D_MODEL)
    c_all = jnp.concatenate([c_prompt, c_sample], axis=0)

    cos_p, sin_p = _rope_tables(jnp.arange(seq))
    cos_s, sin_s = _rope_tables(past_a + jnp.arange(t_new))
    cos_s, sin_s = jnp.tile(cos_s, (tm_s // t_new, 1)), jnp.tile(sin_s, (tm_s // t_new, 1))

    outs = {n: [] for n in ("akp", "avp", "aks", "avs", "bkp", "bvp", "bks", "bvs")}
    for i in range(depth):
        j = i // 2
        mod = _ada_mod(c_all, ada_w[i], ada_b[i])
        mod_p = _Mod(mod[:batch].reshape(batch * 6, 1, D_MODEL), False, lambda tm: seq // tm)
        mod_s = _Mod(jnp.repeat(mod[batch:].reshape(dbatch, 6, D_MODEL).transpose(1, 0, 2), t_new, axis=1),
                     True, None)
        ng0, ng1 = norm_g[i, 0].reshape(1, -1), norm_g[i, 1].reshape(1, -1)

        if i % 2 == 0:
            lam_init = 0.8 - 0.6 * math.exp(-0.3 * i)
            w_in = a_w_in[j].astype(BF16)
            gq, gk = _pair_tile(a_g_q[j]), _pair_tile(a_g_k[j])
            lamv = jnp.stack([a_lq1[j], a_lk1[j], a_lq2[j], a_lk2[j]])
            gsub = a_g_sub[j].reshape(1, -1)
            qp, kp, vp = _inproj(xp, mod_p, ng0, w_in, gq, gk, cos_p, sin_p, rope=True, tm=TM_PROJ,
                                 qscale=DH ** -0.5 * _LOG2E)
            qs, ks, vs = _inproj(xs, mod_s, ng0, w_in, gq, gk, cos_s, sin_s, rope=True, tm=tm_s)
            op = _flash_diff(qp, kp, vp, lamv, gsub, batch=batch, seq=seq, lam_init=lam_init)
            os_ = _diff_sample(qs, cache_a_k[j].reshape(dbatch * past_a, D_MODEL),
                               cache_a_v[j].reshape(dbatch * past_a, D_MODEL), ks, vs, lamv, gsub,
                               batch=dbatch, t_new=t_new, past=past_a, lam_init=lam_init)
            w_out = a_w_out[j].astype(BF16)
            heads = D_MODEL // LANES
            k4p, v4p = _cache_layout(kp, vp, tm=TM_PROJ)
            k4s, v4s = _cache_layout(ks, vs, tm=tm_s)
            outs["akp"].append(k4p.reshape(batch, seq, heads, 2, DH))
            outs["avp"].append(v4p.reshape(batch, seq, heads, 2 * DH))
            outs["aks"].append(k4s.reshape(dbatch, t_new, heads, 2, DH))
            outs["avs"].append(v4s.reshape(dbatch, t_new, heads, 2 * DH))
        else:
            w_in = b_w_in[j].astype(BF16)
            gq, gk = _pair_tile(b_g_q[j]), _pair_tile(b_g_k[j])
            qp, kp, vp = _inproj(xp, mod_p, ng0, w_in, gq, gk, cos_p, sin_p, rope=False, tm=TM_PROJ)
            qs, ks, vs = _inproj(xs, mod_s, ng0, w_in, gq, gk, cos_s, sin_s, rope=False, tm=tm_s)
            op = _band_prompt(qp, kp, vp, b_rel_bias[j], batch=batch, seq=seq)
            os_ = _band_sample(qs, cache_b_k[j].reshape(dbatch * past_b, D_MODEL),
                               cache_b_v[j].reshape(dbatch * past_b, D_MODEL), ks, vs, b_rel_bias[j],
                               batch=dbatch, t_new=t_new, past=past_b)
            w_out = b_w_out[j].astype(BF16)
            heads = D_MODEL // DH
            keep = min(BAND_PAST, seq)
            k4 = kp.reshape(batch, seq, heads, DH)
            v4 = vp.reshape(batch, seq, heads, DH)
            outs["bkp"].append(k4[:, seq - keep:])
            outs["bvp"].append(v4[:, seq - keep:])
            outs["bks"].append(jnp.concatenate([cache_b_k[j], ks.reshape(dbatch, t_new, heads, DH)], axis=1)[:, t_new:])
            outs["bvs"].append(jnp.concatenate([cache_b_v[j], vs.reshape(dbatch, t_new, heads, DH)], axis=1)[:, t_new:])

        xp = _outproj(op, w_out, xp, mod_p, tm=TM_PROJ)
        xs = _outproj(os_, w_out, xs, mod_s, tm=tm_s)

        wq = peer_w_q[i].astype(BF16)
        keys = peer_sub_keys[i].astype(BF16).reshape(PEER_HEADS * 2, N_KEYS, -1)
        u_bf, v_bf = peer_u[i].astype(BF16), peer_v[i].astype(BF16)
        xp = _peer(xp, mod_p, ng1, wq, keys, u_bf, v_bf, tm=TM_DENSE)
        xs = _peer(xs, mod_s, ng1, wq, keys, u_bf, v_bf, tm=tmd_s)

    st = lambda n: jnp.stack(outs[n], 0)
    return (xp.reshape(x_prompt.shape), xs.reshape(x_sample.shape),
            st("akp"), st("avp"), st("aks"), st("avs"), st("bkp"), st("bvp"), st("bks"), st("bvs"))
```

```python
import functools
import math

import numpy as np
import jax
import jax.numpy as jnp
from jax import lax
from jax.experimental import pallas as pl
from jax.experimental.pallas import tpu as pltpu

F32 = jnp.float32
BF16 = jnp.bfloat16

D_MODEL = 1024
CHUNK = 64
EPS = 1e-6
NEG = -1e30
ROPE_THETA = 10000.0
DH = 64
LANES = 128
PREV_CHUNKS = 8
BAND_PAST = PREV_CHUNKS * CHUNK
REL_CLIP = 128
PEER_HEADS = 8
N_KEYS = 128
TOPK = 16
VMEM_LIMIT = 48 * 1024 * 1024

TM_PROJ = 256
TQ_FLASH = 1024
FLASH_ROWS = 1024
TQ_BAND = 256
BAND_QTILES = 4
TM_DENSE = 512
TOKEN_UNROLL = 64
W3_PAD = 8

_NT = (((1,), (1,)), ((), ()))
_LOG2E = math.log2(math.e)


def _cparams(sem):
    return pltpu.CompilerParams(dimension_semantics=sem, vmem_limit_bytes=VMEM_LIMIT)


def _dot(a, b):
    return jnp.dot(a, b, preferred_element_type=F32)


def _dot_nt(a, b):
    return lax.dot_general(a, b, _NT, preferred_element_type=F32)


def _split(a):
    hi = a.astype(BF16)
    lo = (a - hi.astype(F32)).astype(BF16)
    return hi, lo


def _ada_kernel(c_ref, w_ref, b_ref, o_ref):
    c = c_ref[...]
    a = c * (1.0 / (1.0 + jnp.exp(-c)))
    ah, al = _split(a)
    wh, wl = _split(w_ref[...])
    o_ref[...] = _dot(ah, wh) + _dot(al, wh) + _dot(ah, wl) + b_ref[...]


def _ada_mod(c_all, w, b):
    n, d = c_all.shape
    nout = w.shape[1]
    tn = 512
    return pl.pallas_call(
        _ada_kernel,
        grid=(nout // tn,),
        in_specs=[pl.BlockSpec((n, d), lambda j: (0, 0)),
                  pl.BlockSpec((d, tn), lambda j: (0, j)),
                  pl.BlockSpec((1, tn), lambda j: (0, j))],
        out_specs=pl.BlockSpec((n, tn), lambda j: (0, j)),
        out_shape=jax.ShapeDtypeStruct((n, nout), F32),
        compiler_params=_cparams(("parallel",)),
        name="ada_mod",
    )(c_all, w, b.reshape(1, nout))


class _Mod:
    def __init__(self, arr, per_row, tiles_per_batch):
        self.arr, self.per_row, self.tpb = arr, per_row, tiles_per_batch

    def spec(self, k, tm, tile=lambda i: i):
        if self.per_row:
            return pl.BlockSpec((None, tm, D_MODEL), lambda i, *_: (k, tile(i), 0))
        tpb = self.tpb(tm)
        return pl.BlockSpec((None, 1, D_MODEL), lambda i, *_: ((tile(i) // tpb) * 6 + k, 0, 0))


def _modulated(x, ng, scale, shift):
    ms = jnp.mean(x * x, axis=-1, keepdims=True)
    return (x * lax.rsqrt(ms + EPS) * ng) * (1.0 + scale) + shift


def _inproj_kernel(x_ref, sh_ref, sc_ref, ng_ref, w_ref, gq_ref, gk_ref, cos_ref, sin_ref,
                   q_ref, k_ref, v_ref, *, rope, qscale):
    tm = x_ref.shape[0]
    hb = _modulated(x_ref[...], ng_ref[...], sc_ref[...], sh_ref[...]).astype(BF16)
    lane = lax.broadcasted_iota(jnp.int32, (tm, LANES), 1)
    lo = lane < DH
    swap_sel = (lane & (DH // 2)) != 0

    def norm_rope(xb, g):
        x2 = xb * xb
        slo = jnp.sum(jnp.where(lo, x2, 0.0), axis=-1, keepdims=True)
        shi = jnp.sum(jnp.where(lo, 0.0, x2), axis=-1, keepdims=True)
        ms = jnp.where(lo, slo, shi) * (1.0 / DH)
        y = xb * lax.rsqrt(ms + EPS) * g
        if rope:
            sw = jnp.where(swap_sel, pltpu.roll(y, DH // 2, 1), pltpu.roll(y, LANES - DH // 2, 1))
            y = y * cos_ref[...] + sw * sin_ref[...]
        return y

    nblk = w_ref.shape[1] // (2 * LANES)
    for j in range(nblk):
        acc = _dot(hb, w_ref[:, j * 2 * LANES:(j + 1) * 2 * LANES])
        for half in range(2):
            blk = acc[:, half * LANES:(half + 1) * LANES]
            col = j * 2 * LANES + half * LANES
            if col < D_MODEL:
                q_ref[:, col:col + LANES] = (norm_rope(blk, gq_ref[...]) * qscale).astype(BF16)
            elif col < 2 * D_MODEL:
                k_ref[:, col - D_MODEL:col - D_MODEL + LANES] = norm_rope(blk, gk_ref[...])
            else:
                v_ref[:, col - 2 * D_MODEL:col - 2 * D_MODEL + LANES] = blk


def _inproj(x, mod, ng, w_bf, gq, gk, cos, sin, *, rope, tm, qscale=DH ** -0.5):
    t = x.shape[0]
    nrep = cos.shape[0] // tm
    row = lambda i: (i, 0)
    full = lambda i: (0, 0)
    tab = lambda i: (i % nrep, 0)
    return pl.pallas_call(
        functools.partial(_inproj_kernel, rope=rope, qscale=qscale),
        grid=(t // tm,),
        in_specs=[pl.BlockSpec((tm, D_MODEL), row), mod.spec(0, tm), mod.spec(1, tm),
                  pl.BlockSpec((1, D_MODEL), full),
                  pl.BlockSpec(w_bf.shape, full),
                  pl.BlockSpec((1, LANES), full), pl.BlockSpec((1, LANES), full),
                  pl.BlockSpec((tm, LANES), tab), pl.BlockSpec((tm, LANES), tab)],
        out_specs=[pl.BlockSpec((tm, D_MODEL), row)] * 3,
        out_shape=[jax.ShapeDtypeStruct((t, D_MODEL), BF16),
                   jax.ShapeDtypeStruct((t, D_MODEL), F32),
                   jax.ShapeDtypeStruct((t, D_MODEL), F32)],
        compiler_params=_cparams(("parallel",)),
        name="qkv_proj",
    )(x, mod.arr, mod.arr, ng, w_bf, gq, gk, cos, sin)


def _cache_layout_kernel(k_ref, v_ref, k4_ref, v4_ref):
    heads = v4_ref.shape[1]
    for h in range(heads):
        kb = k_ref[:, h * LANES:(h + 1) * LANES]
        v4_ref[:, h, :] = v_ref[:, h * LANES:(h + 1) * LANES]
        for c in range(2):
            k4_ref[:, h, c, :] = kb[:, c * DH:(c + 1) * DH]


def _cache_layout(k, v, *, tm):
    t = k.shape[0]
    heads = D_MODEL // LANES
    row = lambda i: (i, 0)
    return pl.pallas_call(
        _cache_layout_kernel,
        grid=(t // tm,),
        in_specs=[pl.BlockSpec((tm, D_MODEL), row)] * 2,
        out_specs=[pl.BlockSpec((tm, heads, 2, DH), lambda i: (i, 0, 0, 0)),
                   pl.BlockSpec((tm, heads, LANES), lambda i: (i, 0, 0))],
        out_shape=[jax.ShapeDtypeStruct((t, heads, 2, DH), F32), jax.ShapeDtypeStruct((t, heads, LANES), F32)],
        compiler_params=_cparams(("parallel",)),
        name="cache_layout",
    )(k, v)


def _diff_lambda(lamv_ref, lam_init):
    lv = lamv_ref[...]
    e1 = jnp.exp(jnp.sum(lv[0:1] * lv[1:2], axis=-1, keepdims=True))
    e2 = jnp.exp(jnp.sum(lv[2:3] * lv[3:4], axis=-1, keepdims=True))
    return e1 - e2 + lam_init


def _diff_finish(o0, o1, lamv_ref, gsub_ref, lam_init):
    o = o0 - _diff_lambda(lamv_ref, lam_init) * o1
    ms = jnp.mean(o * o, axis=-1, keepdims=True)
    return ((o * lax.rsqrt(ms + EPS) * gsub_ref[...]) * (1.0 - lam_init)).astype(BF16)


def _component_queries(q):
    lane = lax.broadcasted_iota(jnp.int32, q.shape, 1)
    zero = jnp.zeros_like(q)
    return jnp.where(lane < DH, q, zero), jnp.where(lane < DH, zero, q)


def _flash_diff_kernel(qt_ref, kt_ref, q_ref, k_ref, v_ref, lamv_ref, gsub_ref, o_ref,
                       m_ref, l_ref, a_ref, *, lam_init):
    p = pl.program_id(2)
    qi = qt_ref[p]
    ki = kt_ref[p]
    tq = q_ref.shape[0]

    @pl.when(ki == 0)
    def _():
        m_ref[...] = jnp.full(m_ref.shape, NEG, F32)
        l_ref[...] = jnp.zeros(l_ref.shape, F32)
        a_ref[...] = jnp.zeros(a_ref.shape, F32)

    def update(masked):
        kb = k_ref[...].astype(BF16)
        vb = v_ref[...].astype(BF16)
        qq = jnp.concatenate(_component_queries(q_ref[...]), axis=0)
        for r in range(2 * tq // FLASH_ROWS):
            rows = slice(r * FLASH_ROWS, (r + 1) * FLASH_ROWS)
            q0 = (r * FLASH_ROWS) % tq
            ncol = min(q0 + FLASH_ROWS, tq) if masked else tq
            s = _dot_nt(qq[rows], kb[:ncol])
            if masked:
                row_c = ((lax.broadcasted_iota(jnp.int32, (FLASH_ROWS, ncol), 0) + q0) % tq) // CHUNK
                col_c = lax.broadcasted_iota(jnp.int32, (FLASH_ROWS, ncol), 1) // CHUNK
                s = jnp.where(col_c <= row_c, s, NEG)
            m_prev = m_ref[rows, :]
            m_new = jnp.maximum(m_prev, jnp.max(s, axis=-1, keepdims=True))
            alpha = jnp.exp2(m_prev - m_new)
            pm = jnp.exp2(s - jnp.tile(m_new, (1, ncol // LANES)))
            l_ref[rows, :] = alpha * l_ref[rows, :] + jnp.sum(pm, axis=-1, keepdims=True)
            a_ref[rows, :] = alpha * a_ref[rows, :] + _dot(pm.astype(BF16), vb[:ncol])
            m_ref[rows, :] = m_new

    @pl.when(ki < qi)
    def _():
        update(False)

    @pl.when(ki == qi)
    def _():
        update(True)
        o = a_ref[...] / l_ref[...]
        o_ref[...] = _diff_finish(o[:tq], o[tq:], lamv_ref, gsub_ref, lam_init)


def _flash_diff(q, k, v, lamv, gsub, *, batch, seq, lam_init):
    tq = min(TQ_FLASH, seq)
    nq = seq // tq
    heads = D_MODEL // LANES
    pairs = [(a, b) for a in range(nq) for b in range(a + 1)]
    qt = jnp.asarray([a for a, _ in pairs], jnp.int32)
    kt = jnp.asarray([b for _, b in pairs], jnp.int32)
    qmap = lambda b, h, p, qt, kt: (b * nq + qt[p], h)
    kmap = lambda b, h, p, qt, kt: (b * nq + kt[p], h)
    return pl.pallas_call(
        functools.partial(_flash_diff_kernel, lam_init=lam_init),
        grid_spec=pltpu.PrefetchScalarGridSpec(
            num_scalar_prefetch=2,
            grid=(batch, heads, len(pairs)),
            in_specs=[pl.BlockSpec((tq, LANES), qmap),
                      pl.BlockSpec((tq, LANES), kmap),
                      pl.BlockSpec((tq, LANES), kmap),
                      pl.BlockSpec(lamv.shape, lambda *_: (0, 0)),
                      pl.BlockSpec((1, LANES), lambda *_: (0, 0))],
            out_specs=pl.BlockSpec((tq, LANES), qmap),
            scratch_shapes=[pltpu.VMEM((2 * tq, LANES), F32)] * 3),
        out_shape=jax.ShapeDtypeStruct(q.shape, BF16),
        compiler_params=_cparams(("parallel", "parallel", "arbitrary")),
        name="flash_diff_attn",
    )(qt, kt, q, k, v, lamv, gsub)


def _joint_softmax_attend(qc, pieces):
    ss = []
    for kb, _, bias, visible in pieces:
        s = _dot_nt(qc, kb)
        if bias is not None:
            s = s + bias
        if visible is not None:
            s = jnp.where(visible, s, NEG)
        ss.append(s)
    m = functools.reduce(jnp.maximum, [jnp.max(s, axis=-1, keepdims=True) for s in ss])
    l = 0.0
    o = 0.0
    for s, (_, vb, _, _) in zip(ss, pieces):
        pm = jnp.exp(s - m)
        l = l + jnp.sum(pm, axis=-1, keepdims=True)
        o = o + _dot(pm.astype(BF16), vb)
    return o / l


def _diff_sample_kernel(q_ref, ck_ref, cv_ref, nk_ref, nv_ref, lamv_ref, gsub_ref, o_ref, *, lam_init):
    t_new = q_ref.shape[0]
    pieces = [(ck_ref[...].astype(BF16), cv_ref[...].astype(BF16), None, None),
              (nk_ref[...].astype(BF16), nv_ref[...].astype(BF16), None, None)]
    o = _joint_softmax_attend(jnp.concatenate(_component_queries(q_ref[...]), axis=0), pieces)
    o_ref[...] = _diff_finish(o[:t_new], o[t_new:], lamv_ref, gsub_ref, lam_init)


def _diff_sample(q, ck, cv, nk, nv, lamv, gsub, *, batch, t_new, past, lam_init):
    heads = D_MODEL // LANES
    bh = lambda b, h: (b, h)
    return pl.pallas_call(
        functools.partial(_diff_sample_kernel, lam_init=lam_init),
        grid=(batch, heads),
        in_specs=[pl.BlockSpec((t_new, LANES), bh),
                  pl.BlockSpec((past, LANES), bh), pl.BlockSpec((past, LANES), bh),
                  pl.BlockSpec((t_new, LANES), bh), pl.BlockSpec((t_new, LANES), bh),
                  pl.BlockSpec(lamv.shape, lambda b, h: (0, 0)),
                  pl.BlockSpec((1, LANES), lambda b, h: (0, 0))],
        out_specs=pl.BlockSpec((t_new, LANES), bh),
        out_shape=jax.ShapeDtypeStruct(q.shape, BF16),
        compiler_params=_cparams(("parallel", "parallel")),
        name="diff_attn_sample",
    )(q, ck, cv, nk, nv, lamv, gsub)


def _band_finish(o):
    tq = o.shape[0] // 2
    lane = lax.broadcasted_iota(jnp.int32, (tq, LANES), 1)
    return jnp.where(lane < DH, o[:tq], o[tq:]).astype(BF16)


def _band_prompt_kernel(q_ref, *refs):
    nkv = BAND_QTILES + 2
    k_refs, v_refs, bias_ref, o_ref = refs[:nkv], refs[nkv:2 * nkv], refs[2 * nkv], refs[2 * nkv + 1]
    first = pl.program_id(2) * BAND_QTILES
    tq = q_ref.shape[0] // BAND_QTILES
    row_c = (lax.broadcasted_iota(jnp.int32, (2 * tq, tq), 0) % tq) // CHUNK
    col_c = lax.broadcasted_iota(jnp.int32, (2 * tq, tq), 1) // CHUNK
    kvs = [(k_ref[...].astype(BF16), v_ref[...].astype(BF16)) for k_ref, v_ref in zip(k_refs, v_refs)]
    for sub in range(BAND_QTILES):
        pieces = []
        for r in range(3):
            before_start = jnp.where(first + sub + r - 2 < 0, 4 * PREV_CHUNKS, 0)
            kc = col_c + (r - 2) * (tq // CHUNK) + before_start
            visible = (kc <= row_c) & (kc >= row_c - PREV_CHUNKS)
            pieces.append((*kvs[sub + r], bias_ref[:, r * tq:(r + 1) * tq], visible))
        rows = slice(sub * tq, (sub + 1) * tq)
        qq = jnp.concatenate(_component_queries(q_ref[rows, :]), axis=0)
        o_ref[rows, :] = _band_finish(_joint_softmax_attend(qq, pieces))


def _band_bias_tiles(table, nq, nk, c0):
    n = nq + nk - 1
    diag = jnp.clip(c0 + nq - 1 - np.arange(n), -REL_CLIP, REL_CLIP) + REL_CLIP
    e = table[:, diag]
    a = jnp.tile(e, (1, nq + 1))[:, :nq * (n + 1)].reshape(-1, nq, n + 1)[:, ::-1, :nk]
    return a.reshape(table.shape[0] // 2, 2 * nq, nk)


def _band_prompt(q, k, v, table, *, batch, seq):
    tq = TQ_BAND
    nsub = BAND_QTILES
    assert BAND_PAST == 2 * tq and seq % (nsub * tq) == 0
    nq = seq // tq
    hp = D_MODEL // LANES
    bias = _band_bias_tiles(table, tq, 3 * tq, 2 * tq)
    qmap = lambda h, b, i: (b * (nq // nsub) + i, h)
    kmap = lambda r: (lambda h, b, i: (b * nq + jnp.maximum(nsub * i + r - 2, 0), h))
    kv_specs = [pl.BlockSpec((tq, LANES), kmap(r)) for r in range(nsub + 2)]
    return pl.pallas_call(
        _band_prompt_kernel,
        grid=(hp, batch, nq // nsub),
        in_specs=[pl.BlockSpec((nsub * tq, LANES), qmap)] + kv_specs * 2
                 + [pl.BlockSpec((None, 2 * tq, 3 * tq), lambda h, b, i: (h, 0, 0))],
        out_specs=pl.BlockSpec((nsub * tq, LANES), qmap),
        out_shape=jax.ShapeDtypeStruct(q.shape, BF16),
        compiler_params=_cparams(("parallel", "parallel", "parallel")),
        name="band_attn_prompt",
    )(q, *([k] * (nsub + 2)), *([v] * (nsub + 2)), bias)


def _band_sample_kernel(q_ref, ck_ref, cv_ref, nk_ref, nv_ref, bc_ref, bn_ref, o_ref):
    pieces = [(ck_ref[...].astype(BF16), cv_ref[...].astype(BF16), bc_ref[...], None),
              (nk_ref[...].astype(BF16), nv_ref[...].astype(BF16), bn_ref[...], None)]
    qq = jnp.concatenate(_component_queries(q_ref[...]), axis=0)
    o_ref[...] = _band_finish(_joint_softmax_attend(qq, pieces))


def _band_sample(q, ck, cv, nk, nv, table, *, batch, t_new, past):
    hp = D_MODEL // LANES
    bias = _band_bias_tiles(table, t_new, past + t_new, past)
    bias_c, bias_n = bias[..., :past], bias[..., past:]
    hb = lambda h, b: (b, h)
    return pl.pallas_call(
        _band_sample_kernel,
        grid=(hp, batch),
        in_specs=[pl.BlockSpec((t_new, LANES), hb),
                  pl.BlockSpec((past, LANES), hb), pl.BlockSpec((past, LANES), hb),
                  pl.BlockSpec((t_new, LANES), hb), pl.BlockSpec((t_new, LANES), hb),
                  pl.BlockSpec((None, 2 * t_new, past), lambda h, b: (h, 0, 0)),
                  pl.BlockSpec((None, 2 * t_new, t_new), lambda h, b: (h, 0, 0))],
        out_specs=pl.BlockSpec((t_new, LANES), hb),
        out_shape=jax.ShapeDtypeStruct(q.shape, BF16),
        compiler_params=_cparams(("parallel", "parallel")),
        name="band_attn_sample",
    )(q, ck, cv, nk, nv, bias_c, bias_n)


def _outproj_kernel(o_ref, w_ref, x_ref, gate_ref, y_ref):
    y_ref[...] = x_ref[...] + gate_ref[...] * _dot(o_ref[...], w_ref[...])


def _outproj(o, w_bf, x, mod, *, tm):
    t = x.shape[0]
    row = lambda i: (i, 0)
    return pl.pallas_call(
        _outproj_kernel,
        grid=(t // tm,),
        in_specs=[pl.BlockSpec((tm, D_MODEL), row), pl.BlockSpec(w_bf.shape, lambda i: (0, 0)),
                  pl.BlockSpec((tm, D_MODEL), row), mod.spec(2, tm)],
        out_specs=pl.BlockSpec((tm, D_MODEL), row),
        out_shape=jax.ShapeDtypeStruct(x.shape, F32),
        compiler_params=_cparams(("parallel",)),
        name="out_proj",
    )(o, w_bf, x, mod.arr)


def _top_rows(s, k):
    if s.shape[1] > LANES:
        cols = [_top_rows(s[:, c:c + LANES], k) for c in range(0, s.shape[1], LANES)]
        return tuple(jnp.concatenate(x, axis=1) for x in zip(*cols))
    rows = s.shape[0]
    rid = lax.broadcasted_iota(jnp.int32, s.shape, 0).astype(F32)
    vals, ids = [], []
    for r in range(k):
        cand = [(s[v:v + _SUB], rid[v:v + _SUB]) for v in range(0, rows, _SUB)]
        while len(cand) > 1:
            nxt = []
            for a in range(0, len(cand) - 1, 2):
                (va, ia), (vb, ib) = cand[a], cand[a + 1]
                keep = va >= vb
                nxt.append((jnp.where(keep, va, vb), jnp.where(keep, ia, ib)))
            cand = nxt + cand[len(cand) - len(cand) % 2:]
        v8, i8 = cand[0]
        m = jnp.max(v8, axis=0, keepdims=True)
        i = jnp.min(jnp.where(v8 == m, i8, float(rows)), axis=0, keepdims=True)
        vals.append(m)
        ids.append(i)
        if r + 1 < k:
            s = jnp.where(rid == i, -jnp.inf, s)
    return jnp.concatenate(vals, axis=0), jnp.concatenate(ids, axis=0)


_SUB = 8
_STAIR_PIECES = ([(0, 1, 0, _SUB), (0, 1, _SUB, _SUB), (1, 1, 0, _SUB)]
                 + [(a, 1, 0, TOPK // (a + 1)) for a in range(2, _SUB)] + [(_SUB, _SUB, 0, 1)])


def _stair_candidates(s1, s2):
    sub = lax.broadcasted_iota(jnp.int32, (_SUB, s1.shape[1]), 0)
    pieces = []
    for a0, na, b0, nb in _STAIR_PIECES:
        if na == 1:
            piece = s1[a0:a0 + 1, :] + s2[b0:b0 + _SUB, :]
            if nb < _SUB:
                piece = jnp.where(sub < nb, piece, -jnp.inf)
        else:
            piece = s1[a0:a0 + na, :] + s2[b0:b0 + 1, :]
        pieces.append(piece)
    return jnp.concatenate(pieces, axis=0)


def _stair_ranks(pos):
    a = jnp.zeros(pos.shape, F32)
    b = pos
    for p, (a0, na, b0, nb) in enumerate(_STAIR_PIECES):
        start = float(p * _SUB)
        inside = pos >= start
        if na == 1:
            a = jnp.where(inside, float(a0), a)
            b = jnp.where(inside, pos - start + float(b0), b)
        else:
            a = jnp.where(inside, pos - start + float(a0), a)
            b = jnp.where(inside, float(b0), b)
    return a, b


def _pick_rows(sel, table):
    out = jnp.zeros(sel.shape, F32)
    for a in range(table.shape[0]):
        out = out + jnp.where(sel == float(a), table[a:a + 1, :], 0.0)
    return out


def _select_unit(q_scr, keys_ref, sel_t_scr, h, part):
    tokens = pl.ds(pl.multiple_of(part * LANES, LANES), LANES)
    top = [_top_rows(_dot_nt(keys_ref[2 * h + c], q_scr[2 * h + c, tokens, :]), TOPK) for c in range(2)]
    (s1, i1), (s2, i2) = top
    top_s, pos = _top_rows(_stair_candidates(s1, s2), TOPK)
    a_sel, b_sel = _stair_ranks(pos)
    e = jnp.exp(top_s - top_s[0:1, :])
    rows = pl.ds(pl.multiple_of(h * TOPK, TOPK), TOPK)
    sel_t_scr[0, part, rows, :] = _pick_rows(a_sel, i1)
    sel_t_scr[1, part, rows, :] = _pick_rows(b_sel, i2)
    sel_t_scr[2, part, rows, :] = e / jnp.sum(e, axis=0, keepdims=True)


def _peer_kernel(xs_ref, sh_ref, sc_ref, ng_ref, wq_ref, keys_ref, u_ref, v_ref, xr_ref, gate_ref, y_ref,
                 hb_scr, q_scr, sel_t_scr, sel_scr, w3_ref, acc_ref):
    i = pl.program_id(0)
    j = pl.program_id(1)
    tm = xs_ref.shape[0]
    nparts = tm // LANES
    half = N_KEYS // 2
    pitch = w3_ref.shape[0] // half
    e_tile = u_ref.shape[0]
    slot_new, slot_dense = i % 2, (i + 1) % 2

    @pl.when((i == 0) & (j == 0))
    def _():
        w3_ref[...] = jnp.zeros(w3_ref.shape, w3_ref.dtype)
        hb_scr[1] = jnp.zeros(hb_scr.shape[1:], hb_scr.dtype)

    @pl.when((i > 0) & (j == 0))
    def _():
        for k in range(3):
            for part in range(nparts):
                sel_scr[k, part * LANES:(part + 1) * LANES, :] = sel_t_scr[k, part].T
        row = lax.broadcasted_iota(jnp.int32, (N_KEYS, N_KEYS), 0)
        i1_ids = jnp.where(row < half, 2 * row, 2 * (row - half) + 1).astype(F32)
        i2_ids = row.astype(F32)

        def token(t, carry):
            arow = sel_scr[0, pl.ds(t, 1), :]
            brow = sel_scr[1, pl.ds(t, 1), :]
            grow = sel_scr[2, pl.ds(t, 1), :]
            oa = jnp.where(arow == i1_ids, 1.0, 0.0).astype(BF16)
            ob = jnp.where(brow == i2_ids, 0.5 * grow, 0.0).astype(BF16)
            w = _dot_nt(oa, ob)
            w3_ref[pl.ds(t, half, stride=pitch), :] = pltpu.pack_elementwise([w[:half], w[half:]], packed_dtype=BF16)
            return carry

        lax.fori_loop(0, tm, token, 0, unroll=TOKEN_UNROLL)

    @pl.when(j == 0)
    def _():
        acc_ref[...] = jnp.zeros(acc_ref.shape, F32)
        hb = _modulated(xs_ref[...], ng_ref[...], sc_ref[...], sh_ref[...]).astype(BF16)
        hb_scr[slot_new] = hb
        q_all = _dot(hb, wq_ref[...])
        for hc in range(2 * PEER_HEADS):
            q_scr[hc] = q_all[:, hc * N_KEYS:(hc + 1) * N_KEYS].astype(BF16)

    _select_unit(q_scr, keys_ref, sel_t_scr, j // nparts, j % nparts)

    cols = []
    for q in range(e_tile // (2 * N_KEYS)):
        word = w3_ref[pl.ds(pl.multiple_of((e_tile // (2 * N_KEYS) * j + q) * pitch, 8), tm), :]
        cols += [pltpu.unpack_elementwise(word, index=k, packed_dtype=BF16, unpacked_dtype=F32) for k in range(2)]
    w = jnp.concatenate(cols, axis=1)
    hid = _dot_nt(hb_scr[slot_dense], u_ref[...])
    act = hid * (1.0 + lax.erf(hid * math.sqrt(0.5)))
    acc_ref[...] += _dot((w * act).astype(BF16), v_ref[...])

    @pl.when((i > 0) & (j == pl.num_programs(1) - 1))
    def _():
        y_ref[...] = xr_ref[...] + gate_ref[...] * acc_ref[...]


def _peer(x, mod, ng, wq_bf, keys_bf, u_bf, v_bf, *, tm):
    t = x.shape[0]
    nt = t // tm
    units = PEER_HEADS * (tm // LANES)
    e_tile = u_bf.shape[0] // units
    assert e_tile % (2 * N_KEYS) == 0 and tm % TOKEN_UNROLL == 0
    new_tile = lambda i: jnp.minimum(i, nt - 1)
    dense_tile = lambda i: jnp.maximum(i - 1, 0)
    full = lambda i, j: (0, 0)
    return pl.pallas_call(
        _peer_kernel,
        grid=(nt + 1, units),
        in_specs=[pl.BlockSpec((tm, D_MODEL), lambda i, j: (new_tile(i), 0)),
                  mod.spec(3, tm, new_tile), mod.spec(4, tm, new_tile),
                  pl.BlockSpec((1, D_MODEL), full),
                  pl.BlockSpec(wq_bf.shape, full),
                  pl.BlockSpec(keys_bf.shape, lambda i, j: (0, 0, 0)),
                  pl.BlockSpec((e_tile, D_MODEL), lambda i, j: (j, 0)),
                  pl.BlockSpec((e_tile, D_MODEL), lambda i, j: (j, 0)),
                  pl.BlockSpec((tm, D_MODEL), lambda i, j: (dense_tile(i), 0)),
                  mod.spec(5, tm, dense_tile)],
        out_specs=pl.BlockSpec((tm, D_MODEL), lambda i, j: (dense_tile(i), 0)),
        out_shape=jax.ShapeDtypeStruct(x.shape, F32),
        scratch_shapes=[pltpu.VMEM((2, tm, D_MODEL), BF16),
                        pltpu.VMEM((2 * PEER_HEADS, tm, N_KEYS), BF16),
                        pltpu.VMEM((3, tm // LANES, PEER_HEADS * TOPK, LANES), F32),
                        pltpu.VMEM((3, tm, PEER_HEADS * TOPK), F32),
                        pltpu.VMEM((N_KEYS // 2 * (tm + W3_PAD), N_KEYS), jnp.int32),
                        pltpu.VMEM((tm, D_MODEL), F32)],
        compiler_params=_cparams(("arbitrary", "arbitrary")),
        name="peer",
    )(x, mod.arr, mod.arr, ng, wq_bf, keys_bf, u_bf, v_bf, x, mod.arr)


def _rope_tables(pos):
    half = DH // 2
    inv = ROPE_THETA ** (-jnp.arange(half, dtype=F32) / half)
    ang = pos.astype(F32)[:, None] * inv[None, :]
    cos, sin = jnp.cos(ang), jnp.sin(ang)
    return jnp.tile(cos, (1, 4)), jnp.concatenate([-sin, sin, -sin, sin], axis=1)


def _pair_tile(g):
    return jnp.tile(g.reshape(1, -1), (1, LANES // g.shape[-1]))


def kernel(x_prompt, x_sample, c_prompt, c_sample, cache_a_k, cache_a_v, cache_b_k, cache_b_v, ada_w, ada_b, norm_g, a_w_in, a_g_q, a_g_k, a_lq1, a_lk1, a_lq2, a_lk2, a_g_sub, a_w_out, b_w_in, b_g_q, b_g_k, b_rel_bias, b_w_out, peer_w_q, peer_sub_keys, peer_u, peer_v):
    batch, seq, _ = x_prompt.shape
    dbatch, t_new, _ = x_sample.shape
    past_a = cache_a_k.shape[2]
    past_b = cache_b_k.shape[2]
    depth = ada_w.shape[0]
    tp, ts = batch * seq, dbatch * t_new
    tm_s = min(TM_PROJ, ts)
    tmd_s = min(TM_DENSE // 2, ts)

    xp = x_prompt.reshape(tp, D_MODEL)
    xs = x_sample.reshape(ts, D_MODEL)
    c_all = jnp.concatenate([c_prompt, c_sample], axis=0)

    cos_p, sin_p = _rope_tables(jnp.arange(seq))
    cos_s, sin_s = _rope_tables(past_a + jnp.arange(t_new))
    cos_s, sin_s = jnp.tile(cos_s, (tm_s // t_new, 1)), jnp.tile(sin_s, (tm_s // t_new, 1))

    outs = {n: [] for n in ("akp", "avp", "aks", "avs", "bkp", "bvp", "bks", "bvs")}
    for i in range(depth):
        j = i // 2
        mod = _ada_mod(c_all, ada_w[i], ada_b[i])
        mod_p = _Mod(mod[:batch].reshape(batch * 6, 1, D_MODEL), False, lambda tm: seq // tm)
        mod_s = _Mod(jnp.repeat(mod[batch:].reshape(dbatch, 6, D_MODEL).transpose(1, 0, 2), t_new, axis=1),
                     True, None)
        ng0, ng1 = norm_g[i, 0].reshape(1, -1), norm_g[i, 1].reshape(1, -1)

        if i % 2 == 0:
            lam_init = 0.8 - 0.6 * math.exp(-0.3 * i)
            w_in = a_w_in[j].astype(BF16)
            gq, gk = _pair_tile(a_g_q[j]), _pair_tile(a_g_k[j])
            lamv = jnp.stack([a_lq1[j], a_lk1[j], a_lq2[j], a_lk2[j]])
            gsub = a_g_sub[j].reshape(1, -1)
            qp, kp, vp = _inproj(xp, mod_p, ng0, w_in, gq, gk, cos_p, sin_p, rope=True, tm=TM_PROJ,
                                 qscale=DH ** -0.5 * _LOG2E)
            qs, ks, vs = _inproj(xs, mod_s, ng0, w_in, gq, gk, cos_s, sin_s, rope=True, tm=tm_s)
            op = _flash_diff(qp, kp, vp, lamv, gsub, batch=batch, seq=seq, lam_init=lam_init)
            os_ = _diff_sample(qs, cache_a_k[j].reshape(dbatch * past_a, D_MODEL),
                               cache_a_v[j].reshape(dbatch * past_a, D_MODEL), ks, vs, lamv, gsub,
                               batch=dbatch, t_new=t_new, past=past_a, lam_init=lam_init)
            w_out = a_w_out[j].astype(BF16)
            heads = D_MODEL // LANES
            k4p, v4p = _cache_layout(kp, vp, tm=TM_PROJ)
            k4s, v4s = _cache_layout(ks, vs, tm=tm_s)
            outs["akp"].append(k4p.reshape(batch, seq, heads, 2, DH))
            outs["avp"].append(v4p.reshape(batch, seq, heads, 2 * DH))
            outs["aks"].append(k4s.reshape(dbatch, t_new, heads, 2, DH))
            outs["avs"].append(v4s.reshape(dbatch, t_new, heads, 2 * DH))
        else:
            w_in = b_w_in[j].astype(BF16)
            gq, gk = _pair_tile(b_g_q[j]), _pair_tile(b_g_k[j])
            qp, kp, vp = _inproj(xp, mod_p, ng0, w_in, gq, gk, cos_p, sin_p, rope=False, tm=TM_PROJ)
            qs, ks, vs = _inproj(xs, mod_s, ng0, w_in, gq, gk, cos_s, sin_s, rope=False, tm=tm_s)
            op = _band_prompt(qp, kp, vp, b_rel_bias[j], batch=batch, seq=seq)
            os_ = _band_sample(qs, cache_b_k[j].reshape(dbatch * past_b, D_MODEL),
                               cache_b_v[j].reshape(dbatch * past_b, D_MODEL), ks, vs, b_rel_bias[j],
                               batch=dbatch, t_new=t_new, past=past_b)
            w_out = b_w_out[j].astype(BF16)
            heads = D_MODEL // DH
            keep = min(BAND_PAST, seq)
            k4 = kp.reshape(batch, seq, heads, DH)
            v4 = vp.reshape(batch, seq, heads, DH)
            outs["bkp"].append(k4[:, seq - keep:])
            outs["bvp"].append(v4[:, seq - keep:])
            outs["bks"].append(jnp.concatenate([cache_b_k[j], ks.reshape(dbatch, t_new, heads, DH)], axis=1)[:, t_new:])
            outs["bvs"].append(jnp.concatenate([cache_b_v[j], vs.reshape(dbatch, t_new, heads, DH)], axis=1)[:, t_new:])

        xp = _outproj(op, w_out, xp, mod_p, tm=TM_PROJ)
        xs = _outproj(os_, w_out, xs, mod_s, tm=tm_s)

        wq = peer_w_q[i].astype(BF16)
        keys = peer_sub_keys[i].astype(BF16).reshape(PEER_HEADS * 2, N_KEYS, -1)
        u_bf, v_bf = peer_u[i].astype(BF16), peer_v[i].astype(BF16)
        xp = _peer(xp, mod_p, ng1, wq, keys, u_bf, v_bf, tm=TM_DENSE)
        xs = _peer(xs, mod_s, ng1, wq, keys, u_bf, v_bf, tm=tmd_s)

    st = lambda n: jnp.stack(outs[n], 0)
    return (xp.reshape(x_prompt.shape), xs.reshape(x_sample.shape),
            st("akp"), st("avp"), st("aks"), st("avs"), st("bkp"), st("bvp"), st("bks"), st("bvs"))
```

```python
import functools
import math

import numpy as np
import jax
import jax.numpy as jnp
from jax import lax
from jax.experimental import pallas as pl
from jax.experimental.pallas import tpu as pltpu

F32 = jnp.float32
BF16 = jnp.bfloat16

D_MODEL = 1024
CHUNK = 64
EPS = 1e-6
NEG = -1e30
ROPE_THETA = 10000.0
DH = 64
LANES = 128
PREV_CHUNKS = 8
BAND_PAST = PREV_CHUNKS * CHUNK
REL_CLIP = 128
PEER_HEADS = 8
N_KEYS = 128
TOPK = 16
VMEM_LIMIT = 48 * 1024 * 1024

TM_PROJ = 512
TQ_FLASH = 1024
FLASH_ROWS = 1024
TQ_BAND = 256
BAND_QTILES = 4
TM_DENSE = 512
TOKEN_UNROLL = 64
W3_PAD = 8

_NT = (((1,), (1,)), ((), ()))
_LOG2E = math.log2(math.e)


def _cparams(sem):
    return pltpu.CompilerParams(dimension_semantics=sem, vmem_limit_bytes=VMEM_LIMIT)


def _dot(a, b):
    return jnp.dot(a, b, preferred_element_type=F32)


def _dot_nt(a, b):
    return lax.dot_general(a, b, _NT, preferred_element_type=F32)


def _split(a):
    hi = a.astype(BF16)
    lo = (a - hi.astype(F32)).astype(BF16)
    return hi, lo


def _ada_kernel(c_ref, w_ref, b_ref, o_ref):
    c = c_ref[...]
    a = c * (1.0 / (1.0 + jnp.exp(-c)))
    ah, al = _split(a)
    wh, wl = _split(w_ref[...])
    o_ref[...] = _dot(ah, wh) + _dot(al, wh) + _dot(ah, wl) + b_ref[...]


def _ada_mod(c_all, w, b):
    n, d = c_all.shape
    nout = w.shape[1]
    tn = 512
    return pl.pallas_call(
        _ada_kernel,
        grid=(nout // tn,),
        in_specs=[pl.BlockSpec((n, d), lambda j: (0, 0)),
                  pl.BlockSpec((d, tn), lambda j: (0, j)),
                  pl.BlockSpec((1, tn), lambda j: (0, j))],
        out_specs=pl.BlockSpec((n, tn), lambda j: (0, j)),
        out_shape=jax.ShapeDtypeStruct((n, nout), F32),
        compiler_params=_cparams(("parallel",)),
        name="ada_mod",
    )(c_all, w, b.reshape(1, nout))


class _Mod:
    def __init__(self, arr, per_row, tiles_per_batch):
        self.arr, self.per_row, self.tpb = arr, per_row, tiles_per_batch

    def spec(self, k, tm, tile=lambda i: i):
        if self.per_row:
            return pl.BlockSpec((None, tm, D_MODEL), lambda i, *_: (k, tile(i), 0))
        tpb = self.tpb(tm)
        return pl.BlockSpec((None, 1, D_MODEL), lambda i, *_: ((tile(i) // tpb) * 6 + k, 0, 0))


def _modulated(x, ng, scale, shift):
    ms = jnp.mean(x * x, axis=-1, keepdims=True)
    return (x * lax.rsqrt(ms + EPS) * ng) * (1.0 + scale) + shift


def _inproj_kernel(x_ref, sh_ref, sc_ref, ng_ref, w_ref, gq_ref, gk_ref, cos_ref, sin_ref,
                   q_ref, k_ref, v_ref, *, rope, qscale):
    tm = x_ref.shape[0]
    hb = _modulated(x_ref[...], ng_ref[...], sc_ref[...], sh_ref[...]).astype(BF16)
    lane = lax.broadcasted_iota(jnp.int32, (tm, LANES), 1)
    lo = lane < DH
    swap_sel = (lane & (DH // 2)) != 0

    def norm_rope(xb, g):
        x2 = xb * xb
        slo = jnp.sum(jnp.where(lo, x2, 0.0), axis=-1, keepdims=True)
        shi = jnp.sum(jnp.where(lo, 0.0, x2), axis=-1, keepdims=True)
        ms = jnp.where(lo, slo, shi) * (1.0 / DH)
        y = xb * lax.rsqrt(ms + EPS) * g
        if rope:
            sw = jnp.where(swap_sel, pltpu.roll(y, DH // 2, 1), pltpu.roll(y, LANES - DH // 2, 1))
            y = y * cos_ref[...] + sw * sin_ref[...]
        return y

    nblk = w_ref.shape[1] // (2 * LANES)
    for j in range(nblk):
        acc = _dot(hb, w_ref[:, j * 2 * LANES:(j + 1) * 2 * LANES])
        for half in range(2):
            blk = acc[:, half * LANES:(half + 1) * LANES]
            col = j * 2 * LANES + half * LANES
            if col < D_MODEL:
                q_ref[:, col:col + LANES] = (norm_rope(blk, gq_ref[...]) * qscale).astype(BF16)
            elif col < 2 * D_MODEL:
                k_ref[:, col - D_MODEL:col - D_MODEL + LANES] = norm_rope(blk, gk_ref[...])
            else:
                v_ref[:, col - 2 * D_MODEL:col - 2 * D_MODEL + LANES] = blk


def _inproj(x, mod, ng, w_bf, gq, gk, cos, sin, *, rope, tm, qscale=DH ** -0.5):
    t = x.shape[0]
    nrep = cos.shape[0] // tm
    row = lambda i: (i, 0)
    full = lambda i: (0, 0)
    tab = lambda i: (i % nrep, 0)
    return pl.pallas_call(
        functools.partial(_inproj_kernel, rope=rope, qscale=qscale),
        grid=(t // tm,),
        in_specs=[pl.BlockSpec((tm, D_MODEL), row), mod.spec(0, tm), mod.spec(1, tm),
                  pl.BlockSpec((1, D_MODEL), full),
                  pl.BlockSpec(w_bf.shape, full),
                  pl.BlockSpec((1, LANES), full), pl.BlockSpec((1, LANES), full),
                  pl.BlockSpec((tm, LANES), tab), pl.BlockSpec((tm, LANES), tab)],
        out_specs=[pl.BlockSpec((tm, D_MODEL), row)] * 3,
        out_shape=[jax.ShapeDtypeStruct((t, D_MODEL), BF16),
                   jax.ShapeDtypeStruct((t, D_MODEL), F32),
                   jax.ShapeDtypeStruct((t, D_MODEL), F32)],
        compiler_params=_cparams(("parallel",)),
        name="qkv_proj",
    )(x, mod.arr, mod.arr, ng, w_bf, gq, gk, cos, sin)


def _cache_layout_kernel(k_ref, v_ref, k4_ref, v4_ref):
    heads = v4_ref.shape[1]
    for h in range(heads):
        kb = k_ref[:, h * LANES:(h + 1) * LANES]
        v4_ref[:, h, :] = v_ref[:, h * LANES:(h + 1) * LANES]
        for c in range(2):
            k4_ref[:, h, c, :] = kb[:, c * DH:(c + 1) * DH]


def _cache_layout(k, v, *, tm):
    t = k.shape[0]
    heads = D_MODEL // LANES
    row = lambda i: (i, 0)
    return pl.pallas_call(
        _cache_layout_kernel,
        grid=(t // tm,),
        in_specs=[pl.BlockSpec((tm, D_MODEL), row)] * 2,
        out_specs=[pl.BlockSpec((tm, heads, 2, DH), lambda i: (i, 0, 0, 0)),
                   pl.BlockSpec((tm, heads, LANES), lambda i: (i, 0, 0))],
        out_shape=[jax.ShapeDtypeStruct((t, heads, 2, DH), F32), jax.ShapeDtypeStruct((t, heads, LANES), F32)],
        compiler_params=_cparams(("parallel",)),
        name="cache_layout",
    )(k, v)


def _diff_lambda(lamv_ref, lam_init):
    lv = lamv_ref[...]
    e1 = jnp.exp(jnp.sum(lv[0:1] * lv[1:2], axis=-1, keepdims=True))
    e2 = jnp.exp(jnp.sum(lv[2:3] * lv[3:4], axis=-1, keepdims=True))
    return e1 - e2 + lam_init


def _diff_finish(o0, o1, lamv_ref, gsub_ref, lam_init):
    o = o0 - _diff_lambda(lamv_ref, lam_init) * o1
    ms = jnp.mean(o * o, axis=-1, keepdims=True)
    return ((o * lax.rsqrt(ms + EPS) * gsub_ref[...]) * (1.0 - lam_init)).astype(BF16)


def _component_queries(q):
    lane = lax.broadcasted_iota(jnp.int32, q.shape, 1)
    zero = jnp.zeros_like(q)
    return jnp.where(lane < DH, q, zero), jnp.where(lane < DH, zero, q)


def _flash_diff_kernel(qt_ref, kt_ref, q_ref, k_ref, v_ref, lamv_ref, gsub_ref, o_ref,
                       m_ref, l_ref, a_ref, *, lam_init):
    p = pl.program_id(2)
    qi = qt_ref[p]
    ki = kt_ref[p]
    tq = q_ref.shape[0]

    @pl.when(ki == 0)
    def _():
        m_ref[...] = jnp.full(m_ref.shape, NEG, F32)
        l_ref[...] = jnp.zeros(l_ref.shape, F32)
        a_ref[...] = jnp.zeros(a_ref.shape, F32)

    def update(masked):
        kb = k_ref[...].astype(BF16)
        vb = v_ref[...].astype(BF16)
        qq = jnp.concatenate(_component_queries(q_ref[...]), axis=0)
        for r in range(2 * tq // FLASH_ROWS):
            rows = slice(r * FLASH_ROWS, (r + 1) * FLASH_ROWS)
            q0 = (r * FLASH_ROWS) % tq
            ncol = min(q0 + FLASH_ROWS, tq) if masked else tq
            s = _dot_nt(qq[rows], kb[:ncol])
            if masked:
                row_c = ((lax.broadcasted_iota(jnp.int32, (FLASH_ROWS, ncol), 0) + q0) % tq) // CHUNK
                col_c = lax.broadcasted_iota(jnp.int32, (FLASH_ROWS, ncol), 1) // CHUNK
                s = jnp.where(col_c <= row_c, s, NEG)
            m_prev = m_ref[rows, :]
            m_new = jnp.maximum(m_prev, jnp.max(s, axis=-1, keepdims=True))
            alpha = jnp.exp2(m_prev - m_new)
            pm = jnp.exp2(s - jnp.tile(m_new, (1, ncol // LANES)))
            l_ref[rows, :] = alpha * l_ref[rows, :] + jnp.sum(pm, axis=-1, keepdims=True)
            a_ref[rows, :] = alpha * a_ref[rows, :] + _dot(pm.astype(BF16), vb[:ncol])
            m_ref[rows, :] = m_new

    @pl.when(ki < qi)
    def _():
        update(False)

    @pl.when(ki == qi)
    def _():
        update(True)
        o = a_ref[...] / l_ref[...]
        o_ref[...] = _diff_finish(o[:tq], o[tq:], lamv_ref, gsub_ref, lam_init)


def _flash_diff(q, k, v, lamv, gsub, *, batch, seq, lam_init):
    tq = min(TQ_FLASH, seq)
    nq = seq // tq
    heads = D_MODEL // LANES
    pairs = [(a, b) for a in range(nq) for b in range(a + 1)]
    qt = jnp.asarray([a for a, _ in pairs], jnp.int32)
    kt = jnp.asarray([b for _, b in pairs], jnp.int32)
    qmap = lambda b, h, p, qt, kt: (b * nq + qt[p], h)
    kmap = lambda b, h, p, qt, kt: (b * nq + kt[p], h)
    return pl.pallas_call(
        functools.partial(_flash_diff_kernel, lam_init=lam_init),
        grid_spec=pltpu.PrefetchScalarGridSpec(
            num_scalar_prefetch=2,
            grid=(batch, heads, len(pairs)),
            in_specs=[pl.BlockSpec((tq, LANES), qmap),
                      pl.BlockSpec((tq, LANES), kmap),
                      pl.BlockSpec((tq, LANES), kmap),
                      pl.BlockSpec(lamv.shape, lambda *_: (0, 0)),
                      pl.BlockSpec((1, LANES), lambda *_: (0, 0))],
            out_specs=pl.BlockSpec((tq, LANES), qmap),
            scratch_shapes=[pltpu.VMEM((2 * tq, LANES), F32)] * 3),
        out_shape=jax.ShapeDtypeStruct(q.shape, BF16),
        compiler_params=_cparams(("parallel", "parallel", "arbitrary")),
        name="flash_diff_attn",
    )(qt, kt, q, k, v, lamv, gsub)


def _joint_softmax_attend(qc, pieces):
    ss = []
    for kb, _, bias, visible in pieces:
        s = _dot_nt(qc, kb)
        if bias is not None:
            s = s + bias
        if visible is not None:
            s = jnp.where(visible, s, NEG)
        ss.append(s)
    m = functools.reduce(jnp.maximum, [jnp.max(s, axis=-1, keepdims=True) for s in ss])
    l = 0.0
    o = 0.0
    for s, (_, vb, _, _) in zip(ss, pieces):
        pm = jnp.exp(s - m)
        l = l + jnp.sum(pm, axis=-1, keepdims=True)
        o = o + _dot(pm.astype(BF16), vb)
    return o / l


def _diff_sample_kernel(q_ref, ck_ref, cv_ref, nk_ref, nv_ref, lamv_ref, gsub_ref, o_ref, *, lam_init):
    t_new = q_ref.shape[0]
    pieces = [(ck_ref[...].astype(BF16), cv_ref[...].astype(BF16), None, None),
              (nk_ref[...].astype(BF16), nv_ref[...].astype(BF16), None, None)]
    o = _joint_softmax_attend(jnp.concatenate(_component_queries(q_ref[...]), axis=0), pieces)
    o_ref[...] = _diff_finish(o[:t_new], o[t_new:], lamv_ref, gsub_ref, lam_init)


def _diff_sample(q, ck, cv, nk, nv, lamv, gsub, *, batch, t_new, past, lam_init):
    heads = D_MODEL // LANES
    bh = lambda b, h: (b, h)
    return pl.pallas_call(
        functools.partial(_diff_sample_kernel, lam_init=lam_init),
        grid=(batch, heads),
        in_specs=[pl.BlockSpec((t_new, LANES), bh),
                  pl.BlockSpec((past, LANES), bh), pl.BlockSpec((past, LANES), bh),
                  pl.BlockSpec((t_new, LANES), bh), pl.BlockSpec((t_new, LANES), bh),
                  pl.BlockSpec(lamv.shape, lambda b, h: (0, 0)),
                  pl.BlockSpec((1, LANES), lambda b, h: (0, 0))],
        out_specs=pl.BlockSpec((t_new, LANES), bh),
        out_shape=jax.ShapeDtypeStruct(q.shape, BF16),
        compiler_params=_cparams(("parallel", "parallel")),
        name="diff_attn_sample",
    )(q, ck, cv, nk, nv, lamv, gsub)


def _band_finish(o):
    tq = o.shape[0] // 2
    lane = lax.broadcasted_iota(jnp.int32, (tq, LANES), 1)
    return jnp.where(lane < DH, o[:tq], o[tq:]).astype(BF16)


def _band_prompt_kernel(q_ref, *refs):
    nkv = BAND_QTILES + 2
    k_refs, v_refs, bias_ref, o_ref = refs[:nkv], refs[nkv:2 * nkv], refs[2 * nkv], refs[2 * nkv + 1]
    first = pl.program_id(2) * BAND_QTILES
    tq = q_ref.shape[0] // BAND_QTILES
    row_c = (lax.broadcasted_iota(jnp.int32, (2 * tq, tq), 0) % tq) // CHUNK
    col_c = lax.broadcasted_iota(jnp.int32, (2 * tq, tq), 1) // CHUNK
    kvs = [(k_ref[...].astype(BF16), v_ref[...].astype(BF16)) for k_ref, v_ref in zip(k_refs, v_refs)]
    for sub in range(BAND_QTILES):
        pieces = []
        for r in range(3):
            before_start = jnp.where(first + sub + r - 2 < 0, 4 * PREV_CHUNKS, 0)
            kc = col_c + (r - 2) * (tq // CHUNK) + before_start
            visible = (kc <= row_c) & (kc >= row_c - PREV_CHUNKS)
            pieces.append((*kvs[sub + r], bias_ref[:, r * tq:(r + 1) * tq], visible))
        rows = slice(sub * tq, (sub + 1) * tq)
        qq = jnp.concatenate(_component_queries(q_ref[rows, :]), axis=0)
        o_ref[rows, :] = _band_finish(_joint_softmax_attend(qq, pieces))


def _band_bias_tiles(table, nq, nk, c0):
    n = nq + nk - 1
    m = np.arange(n)
    offset = np.where(m < nk, -m, n - m)
    e = table[:, np.clip(c0 + offset, -REL_CLIP, REL_CLIP) + REL_CLIP]
    a = jnp.tile(e, (1, nq))[:, :nq * (n - 1)].reshape(-1, nq, n - 1)[:, :, :nk]
    return a.reshape(table.shape[0] // 2, 2 * nq, nk)


def _band_prompt(q, k, v, table, *, batch, seq):
    tq = TQ_BAND
    nsub = BAND_QTILES
    assert BAND_PAST == 2 * tq and seq % (nsub * tq) == 0
    nq = seq // tq
    hp = D_MODEL // LANES
    bias = _band_bias_tiles(table, tq, 3 * tq, 2 * tq)
    qmap = lambda h, b, i: (b * (nq // nsub) + i, h)
    kmap = lambda r: (lambda h, b, i: (b * nq + jnp.maximum(nsub * i + r - 2, 0), h))
    kv_specs = [pl.BlockSpec((tq, LANES), kmap(r)) for r in range(nsub + 2)]
    return pl.pallas_call(
        _band_prompt_kernel,
        grid=(hp, batch, nq // nsub),
        in_specs=[pl.BlockSpec((nsub * tq, LANES), qmap)] + kv_specs * 2
                 + [pl.BlockSpec((None, 2 * tq, 3 * tq), lambda h, b, i: (h, 0, 0))],
        out_specs=pl.BlockSpec((nsub * tq, LANES), qmap),
        out_shape=jax.ShapeDtypeStruct(q.shape, BF16),
        compiler_params=_cparams(("parallel", "parallel", "parallel")),
        name="band_attn_prompt",
    )(q, *([k] * (nsub + 2)), *([v] * (nsub + 2)), bias)


def _band_sample_kernel(q_ref, ck_ref, cv_ref, nk_ref, nv_ref, bc_ref, bn_ref, o_ref):
    pieces = [(ck_ref[...].astype(BF16), cv_ref[...].astype(BF16), bc_ref[...], None),
              (nk_ref[...].astype(BF16), nv_ref[...].astype(BF16), bn_ref[...], None)]
    qq = jnp.concatenate(_component_queries(q_ref[...]), axis=0)
    o_ref[...] = _band_finish(_joint_softmax_attend(qq, pieces))


def _band_sample(q, ck, cv, nk, nv, table, *, batch, t_new, past):
    hp = D_MODEL // LANES
    bias = _band_bias_tiles(table, t_new, past + t_new, past)
    bias_c, bias_n = bias[..., :past], bias[..., past:]
    hb = lambda h, b: (b, h)
    return pl.pallas_call(
        _band_sample_kernel,
        grid=(hp, batch),
        in_specs=[pl.BlockSpec((t_new, LANES), hb),
                  pl.BlockSpec((past, LANES), hb), pl.BlockSpec((past, LANES), hb),
                  pl.BlockSpec((t_new, LANES), hb), pl.BlockSpec((t_new, LANES), hb),
                  pl.BlockSpec((None, 2 * t_new, past), lambda h, b: (h, 0, 0)),
                  pl.BlockSpec((None, 2 * t_new, t_new), lambda h, b: (h, 0, 0))],
        out_specs=pl.BlockSpec((t_new, LANES), hb),
        out_shape=jax.ShapeDtypeStruct(q.shape, BF16),
        compiler_params=_cparams(("parallel", "parallel")),
        name="band_attn_sample",
    )(q, ck, cv, nk, nv, bias_c, bias_n)


def _outproj_kernel(o_ref, w_ref, x_ref, gate_ref, y_ref):
    y_ref[...] = x_ref[...] + gate_ref[...] * _dot(o_ref[...], w_ref[...])


def _outproj(o, w_bf, x, mod, *, tm):
    t = x.shape[0]
    row = lambda i: (i, 0)
    return pl.pallas_call(
        _outproj_kernel,
        grid=(t // tm,),
        in_specs=[pl.BlockSpec((tm, D_MODEL), row), pl.BlockSpec(w_bf.shape, lambda i: (0, 0)),
                  pl.BlockSpec((tm, D_MODEL), row), mod.spec(2, tm)],
        out_specs=pl.BlockSpec((tm, D_MODEL), row),
        out_shape=jax.ShapeDtypeStruct(x.shape, F32),
        compiler_params=_cparams(("parallel",)),
        name="out_proj",
    )(o, w_bf, x, mod.arr)


def _top_rows(s, k):
    if s.shape[1] > LANES:
        cols = [_top_rows(s[:, c:c + LANES], k) for c in range(0, s.shape[1], LANES)]
        return tuple(jnp.concatenate(x, axis=1) for x in zip(*cols))
    rows = s.shape[0]
    rid = lax.broadcasted_iota(jnp.int32, s.shape, 0).astype(F32)
    vals, ids = [], []
    for r in range(k):
        cand = [(s[v:v + _SUB], rid[v:v + _SUB]) for v in range(0, rows, _SUB)]
        while len(cand) > 1:
            nxt = []
            for a in range(0, len(cand) - 1, 2):
                (va, ia), (vb, ib) = cand[a], cand[a + 1]
                keep = va >= vb
                nxt.append((jnp.where(keep, va, vb), jnp.where(keep, ia, ib)))
            cand = nxt + cand[len(cand) - len(cand) % 2:]
        v8, i8 = cand[0]
        m = jnp.max(v8, axis=0, keepdims=True)
        i = jnp.min(jnp.where(v8 == m, i8, float(rows)), axis=0, keepdims=True)
        vals.append(m)
        ids.append(i)
        if r + 1 < k:
            s = jnp.where(rid == i, -jnp.inf, s)
    return jnp.concatenate(vals, axis=0), jnp.concatenate(ids, axis=0)


_SUB = 8
_STAIR_PIECES = ([(0, 1, 0, _SUB), (0, 1, _SUB, _SUB), (1, 1, 0, _SUB)]
                 + [(a, 1, 0, TOPK // (a + 1)) for a in range(2, _SUB)] + [(_SUB, _SUB, 0, 1)])


def _stair_candidates(s1, s2):
    sub = lax.broadcasted_iota(jnp.int32, (_SUB, s1.shape[1]), 0)
    pieces = []
    for a0, na, b0, nb in _STAIR_PIECES:
        if na == 1:
            piece = s1[a0:a0 + 1, :] + s2[b0:b0 + _SUB, :]
            if nb < _SUB:
                piece = jnp.where(sub < nb, piece, -jnp.inf)
        else:
            piece = s1[a0:a0 + na, :] + s2[b0:b0 + 1, :]
        pieces.append(piece)
    return jnp.concatenate(pieces, axis=0)


def _stair_ranks(pos):
    a = jnp.zeros(pos.shape, F32)
    b = pos
    for p, (a0, na, b0, nb) in enumerate(_STAIR_PIECES):
        start = float(p * _SUB)
        inside = pos >= start
        if na == 1:
            a = jnp.where(inside, float(a0), a)
            b = jnp.where(inside, pos - start + float(b0), b)
        else:
            a = jnp.where(inside, pos - start + float(a0), a)
            b = jnp.where(inside, float(b0), b)
    return a, b


def _pick_rows(sel, table):
    out = jnp.zeros(sel.shape, F32)
    for a in range(table.shape[0]):
        out = out + jnp.where(sel == float(a), table[a:a + 1, :], 0.0)
    return out


def _select_unit(q_scr, keys_ref, sel_t_scr, h, part):
    tokens = pl.ds(pl.multiple_of(part * LANES, LANES), LANES)
    top = [_top_rows(_dot_nt(keys_ref[2 * h + c], q_scr[2 * h + c, tokens, :]), TOPK) for c in range(2)]
    (s1, i1), (s2, i2) = top
    top_s, pos = _top_rows(_stair_candidates(s1, s2), TOPK)
    a_sel, b_sel = _stair_ranks(pos)
    e = jnp.exp(top_s - top_s[0:1, :])
    rows = pl.ds(pl.multiple_of(h * TOPK, TOPK), TOPK)
    sel_t_scr[0, part, rows, :] = _pick_rows(a_sel, i1)
    sel_t_scr[1, part, rows, :] = _pick_rows(b_sel, i2)
    sel_t_scr[2, part, rows, :] = e / jnp.sum(e, axis=0, keepdims=True)


def _peer_kernel(xs_ref, sh_ref, sc_ref, ng_ref, wq_ref, keys_ref, u_ref, v_ref, xr_ref, gate_ref, y_ref,
                 hb_scr, q_scr, sel_t_scr, sel_scr, w3_ref, acc_ref):
    i = pl.program_id(0)
    j = pl.program_id(1)
    tm = xs_ref.shape[0]
    nparts = tm // LANES
    half = N_KEYS // 2
    pitch = w3_ref.shape[0] // half
    e_tile = u_ref.shape[0]
    slot_new, slot_dense = i % 2, (i + 1) % 2

    @pl.when((i == 0) & (j == 0))
    def _():
        w3_ref[...] = jnp.zeros(w3_ref.shape, w3_ref.dtype)
        hb_scr[1] = jnp.zeros(hb_scr.shape[1:], hb_scr.dtype)

    @pl.when((i > 0) & (j == 0))
    def _():
        for k in range(3):
            for part in range(nparts):
                sel_scr[k, part * LANES:(part + 1) * LANES, :] = sel_t_scr[k, part].T
        row = lax.broadcasted_iota(jnp.int32, (N_KEYS, N_KEYS), 0)
        i1_ids = jnp.where(row < half, 2 * row, 2 * (row - half) + 1).astype(F32)
        i2_ids = row.astype(F32)

        def token(t, carry):
            arow = sel_scr[0, pl.ds(t, 1), :]
            brow = sel_scr[1, pl.ds(t, 1), :]
            grow = sel_scr[2, pl.ds(t, 1), :]
            oa = jnp.where(arow == i1_ids, 1.0, 0.0).astype(BF16)
            ob = jnp.where(brow == i2_ids, 0.5 * grow, 0.0).astype(BF16)
            w = _dot_nt(oa, ob)
            w3_ref[pl.ds(t, half, stride=pitch), :] = pltpu.pack_elementwise([w[:half], w[half:]], packed_dtype=BF16)
            return carry

        lax.fori_loop(0, tm, token, 0, unroll=TOKEN_UNROLL)

    @pl.when(j == 0)
    def _():
        acc_ref[...] = jnp.zeros(acc_ref.shape, F32)
        hb = _modulated(xs_ref[...], ng_ref[...], sc_ref[...], sh_ref[...]).astype(BF16)
        hb_scr[slot_new] = hb
        q_all = _dot(hb, wq_ref[...])
        for hc in range(2 * PEER_HEADS):
            q_scr[hc] = q_all[:, hc * N_KEYS:(hc + 1) * N_KEYS].astype(BF16)

    _select_unit(q_scr, keys_ref, sel_t_scr, j // nparts, j % nparts)

    cols = []
    for q in range(e_tile // (2 * N_KEYS)):
        word = w3_ref[pl.ds(pl.multiple_of((e_tile // (2 * N_KEYS) * j + q) * pitch, 8), tm), :]
        cols += [pltpu.unpack_elementwise(word, index=k, packed_dtype=BF16, unpacked_dtype=F32) for k in range(2)]
    w = jnp.concatenate(cols, axis=1)
    hid = _dot_nt(hb_scr[slot_dense], u_ref[...])
    act = hid * (1.0 + lax.erf(hid * math.sqrt(0.5)))
    acc_ref[...] += _dot((w * act).astype(BF16), v_ref[...])

    @pl.when((i > 0) & (j == pl.num_programs(1) - 1))
    def _():
        y_ref[...] = xr_ref[...] + gate_ref[...] * acc_ref[...]


def _peer(x, mod, ng, wq_bf, keys_bf, u_bf, v_bf, *, tm):
    t = x.shape[0]
    nt = t // tm
    units = PEER_HEADS * (tm // LANES)
    e_tile = u_bf.shape[0] // units
    assert e_tile % (2 * N_KEYS) == 0 and tm % TOKEN_UNROLL == 0
    new_tile = lambda i: jnp.minimum(i, nt - 1)
    dense_tile = lambda i: jnp.maximum(i - 1, 0)
    full = lambda i, j: (0, 0)
    return pl.pallas_call(
        _peer_kernel,
        grid=(nt + 1, units),
        in_specs=[pl.BlockSpec((tm, D_MODEL), lambda i, j: (new_tile(i), 0)),
                  mod.spec(3, tm, new_tile), mod.spec(4, tm, new_tile),
                  pl.BlockSpec((1, D_MODEL), full),
                  pl.BlockSpec(wq_bf.shape, full),
                  pl.BlockSpec(keys_bf.shape, lambda i, j: (0, 0, 0)),
                  pl.BlockSpec((e_tile, D_MODEL), lambda i, j: (j, 0)),
                  pl.BlockSpec((e_tile, D_MODEL), lambda i, j: (j, 0)),
                  pl.BlockSpec((tm, D_MODEL), lambda i, j: (dense_tile(i), 0)),
                  mod.spec(5, tm, dense_tile)],
        out_specs=pl.BlockSpec((tm, D_MODEL), lambda i, j: (dense_tile(i), 0)),
        out_shape=jax.ShapeDtypeStruct(x.shape, F32),
        scratch_shapes=[pltpu.VMEM((2, tm, D_MODEL), BF16),
                        pltpu.VMEM((2 * PEER_HEADS, tm, N_KEYS), BF16),
                        pltpu.VMEM((3, tm // LANES, PEER_HEADS * TOPK, LANES), F32),
                        pltpu.VMEM((3, tm, PEER_HEADS * TOPK), F32),
                        pltpu.VMEM((N_KEYS // 2 * (tm + W3_PAD), N_KEYS), jnp.int32),
                        pltpu.VMEM((tm, D_MODEL), F32)],
        compiler_params=_cparams(("arbitrary", "arbitrary")),
        name="peer",
    )(x, mod.arr, mod.arr, ng, wq_bf, keys_bf, u_bf, v_bf, x, mod.arr)


def _rope_tables(pos):
    half = DH // 2
    inv = ROPE_THETA ** (-jnp.arange(half, dtype=F32) / half)
    ang = pos.astype(F32)[:, None] * inv[None, :]
    cos, sin = jnp.cos(ang), jnp.sin(ang)
    return jnp.tile(cos, (1, 4)), jnp.concatenate([-sin, sin, -sin, sin], axis=1)


def _pair_tile(g):
    return jnp.tile(g.reshape(1, -1), (1, LANES // g.shape[-1]))


def kernel(x_prompt, x_sample, c_prompt, c_sample, cache_a_k, cache_a_v, cache_b_k, cache_b_v, ada_w, ada_b, norm_g, a_w_in, a_g_q, a_g_k, a_lq1, a_lk1, a_lq2, a_lk2, a_g_sub, a_w_out, b_w_in, b_g_q, b_g_k, b_rel_bias, b_w_out, peer_w_q, peer_sub_keys, peer_u, peer_v):
    batch, seq, _ = x_prompt.shape
    dbatch, t_new, _ = x_sample.shape
    past_a = cache_a_k.shape[2]
    past_b = cache_b_k.shape[2]
    depth = ada_w.shape[0]
    tp, ts = batch * seq, dbatch * t_new
    tm_s = min(TM_PROJ, ts)
    tmd_s = min(TM_DENSE // 2, ts)

    xp = x_prompt.reshape(tp, D_MODEL)
    xs = x_sample.reshape(ts, D_MODEL)
    c_all = jnp.concatenate([c_prompt, c_sample], axis=0)

    cos_p, sin_p = _rope_tables(jnp.arange(seq))
    cos_s, sin_s = _rope_tables(past_a + jnp.arange(t_new))
    cos_s, sin_s = jnp.tile(cos_s, (tm_s // t_new, 1)), jnp.tile(sin_s, (tm_s // t_new, 1))

    outs = {n: [] for n in ("akp", "avp", "aks", "avs", "bkp", "bvp", "bks", "bvs")}
    for i in range(depth):
        j = i // 2
        mod = _ada_mod(c_all, ada_w[i], ada_b[i])
        mod_p = _Mod(mod[:batch].reshape(batch * 6, 1, D_MODEL), False, lambda tm: seq // tm)
        mod_s = _Mod(jnp.repeat(mod[batch:].reshape(dbatch, 6, D_MODEL).transpose(1, 0, 2), t_new, axis=1),
                     True, None)
        ng0, ng1 = norm_g[i, 0].reshape(1, -1), norm_g[i, 1].reshape(1, -1)

        if i % 2 == 0:
            lam_init = 0.8 - 0.6 * math.exp(-0.3 * i)
            w_in = a_w_in[j].astype(BF16)
            gq, gk = _pair_tile(a_g_q[j]), _pair_tile(a_g_k[j])
            lamv = jnp.stack([a_lq1[j], a_lk1[j], a_lq2[j], a_lk2[j]])
            gsub = a_g_sub[j].reshape(1, -1)
            qp, kp, vp = _inproj(xp, mod_p, ng0, w_in, gq, gk, cos_p, sin_p, rope=True, tm=TM_PROJ,
                                 qscale=DH ** -0.5 * _LOG2E)
            qs, ks, vs = _inproj(xs, mod_s, ng0, w_in, gq, gk, cos_s, sin_s, rope=True, tm=tm_s)
            op = _flash_diff(qp, kp, vp, lamv, gsub, batch=batch, seq=seq, lam_init=lam_init)
            os_ = _diff_sample(qs, cache_a_k[j].reshape(dbatch * past_a, D_MODEL),
                               cache_a_v[j].reshape(dbatch * past_a, D_MODEL), ks, vs, lamv, gsub,
                               batch=dbatch, t_new=t_new, past=past_a, lam_init=lam_init)
            w_out = a_w_out[j].astype(BF16)
            heads = D_MODEL // LANES
            k4p, v4p = _cache_layout(kp, vp, tm=TM_PROJ)
            k4s, v4s = _cache_layout(ks, vs, tm=tm_s)
            outs["akp"].append(k4p.reshape(batch, seq, heads, 2, DH))
            outs["avp"].append(v4p.reshape(batch, seq, heads, 2 * DH))
            outs["aks"].append(k4s.reshape(dbatch, t_new, heads, 2, DH))
            outs["avs"].append(v4s.reshape(dbatch, t_new, heads, 2 * DH))
        else:
            w_in = b_w_in[j].astype(BF16)
            gq, gk = _pair_tile(b_g_q[j]), _pair_tile(b_g_k[j])
            qp, kp, vp = _inproj(xp, mod_p, ng0, w_in, gq, gk, cos_p, sin_p, rope=False, tm=TM_PROJ)
            qs, ks, vs = _inproj(xs, mod_s, ng0, w_in, gq, gk, cos_s, sin_s, rope=False, tm=tm_s)
            op = _band_prompt(qp, kp, vp, b_rel_bias[j], batch=batch, seq=seq)
            os_ = _band_sample(qs, cache_b_k[j].reshape(dbatch * past_b, D_MODEL),
                               cache_b_v[j].reshape(dbatch * past_b, D_MODEL), ks, vs, b_rel_bias[j],
                               batch=dbatch, t_new=t_new, past=past_b)
            w_out = b_w_out[j].astype(BF16)
            heads = D_MODEL // DH
            keep = min(BAND_PAST, seq)
            k4 = kp.reshape(batch, seq, heads, DH)
            v4 = vp.reshape(batch, seq, heads, DH)
            outs["bkp"].append(k4[:, seq - keep:])
            outs["bvp"].append(v4[:, seq - keep:])
            outs["bks"].append(jnp.concatenate([cache_b_k[j], ks.reshape(dbatch, t_new, heads, DH)], axis=1)[:, t_new:])
            outs["bvs"].append(jnp.concatenate([cache_b_v[j], vs.reshape(dbatch, t_new, heads, DH)], axis=1)[:, t_new:])

        xp = _outproj(op, w_out, xp, mod_p, tm=TM_PROJ)
        xs = _outproj(os_, w_out, xs, mod_s, tm=tm_s)

        wq = peer_w_q[i].astype(BF16)
        keys = peer_sub_keys[i].astype(BF16).reshape(PEER_HEADS * 2, N_KEYS, -1)
        u_bf, v_bf = peer_u[i].astype(BF16), peer_v[i].astype(BF16)
        xp = _peer(xp, mod_p, ng1, wq, keys, u_bf, v_bf, tm=TM_DENSE)
        xs = _peer(xs, mod_s, ng1, wq, keys, u_bf, v_bf, tm=tmd_s)

    st = lambda n: jnp.stack(outs[n], 0)
    return (xp.reshape(x_prompt.shape), xs.reshape(x_sample.shape),
            st("akp"), st("avp"), st("aks"), st("avs"), st("bkp"), st("bvp"), st("bks"), st("bvs"))
```

```python
import functools
import math

import numpy as np
import jax
import jax.numpy as jnp
from jax import lax
from jax.experimental import pallas as pl
from jax.experimental.pallas import tpu as pltpu

F32 = jnp.float32
BF16 = jnp.bfloat16

D_MODEL = 1024
CHUNK = 64
EPS = 1e-6
NEG = -1e30
ROPE_THETA = 10000.0
DH = 64
LANES = 128
PREV_CHUNKS = 8
BAND_PAST = PREV_CHUNKS * CHUNK
REL_CLIP = 128
PEER_HEADS = 8
N_KEYS = 128
TOPK = 16
VMEM_LIMIT = 48 * 1024 * 1024

TM_PROJ = 1024
TQ_FLASH = 1024
FLASH_ROWS = 1024
TQ_BAND = 256
BAND_QTILES = 4
TM_DENSE = 512
TOKEN_UNROLL = 128
W3_PAD = 8

_NT = (((1,), (1,)), ((), ()))
_LOG2E = math.log2(math.e)


def _cparams(sem):
    return pltpu.CompilerParams(dimension_semantics=sem, vmem_limit_bytes=VMEM_LIMIT)


def _dot(a, b):
    return jnp.dot(a, b, preferred_element_type=F32)


def _dot_nt(a, b):
    return lax.dot_general(a, b, _NT, preferred_element_type=F32)


def _split(a):
    hi = a.astype(BF16)
    lo = (a - hi.astype(F32)).astype(BF16)
    return hi, lo


def _ada_kernel(c_ref, w_ref, b_ref, o_ref):
    c = c_ref[...]
    a = c * (1.0 / (1.0 + jnp.exp(-c)))
    ah, al = _split(a)
    wh, wl = _split(w_ref[...])
    o_ref[...] = _dot(ah, wh) + _dot(al, wh) + _dot(ah, wl) + b_ref[...]


def _ada_mod(c_all, w, b):
    n, d = c_all.shape
    nout = w.shape[1]
    tn = 512
    return pl.pallas_call(
        _ada_kernel,
        grid=(nout // tn,),
        in_specs=[pl.BlockSpec((n, d), lambda j: (0, 0)),
                  pl.BlockSpec((d, tn), lambda j: (0, j)),
                  pl.BlockSpec((1, tn), lambda j: (0, j))],
        out_specs=pl.BlockSpec((n, tn), lambda j: (0, j)),
        out_shape=jax.ShapeDtypeStruct((n, nout), F32),
        compiler_params=_cparams(("parallel",)),
        name="ada_mod",
    )(c_all, w, b.reshape(1, nout))


class _Mod:
    def __init__(self, arr, per_row, tiles_per_batch):
        self.arr, self.per_row, self.tpb = arr, per_row, tiles_per_batch

    def spec(self, k, tm, tile=lambda i: i):
        if self.per_row:
            return pl.BlockSpec((None, tm, D_MODEL), lambda i, *_: (k, tile(i), 0))
        tpb = self.tpb(tm)
        return pl.BlockSpec((None, 1, D_MODEL), lambda i, *_: ((tile(i) // tpb) * 6 + k, 0, 0))


def _modulated(x, ng, scale, shift):
    ms = jnp.mean(x * x, axis=-1, keepdims=True)
    return (x * lax.rsqrt(ms + EPS) * ng) * (1.0 + scale) + shift


def _inproj_kernel(x_ref, sh_ref, sc_ref, ng_ref, w_ref, gq_ref, gk_ref, cos_ref, sin_ref,
                   q_ref, k_ref, v_ref, *, rope, qscale):
    tm = x_ref.shape[0]
    hb = _modulated(x_ref[...], ng_ref[...], sc_ref[...], sh_ref[...]).astype(BF16)
    lane = lax.broadcasted_iota(jnp.int32, (tm, LANES), 1)
    lo = lane < DH
    swap_sel = (lane & (DH // 2)) != 0

    def norm_rope(xb, g):
        x2 = xb * xb
        slo = jnp.sum(jnp.where(lo, x2, 0.0), axis=-1, keepdims=True)
        shi = jnp.sum(jnp.where(lo, 0.0, x2), axis=-1, keepdims=True)
        ms = jnp.where(lo, slo, shi) * (1.0 / DH)
        y = xb * lax.rsqrt(ms + EPS) * g
        if rope:
            sw = jnp.where(swap_sel, pltpu.roll(y, DH // 2, 1), pltpu.roll(y, LANES - DH // 2, 1))
            y = y * cos_ref[...] + sw * sin_ref[...]
        return y

    nblk = w_ref.shape[1] // (2 * LANES)
    for j in range(nblk):
        acc = _dot(hb, w_ref[:, j * 2 * LANES:(j + 1) * 2 * LANES])
        for half in range(2):
            blk = acc[:, half * LANES:(half + 1) * LANES]
            col = j * 2 * LANES + half * LANES
            if col < D_MODEL:
                q_ref[:, col:col + LANES] = (norm_rope(blk, gq_ref[...]) * qscale).astype(BF16)
            elif col < 2 * D_MODEL:
                k_ref[:, col - D_MODEL:col - D_MODEL + LANES] = norm_rope(blk, gk_ref[...])
            else:
                v_ref[:, col - 2 * D_MODEL:col - 2 * D_MODEL + LANES] = blk


def _inproj(x, mod, ng, w_bf, gq, gk, cos, sin, *, rope, tm, qscale=DH ** -0.5):
    t = x.shape[0]
    nrep = cos.shape[0] // tm
    row = lambda i: (i, 0)
    full = lambda i: (0, 0)
    tab = lambda i: (i % nrep, 0)
    return pl.pallas_call(
        functools.partial(_inproj_kernel, rope=rope, qscale=qscale),
        grid=(t // tm,),
        in_specs=[pl.BlockSpec((tm, D_MODEL), row), mod.spec(0, tm), mod.spec(1, tm),
                  pl.BlockSpec((1, D_MODEL), full),
                  pl.BlockSpec(w_bf.shape, full),
                  pl.BlockSpec((1, LANES), full), pl.BlockSpec((1, LANES), full),
                  pl.BlockSpec((tm, LANES), tab), pl.BlockSpec((tm, LANES), tab)],
        out_specs=[pl.BlockSpec((tm, D_MODEL), row)] * 3,
        out_shape=[jax.ShapeDtypeStruct((t, D_MODEL), BF16),
                   jax.ShapeDtypeStruct((t, D_MODEL), F32),
                   jax.ShapeDtypeStruct((t, D_MODEL), F32)],
        compiler_params=_cparams(("parallel",)),
        name="qkv_proj",
    )(x, mod.arr, mod.arr, ng, w_bf, gq, gk, cos, sin)


def _cache_layout_kernel(k_ref, v_ref, k4_ref, v4_ref):
    heads = v4_ref.shape[1]
    for h in range(heads):
        kb = k_ref[:, h * LANES:(h + 1) * LANES]
        v4_ref[:, h, :] = v_ref[:, h * LANES:(h + 1) * LANES]
        for c in range(2):
            k4_ref[:, h, c, :] = kb[:, c * DH:(c + 1) * DH]


def _cache_layout(k, v, *, tm):
    t = k.shape[0]
    heads = D_MODEL // LANES
    row = lambda i: (i, 0)
    return pl.pallas_call(
        _cache_layout_kernel,
        grid=(t // tm,),
        in_specs=[pl.BlockSpec((tm, D_MODEL), row)] * 2,
        out_specs=[pl.BlockSpec((tm, heads, 2, DH), lambda i: (i, 0, 0, 0)),
                   pl.BlockSpec((tm, heads, LANES), lambda i: (i, 0, 0))],
        out_shape=[jax.ShapeDtypeStruct((t, heads, 2, DH), F32), jax.ShapeDtypeStruct((t, heads, LANES), F32)],
        compiler_params=_cparams(("parallel",)),
        name="cache_layout",
    )(k, v)


def _diff_lambda(lamv_ref, lam_init):
    lv = lamv_ref[...]
    e1 = jnp.exp(jnp.sum(lv[0:1] * lv[1:2], axis=-1, keepdims=True))
    e2 = jnp.exp(jnp.sum(lv[2:3] * lv[3:4], axis=-1, keepdims=True))
    return e1 - e2 + lam_init


def _diff_finish(o0, o1, lamv_ref, gsub_ref, lam_init):
    o = o0 - _diff_lambda(lamv_ref, lam_init) * o1
    ms = jnp.mean(o * o, axis=-1, keepdims=True)
    return ((o * lax.rsqrt(ms + EPS) * gsub_ref[...]) * (1.0 - lam_init)).astype(BF16)


def _component_queries(q):
    lane = lax.broadcasted_iota(jnp.int32, q.shape, 1)
    zero = jnp.zeros_like(q)
    return jnp.where(lane < DH, q, zero), jnp.where(lane < DH, zero, q)


def _flash_diff_kernel(qt_ref, kt_ref, q_ref, k_ref, v_ref, lamv_ref, gsub_ref, o_ref,
                       m_ref, l_ref, a_ref, *, lam_init):
    p = pl.program_id(2)
    qi = qt_ref[p]
    ki = kt_ref[p]
    tq = q_ref.shape[0]

    @pl.when(ki == 0)
    def _():
        m_ref[...] = jnp.full(m_ref.shape, NEG, F32)
        l_ref[...] = jnp.zeros(l_ref.shape, F32)
        a_ref[...] = jnp.zeros(a_ref.shape, F32)

    def update(masked):
        kb = k_ref[...].astype(BF16)
        vb = v_ref[...].astype(BF16)
        qq = jnp.concatenate(_component_queries(q_ref[...]), axis=0)
        for r in range(2 * tq // FLASH_ROWS):
            rows = slice(r * FLASH_ROWS, (r + 1) * FLASH_ROWS)
            q0 = (r * FLASH_ROWS) % tq
            ncol = min(q0 + FLASH_ROWS, tq) if masked else tq
            s = _dot_nt(qq[rows], kb[:ncol])
            if masked:
                row_c = ((lax.broadcasted_iota(jnp.int32, (FLASH_ROWS, ncol), 0) + q0) % tq) // CHUNK
                col_c = lax.broadcasted_iota(jnp.int32, (FLASH_ROWS, ncol), 1) // CHUNK
                s = jnp.where(col_c <= row_c, s, NEG)
            m_prev = m_ref[rows, :]
            m_new = jnp.maximum(m_prev, jnp.max(s, axis=-1, keepdims=True))
            alpha = jnp.exp2(m_prev - m_new)
            pm = jnp.exp2(s - jnp.tile(m_new, (1, ncol // LANES)))
            l_ref[rows, :] = alpha * l_ref[rows, :] + jnp.sum(pm, axis=-1, keepdims=True)
            a_ref[rows, :] = alpha * a_ref[rows, :] + _dot(pm.astype(BF16), vb[:ncol])
            m_ref[rows, :] = m_new

    @pl.when(ki < qi)
    def _():
        update(False)

    @pl.when(ki == qi)
    def _():
        update(True)
        o = a_ref[...] / l_ref[...]
        o_ref[...] = _diff_finish(o[:tq], o[tq:], lamv_ref, gsub_ref, lam_init)


def _flash_diff(q, k, v, lamv, gsub, *, batch, seq, lam_init):
    tq = min(TQ_FLASH, seq)
    nq = seq // tq
    heads = D_MODEL // LANES
    pairs = [(a, b) for a in range(nq) for b in range(a + 1)]
    qt = jnp.asarray([a for a, _ in pairs], jnp.int32)
    kt = jnp.asarray([b for _, b in pairs], jnp.int32)
    qmap = lambda b, h, p, qt, kt: (b * nq + qt[p], h)
    kmap = lambda b, h, p, qt, kt: (b * nq + kt[p], h)
    return pl.pallas_call(
        functools.partial(_flash_diff_kernel, lam_init=lam_init),
        grid_spec=pltpu.PrefetchScalarGridSpec(
            num_scalar_prefetch=2,
            grid=(batch, heads, len(pairs)),
            in_specs=[pl.BlockSpec((tq, LANES), qmap),
                      pl.BlockSpec((tq, LANES), kmap),
                      pl.BlockSpec((tq, LANES), kmap),
                      pl.BlockSpec(lamv.shape, lambda *_: (0, 0)),
                      pl.BlockSpec((1, LANES), lambda *_: (0, 0))],
            out_specs=pl.BlockSpec((tq, LANES), qmap),
            scratch_shapes=[pltpu.VMEM((2 * tq, LANES), F32)] * 3),
        out_shape=jax.ShapeDtypeStruct(q.shape, BF16),
        compiler_params=_cparams(("parallel", "parallel", "arbitrary")),
        name="flash_diff_attn",
    )(qt, kt, q, k, v, lamv, gsub)


def _joint_softmax_attend(qc, pieces):
    ss = []
    for kb, _, bias, visible in pieces:
        s = _dot_nt(qc, kb)
        if bias is not None:
            s = s + bias
        if visible is not None:
            s = jnp.where(visible, s, NEG)
        ss.append(s)
    m = functools.reduce(jnp.maximum, [jnp.max(s, axis=-1, keepdims=True) for s in ss])
    l = 0.0
    o = 0.0
    for s, (_, vb, _, _) in zip(ss, pieces):
        pm = jnp.exp(s - m)
        l = l + jnp.sum(pm, axis=-1, keepdims=True)
        o = o + _dot(pm.astype(BF16), vb)
    return o / l


def _diff_sample_kernel(q_ref, ck_ref, cv_ref, nk_ref, nv_ref, lamv_ref, gsub_ref, o_ref, *, lam_init):
    t_new = q_ref.shape[0]
    pieces = [(ck_ref[...].astype(BF16), cv_ref[...].astype(BF16), None, None),
              (nk_ref[...].astype(BF16), nv_ref[...].astype(BF16), None, None)]
    o = _joint_softmax_attend(jnp.concatenate(_component_queries(q_ref[...]), axis=0), pieces)
    o_ref[...] = _diff_finish(o[:t_new], o[t_new:], lamv_ref, gsub_ref, lam_init)


def _diff_sample(q, ck, cv, nk, nv, lamv, gsub, *, batch, t_new, past, lam_init):
    heads = D_MODEL // LANES
    bh = lambda b, h: (b, h)
    return pl.pallas_call(
        functools.partial(_diff_sample_kernel, lam_init=lam_init),
        grid=(batch, heads),
        in_specs=[pl.BlockSpec((t_new, LANES), bh),
                  pl.BlockSpec((past, LANES), bh), pl.BlockSpec((past, LANES), bh),
                  pl.BlockSpec((t_new, LANES), bh), pl.BlockSpec((t_new, LANES), bh),
                  pl.BlockSpec(lamv.shape, lambda b, h: (0, 0)),
                  pl.BlockSpec((1, LANES), lambda b, h: (0, 0))],
        out_specs=pl.BlockSpec((t_new, LANES), bh),
        out_shape=jax.ShapeDtypeStruct(q.shape, BF16),
        compiler_params=_cparams(("parallel", "parallel")),
        name="diff_attn_sample",
    )(q, ck, cv, nk, nv, lamv, gsub)


def _band_finish(o):
    tq = o.shape[0] // 2
    lane = lax.broadcasted_iota(jnp.int32, (tq, LANES), 1)
    return jnp.where(lane < DH, o[:tq], o[tq:]).astype(BF16)


def _band_prompt_kernel(q_ref, *refs):
    nkv = BAND_QTILES + 2
    k_refs, v_refs, bias_ref, o_ref = refs[:nkv], refs[nkv:2 * nkv], refs[2 * nkv], refs[2 * nkv + 1]
    first = pl.program_id(2) * BAND_QTILES
    tq = q_ref.shape[0] // BAND_QTILES
    row_c = (lax.broadcasted_iota(jnp.int32, (2 * tq, tq), 0) % tq) // CHUNK
    col_c = lax.broadcasted_iota(jnp.int32, (2 * tq, tq), 1) // CHUNK
    kvs = [(k_ref[...].astype(BF16), v_ref[...].astype(BF16)) for k_ref, v_ref in zip(k_refs, v_refs)]
    for sub in range(BAND_QTILES):
        pieces = []
        for r in range(3):
            before_start = jnp.where(first + sub + r - 2 < 0, 4 * PREV_CHUNKS, 0)
            kc = col_c + (r - 2) * (tq // CHUNK) + before_start
            visible = (kc <= row_c) & (kc >= row_c - PREV_CHUNKS)
            pieces.append((*kvs[sub + r], bias_ref[:, r * tq:(r + 1) * tq], visible))
        rows = slice(sub * tq, (sub + 1) * tq)
        qq = jnp.concatenate(_component_queries(q_ref[rows, :]), axis=0)
        o_ref[rows, :] = _band_finish(_joint_softmax_attend(qq, pieces))


def _band_bias_tiles(table, nq, nk, c0):
    n = nq + nk - 1
    m = np.arange(n)
    offset = np.where(m < nk, -m, n - m)
    e = table[:, np.clip(c0 + offset, -REL_CLIP, REL_CLIP) + REL_CLIP]
    a = jnp.tile(e, (1, nq))[:, :nq * (n - 1)].reshape(-1, nq, n - 1)[:, :, :nk]
    return a.reshape(table.shape[0] // 2, 2 * nq, nk)


def _band_prompt(q, k, v, table, *, batch, seq):
    tq = TQ_BAND
    nsub = BAND_QTILES
    assert BAND_PAST == 2 * tq and seq % (nsub * tq) == 0
    nq = seq // tq
    hp = D_MODEL // LANES
    bias = _band_bias_tiles(table, tq, 3 * tq, 2 * tq)
    qmap = lambda h, b, i: (b * (nq // nsub) + i, h)
    kmap = lambda r: (lambda h, b, i: (b * nq + jnp.maximum(nsub * i + r - 2, 0), h))
    kv_specs = [pl.BlockSpec((tq, LANES), kmap(r)) for r in range(nsub + 2)]
    return pl.pallas_call(
        _band_prompt_kernel,
        grid=(hp, batch, nq // nsub),
        in_specs=[pl.BlockSpec((nsub * tq, LANES), qmap)] + kv_specs * 2
                 + [pl.BlockSpec((None, 2 * tq, 3 * tq), lambda h, b, i: (h, 0, 0))],
        out_specs=pl.BlockSpec((nsub * tq, LANES), qmap),
        out_shape=jax.ShapeDtypeStruct(q.shape, BF16),
        compiler_params=_cparams(("parallel", "parallel", "parallel")),
        name="band_attn_prompt",
    )(q, *([k] * (nsub + 2)), *([v] * (nsub + 2)), bias)


def _band_sample_kernel(q_ref, ck_ref, cv_ref, nk_ref, nv_ref, bc_ref, bn_ref, o_ref):
    pieces = [(ck_ref[...].astype(BF16), cv_ref[...].astype(BF16), bc_ref[...], None),
              (nk_ref[...].astype(BF16), nv_ref[...].astype(BF16), bn_ref[...], None)]
    qq = jnp.concatenate(_component_queries(q_ref[...]), axis=0)
    o_ref[...] = _band_finish(_joint_softmax_attend(qq, pieces))


def _band_sample(q, ck, cv, nk, nv, table, *, batch, t_new, past):
    hp = D_MODEL // LANES
    bias = _band_bias_tiles(table, t_new, past + t_new, past)
    bias_c, bias_n = bias[..., :past], bias[..., past:]
    hb = lambda h, b: (b, h)
    return pl.pallas_call(
        _band_sample_kernel,
        grid=(hp, batch),
        in_specs=[pl.BlockSpec((t_new, LANES), hb),
                  pl.BlockSpec((past, LANES), hb), pl.BlockSpec((past, LANES), hb),
                  pl.BlockSpec((t_new, LANES), hb), pl.BlockSpec((t_new, LANES), hb),
                  pl.BlockSpec((None, 2 * t_new, past), lambda h, b: (h, 0, 0)),
                  pl.BlockSpec((None, 2 * t_new, t_new), lambda h, b: (h, 0, 0))],
        out_specs=pl.BlockSpec((t_new, LANES), hb),
        out_shape=jax.ShapeDtypeStruct(q.shape, BF16),
        compiler_params=_cparams(("parallel", "parallel")),
        name="band_attn_sample",
    )(q, ck, cv, nk, nv, bias_c, bias_n)


def _outproj_kernel(o_ref, w_ref, x_ref, gate_ref, y_ref):
    y_ref[...] = x_ref[...] + gate_ref[...] * _dot(o_ref[...], w_ref[...])


def _outproj(o, w_bf, x, mod, *, tm):
    t = x.shape[0]
    row = lambda i: (i, 0)
    return pl.pallas_call(
        _outproj_kernel,
        grid=(t // tm,),
        in_specs=[pl.BlockSpec((tm, D_MODEL), row), pl.BlockSpec(w_bf.shape, lambda i: (0, 0)),
                  pl.BlockSpec((tm, D_MODEL), row), mod.spec(2, tm)],
        out_specs=pl.BlockSpec((tm, D_MODEL), row),
        out_shape=jax.ShapeDtypeStruct(x.shape, F32),
        compiler_params=_cparams(("parallel",)),
        name="out_proj",
    )(o, w_bf, x, mod.arr)


def _top_rows(s, k):
    if s.shape[1] > LANES:
        cols = [_top_rows(s[:, c:c + LANES], k) for c in range(0, s.shape[1], LANES)]
        return tuple(jnp.concatenate(x, axis=1) for x in zip(*cols))
    rows = s.shape[0]
    rid = lax.broadcasted_iota(jnp.int32, s.shape, 0).astype(F32)
    vals, ids = [], []
    for r in range(k):
        cand = [(s[v:v + _SUB], rid[v:v + _SUB]) for v in range(0, rows, _SUB)]
        while len(cand) > 1:
            nxt = []
            for a in range(0, len(cand) - 1, 2):
                (va, ia), (vb, ib) = cand[a], cand[a + 1]
                keep = va >= vb
                nxt.append((jnp.where(keep, va, vb), jnp.where(keep, ia, ib)))
            cand = nxt + cand[len(cand) - len(cand) % 2:]
        v8, i8 = cand[0]
        m = jnp.max(v8, axis=0, keepdims=True)
        i = jnp.min(jnp.where(v8 == m, i8, float(rows)), axis=0, keepdims=True)
        vals.append(m)
        ids.append(i)
        if r + 1 < k:
            s = jnp.where(rid == i, -jnp.inf, s)
    return jnp.concatenate(vals, axis=0), jnp.concatenate(ids, axis=0)


_SUB = 8
_STAIR_PIECES = ([(0, 1, 0, _SUB), (0, 1, _SUB, _SUB), (1, 1, 0, _SUB)]
                 + [(a, 1, 0, TOPK // (a + 1)) for a in range(2, _SUB)] + [(_SUB, _SUB, 0, 1)])


def _stair_candidates(s1, s2):
    sub = lax.broadcasted_iota(jnp.int32, (_SUB, s1.shape[1]), 0)
    pieces = []
    for a0, na, b0, nb in _STAIR_PIECES:
        if na == 1:
            piece = s1[a0:a0 + 1, :] + s2[b0:b0 + _SUB, :]
            if nb < _SUB:
                piece = jnp.where(sub < nb, piece, -jnp.inf)
        else:
            piece = s1[a0:a0 + na, :] + s2[b0:b0 + 1, :]
        pieces.append(piece)
    return jnp.concatenate(pieces, axis=0)


def _stair_ranks(pos):
    a = jnp.zeros(pos.shape, F32)
    b = pos
    for p, (a0, na, b0, nb) in enumerate(_STAIR_PIECES):
        start = float(p * _SUB)
        inside = pos >= start
        if na == 1:
            a = jnp.where(inside, float(a0), a)
            b = jnp.where(inside, pos - start + float(b0), b)
        else:
            a = jnp.where(inside, pos - start + float(a0), a)
            b = jnp.where(inside, float(b0), b)
    return a, b


def _pick_rows(sel, table):
    out = jnp.zeros(sel.shape, F32)
    for a in range(table.shape[0]):
        out = out + jnp.where(sel == float(a), table[a:a + 1, :], 0.0)
    return out


def _select_unit(q_scr, keys_ref, sel_t_scr, h, part):
    tokens = pl.ds(pl.multiple_of(part * LANES, LANES), LANES)
    top = [_top_rows(_dot_nt(keys_ref[2 * h + c], q_scr[2 * h + c, tokens, :]), TOPK) for c in range(2)]
    (s1, i1), (s2, i2) = top
    top_s, pos = _top_rows(_stair_candidates(s1, s2), TOPK)
    a_sel, b_sel = _stair_ranks(pos)
    e = jnp.exp(top_s - top_s[0:1, :])
    rows = pl.ds(pl.multiple_of(h * TOPK, TOPK), TOPK)
    sel_t_scr[0, part, rows, :] = _pick_rows(a_sel, i1)
    sel_t_scr[1, part, rows, :] = _pick_rows(b_sel, i2)
    sel_t_scr[2, part, rows, :] = e / jnp.sum(e, axis=0, keepdims=True)


def _peer_kernel(xs_ref, sh_ref, sc_ref, ng_ref, wq_ref, keys_ref, u_ref, v_ref, xr_ref, gate_ref, y_ref,
                 hb_scr, q_scr, sel_t_scr, sel_scr, w3_ref, acc_ref):
    i = pl.program_id(0)
    j = pl.program_id(1)
    tm = xs_ref.shape[0]
    nparts = tm // LANES
    half = N_KEYS // 2
    pitch = w3_ref.shape[0] // half
    e_tile = u_ref.shape[0]
    slot_new, slot_dense = i % 2, (i + 1) % 2

    @pl.when((i == 0) & (j == 0))
    def _():
        w3_ref[...] = jnp.zeros(w3_ref.shape, w3_ref.dtype)
        hb_scr[1] = jnp.zeros(hb_scr.shape[1:], hb_scr.dtype)

    @pl.when((i > 0) & (j == 0))
    def _():
        for k in range(3):
            for part in range(nparts):
                sel_scr[k, part * LANES:(part + 1) * LANES, :] = sel_t_scr[k, part].T
        row = lax.broadcasted_iota(jnp.int32, (N_KEYS, N_KEYS), 0)
        i1_ids = jnp.where(row < half, 2 * row, 2 * (row - half) + 1).astype(F32)
        i2_ids = row.astype(F32)

        def token(t, carry):
            arow = sel_scr[0, pl.ds(t, 1), :]
            brow = sel_scr[1, pl.ds(t, 1), :]
            grow = sel_scr[2, pl.ds(t, 1), :]
            oa = jnp.where(arow == i1_ids, 1.0, 0.0).astype(BF16)
            ob = jnp.where(brow == i2_ids, 0.5 * grow, 0.0).astype(BF16)
            w = _dot_nt(oa, ob)
            w3_ref[pl.ds(t, half, stride=pitch), :] = pltpu.pack_elementwise([w[:half], w[half:]], packed_dtype=BF16)
            return carry

        lax.fori_loop(0, tm, token, 0, unroll=TOKEN_UNROLL)

    @pl.when(j == 0)
    def _():
        acc_ref[...] = jnp.zeros(acc_ref.shape, F32)
        hb = _modulated(xs_ref[...], ng_ref[...], sc_ref[...], sh_ref[...]).astype(BF16)
        hb_scr[slot_new] = hb
        q_all = _dot(hb, wq_ref[...])
        for hc in range(2 * PEER_HEADS):
            q_scr[hc] = q_all[:, hc * N_KEYS:(hc + 1) * N_KEYS].astype(BF16)

    _select_unit(q_scr, keys_ref, sel_t_scr, j // nparts, j % nparts)

    cols = []
    for q in range(e_tile // (2 * N_KEYS)):
        word = w3_ref[pl.ds(pl.multiple_of((e_tile // (2 * N_KEYS) * j + q) * pitch, 8), tm), :]
        cols += [pltpu.unpack_elementwise(word, index=k, packed_dtype=BF16, unpacked_dtype=F32) for k in range(2)]
    w = jnp.concatenate(cols, axis=1)
    hid = _dot_nt(hb_scr[slot_dense], u_ref[...])
    act = hid * (1.0 + lax.erf(hid * math.sqrt(0.5)))
    acc_ref[...] += _dot((w * act).astype(BF16), v_ref[...])

    @pl.when((i > 0) & (j == pl.num_programs(1) - 1))
    def _():
        y_ref[...] = xr_ref[...] + gate_ref[...] * acc_ref[...]


def _peer(x, mod, ng, wq_bf, keys_bf, u_bf, v_bf, *, tm):
    t = x.shape[0]
    nt = t // tm
    units = PEER_HEADS * (tm // LANES)
    e_tile = u_bf.shape[0] // units
    assert e_tile % (2 * N_KEYS) == 0 and tm % TOKEN_UNROLL == 0
    new_tile = lambda i: jnp.minimum(i, nt - 1)
    dense_tile = lambda i: jnp.maximum(i - 1, 0)
    full = lambda i, j: (0, 0)
    return pl.pallas_call(
        _peer_kernel,
        grid=(nt + 1, units),
        in_specs=[pl.BlockSpec((tm, D_MODEL), lambda i, j: (new_tile(i), 0)),
                  mod.spec(3, tm, new_tile), mod.spec(4, tm, new_tile),
                  pl.BlockSpec((1, D_MODEL), full),
                  pl.BlockSpec(wq_bf.shape, full),
                  pl.BlockSpec(keys_bf.shape, lambda i, j: (0, 0, 0)),
                  pl.BlockSpec((e_tile, D_MODEL), lambda i, j: (j, 0)),
                  pl.BlockSpec((e_tile, D_MODEL), lambda i, j: (j, 0)),
                  pl.BlockSpec((tm, D_MODEL), lambda i, j: (dense_tile(i), 0)),
                  mod.spec(5, tm, dense_tile)],
        out_specs=pl.BlockSpec((tm, D_MODEL), lambda i, j: (dense_tile(i), 0)),
        out_shape=jax.ShapeDtypeStruct(x.shape, F32),
        scratch_shapes=[pltpu.VMEM((2, tm, D_MODEL), BF16),
                        pltpu.VMEM((2 * PEER_HEADS, tm, N_KEYS), BF16),
                        pltpu.VMEM((3, tm // LANES, PEER_HEADS * TOPK, LANES), F32),
                        pltpu.VMEM((3, tm, PEER_HEADS * TOPK), F32),
                        pltpu.VMEM((N_KEYS // 2 * (tm + W3_PAD), N_KEYS), jnp.int32),
                        pltpu.VMEM((tm, D_MODEL), F32)],
        compiler_params=_cparams(("arbitrary", "arbitrary")),
        name="peer",
    )(x, mod.arr, mod.arr, ng, wq_bf, keys_bf, u_bf, v_bf, x, mod.arr)


def _rope_tables(pos):
    half = DH // 2
    inv = ROPE_THETA ** (-jnp.arange(half, dtype=F32) / half)
    ang = pos.astype(F32)[:, None] * inv[None, :]
    cos, sin = jnp.cos(ang), jnp.sin(ang)
    return jnp.tile(cos, (1, 4)), jnp.concatenate([-sin, sin, -sin, sin], axis=1)


def _pair_tile(g):
    return jnp.tile(g.reshape(1, -1), (1, LANES // g.shape[-1]))


def kernel(x_prompt, x_sample, c_prompt, c_sample, cache_a_k, cache_a_v, cache_b_k, cache_b_v, ada_w, ada_b, norm_g, a_w_in, a_g_q, a_g_k, a_lq1, a_lk1, a_lq2, a_lk2, a_g_sub, a_w_out, b_w_in, b_g_q, b_g_k, b_rel_bias, b_w_out, peer_w_q, peer_sub_keys, peer_u, peer_v):
    batch, seq, _ = x_prompt.shape
    dbatch, t_new, _ = x_sample.shape
    past_a = cache_a_k.shape[2]
    past_b = cache_b_k.shape[2]
    depth = ada_w.shape[0]
    tp, ts = batch * seq, dbatch * t_new
    tm_s = min(TM_PROJ, ts)
    tmd_s = min(TM_DENSE // 2, ts)

    xp = x_prompt.reshape(tp, D_MODEL)
    xs = x_sample.reshape(ts, D_MODEL)
    c_all = jnp.concatenate([c_prompt, c_sample], axis=0)

    cos_p, sin_p = _rope_tables(jnp.arange(seq))
    cos_s, sin_s = _rope_tables(past_a + jnp.arange(t_new))
    cos_s, sin_s = jnp.tile(cos_s, (tm_s // t_new, 1)), jnp.tile(sin_s, (tm_s // t_new, 1))

    outs = {n: [] for n in ("akp", "avp", "aks", "avs", "bkp", "bvp", "bks", "bvs")}
    for i in range(depth):
        j = i // 2
        mod = _ada_mod(c_all, ada_w[i], ada_b[i])
        mod_p = _Mod(mod[:batch].reshape(batch * 6, 1, D_MODEL), False, lambda tm: seq // tm)
        mod_s = _Mod(jnp.repeat(mod[batch:].reshape(dbatch, 6, D_MODEL).transpose(1, 0, 2), t_new, axis=1),
                     True, None)
        ng0, ng1 = norm_g[i, 0].reshape(1, -1), norm_g[i, 1].reshape(1, -1)

        if i % 2 == 0:
            lam_init = 0.8 - 0.6 * math.exp(-0.3 * i)
            w_in = a_w_in[j].astype(BF16)
            gq, gk = _pair_tile(a_g_q[j]), _pair_tile(a_g_k[j])
            lamv = jnp.stack([a_lq1[j], a_lk1[j], a_lq2[j], a_lk2[j]])
            gsub = a_g_sub[j].reshape(1, -1)
            qp, kp, vp = _inproj(xp, mod_p, ng0, w_in, gq, gk, cos_p, sin_p, rope=True, tm=TM_PROJ,
                                 qscale=DH ** -0.5 * _LOG2E)
            qs, ks, vs = _inproj(xs, mod_s, ng0, w_in, gq, gk, cos_s, sin_s, rope=True, tm=tm_s)
            op = _flash_diff(qp, kp, vp, lamv, gsub, batch=batch, seq=seq, lam_init=lam_init)
            os_ = _diff_sample(qs, cache_a_k[j].reshape(dbatch * past_a, D_MODEL),
                               cache_a_v[j].reshape(dbatch * past_a, D_MODEL), ks, vs, lamv, gsub,
                               batch=dbatch, t_new=t_new, past=past_a, lam_init=lam_init)
            w_out = a_w_out[j].astype(BF16)
            heads = D_MODEL // LANES
            k4p, v4p = _cache_layout(kp, vp, tm=TM_PROJ)
            k4s, v4s = _cache_layout(ks, vs, tm=tm_s)
            outs["akp"].append(k4p.reshape(batch, seq, heads, 2, DH))
            outs["avp"].append(v4p.reshape(batch, seq, heads, 2 * DH))
            outs["aks"].append(k4s.reshape(dbatch, t_new, heads, 2, DH))
            outs["avs"].append(v4s.reshape(dbatch, t_new, heads, 2 * DH))
        else:
            w_in = b_w_in[j].astype(BF16)
            gq, gk = _pair_tile(b_g_q[j]), _pair_tile(b_g_k[j])
            qp, kp, vp = _inproj(xp, mod_p, ng0, w_in, gq, gk, cos_p, sin_p, rope=False, tm=TM_PROJ)
            qs, ks, vs = _inproj(xs, mod_s, ng0, w_in, gq, gk, cos_s, sin_s, rope=False, tm=tm_s)
            op = _band_prompt(qp, kp, vp, b_rel_bias[j], batch=batch, seq=seq)
            os_ = _band_sample(qs, cache_b_k[j].reshape(dbatch * past_b, D_MODEL),
                               cache_b_v[j].reshape(dbatch * past_b, D_MODEL), ks, vs, b_rel_bias[j],
                               batch=dbatch, t_new=t_new, past=past_b)
            w_out = b_w_out[j].astype(BF16)
            heads = D_MODEL // DH
            keep = min(BAND_PAST, seq)
            k4 = kp.reshape(batch, seq, heads, DH)
            v4 = vp.reshape(batch, seq, heads, DH)
            outs["bkp"].append(k4[:, seq - keep:])
            outs["bvp"].append(v4[:, seq - keep:])
            outs["bks"].append(jnp.concatenate([cache_b_k[j], ks.reshape(dbatch, t_new, heads, DH)], axis=1)[:, t_new:])
            outs["bvs"].append(jnp.concatenate([cache_b_v[j], vs.reshape(dbatch, t_new, heads, DH)], axis=1)[:, t_new:])

        xp = _outproj(op, w_out, xp, mod_p, tm=TM_PROJ)
        xs = _outproj(os_, w_out, xs, mod_s, tm=tm_s)

        wq = peer_w_q[i].astype(BF16)
        keys = peer_sub_keys[i].astype(BF16).reshape(PEER_HEADS * 2, N_KEYS, -1)
        u_bf, v_bf = peer_u[i].astype(BF16), peer_v[i].astype(BF16)
        xp = _peer(xp, mod_p, ng1, wq, keys, u_bf, v_bf, tm=TM_DENSE)
        xs = _peer(xs, mod_s, ng1, wq, keys, u_bf, v_bf, tm=tmd_s)

    st = lambda n: jnp.stack(outs[n], 0)
    return (xp.reshape(x_prompt.shape), xs.reshape(x_sample.shape),
            st("akp"), st("avp"), st("aks"), st("avs"), st("bkp"), st("bvp"), st("bks"), st("bvs"))
```

```python
import functools
import math

import numpy as np
import jax
import jax.numpy as jnp
from jax import lax
from jax.experimental import pallas as pl
from jax.experimental.pallas import tpu as pltpu

F32 = jnp.float32
BF16 = jnp.bfloat16

D_MODEL = 1024
CHUNK = 64
EPS = 1e-6
NEG = -1e30
ROPE_THETA = 10000.0
DH = 64
LANES = 128
PREV_CHUNKS = 8
BAND_PAST = PREV_CHUNKS * CHUNK
REL_CLIP = 128
PEER_HEADS = 8
N_KEYS = 128
TOPK = 16
VMEM_LIMIT = 48 * 1024 * 1024

TM_PROJ = 1024
TQ_FLASH = 1024
FLASH_ROWS = 1024
TQ_BAND = 256
BAND_QTILES = 8
TM_DENSE = 512
TOKEN_UNROLL = 128
W3_PAD = 8

_NT = (((1,), (1,)), ((), ()))
_LOG2E = math.log2(math.e)


def _cparams(sem):
    return pltpu.CompilerParams(dimension_semantics=sem, vmem_limit_bytes=VMEM_LIMIT)


def _dot(a, b):
    return jnp.dot(a, b, preferred_element_type=F32)


def _dot_nt(a, b):
    return lax.dot_general(a, b, _NT, preferred_element_type=F32)


def _split(a):
    hi = a.astype(BF16)
    lo = (a - hi.astype(F32)).astype(BF16)
    return hi, lo


def _ada_kernel(c_ref, w_ref, b_ref, o_ref):
    c = c_ref[...]
    a = c * (1.0 / (1.0 + jnp.exp(-c)))
    ah, al = _split(a)
    wh, wl = _split(w_ref[...])
    o_ref[...] = _dot(ah, wh) + _dot(al, wh) + _dot(ah, wl) + b_ref[...]


def _ada_mod(c_all, w, b):
    n, d = c_all.shape
    nout = w.shape[1]
    tn = 512
    return pl.pallas_call(
        _ada_kernel,
        grid=(nout // tn,),
        in_specs=[pl.BlockSpec((n, d), lambda j: (0, 0)),
                  pl.BlockSpec((d, tn), lambda j: (0, j)),
                  pl.BlockSpec((1, tn), lambda j: (0, j))],
        out_specs=pl.BlockSpec((n, tn), lambda j: (0, j)),
        out_shape=jax.ShapeDtypeStruct((n, nout), F32),
        compiler_params=_cparams(("parallel",)),
        name="ada_mod",
    )(c_all, w, b.reshape(1, nout))


class _Mod:
    def __init__(self, arr, per_row, tiles_per_batch):
        self.arr, self.per_row, self.tpb = arr, per_row, tiles_per_batch

    def spec(self, k, tm, tile=lambda i: i):
        if self.per_row:
            return pl.BlockSpec((None, tm, D_MODEL), lambda i, *_: (k, tile(i), 0))
        tpb = self.tpb(tm)
        return pl.BlockSpec((None, 1, D_MODEL), lambda i, *_: ((tile(i) // tpb) * 6 + k, 0, 0))


def _modulated(x, ng, scale, shift):
    ms = jnp.mean(x * x, axis=-1, keepdims=True)
    return (x * lax.rsqrt(ms + EPS) * ng) * (1.0 + scale) + shift


def _inproj_kernel(x_ref, sh_ref, sc_ref, ng_ref, w_ref, gq_ref, gk_ref, cos_ref, sin_ref,
                   q_ref, k_ref, v_ref, *, rope, qscale):
    tm = x_ref.shape[0]
    hb = _modulated(x_ref[...], ng_ref[...], sc_ref[...], sh_ref[...]).astype(BF16)
    lane = lax.broadcasted_iota(jnp.int32, (tm, LANES), 1)
    lo = lane < DH
    swap_sel = (lane & (DH // 2)) != 0

    def norm_rope(xb, g):
        x2 = xb * xb
        slo = jnp.sum(jnp.where(lo, x2, 0.0), axis=-1, keepdims=True)
        shi = jnp.sum(jnp.where(lo, 0.0, x2), axis=-1, keepdims=True)
        ms = jnp.where(lo, slo, shi) * (1.0 / DH)
        y = xb * lax.rsqrt(ms + EPS) * g
        if rope:
            sw = jnp.where(swap_sel, pltpu.roll(y, DH // 2, 1), pltpu.roll(y, LANES - DH // 2, 1))
            y = y * cos_ref[...] + sw * sin_ref[...]
        return y

    nblk = w_ref.shape[1] // (2 * LANES)
    for j in range(nblk):
        acc = _dot(hb, w_ref[:, j * 2 * LANES:(j + 1) * 2 * LANES])
        for half in range(2):
            blk = acc[:, half * LANES:(half + 1) * LANES]
            col = j * 2 * LANES + half * LANES
            if col < D_MODEL:
                q_ref[:, col:col + LANES] = (norm_rope(blk, gq_ref[...]) * qscale).astype(BF16)
            elif col < 2 * D_MODEL:
                k_ref[:, col - D_MODEL:col - D_MODEL + LANES] = norm_rope(blk, gk_ref[...])
            else:
                v_ref[:, col - 2 * D_MODEL:col - 2 * D_MODEL + LANES] = blk


def _inproj(x, mod, ng, w_bf, gq, gk, cos, sin, *, rope, tm, qscale=DH ** -0.5):
    t = x.shape[0]
    nrep = cos.shape[0] // tm
    row = lambda i: (i, 0)
    full = lambda i: (0, 0)
    tab = lambda i: (i % nrep, 0)
    return pl.pallas_call(
        functools.partial(_inproj_kernel, rope=rope, qscale=qscale),
        grid=(t // tm,),
        in_specs=[pl.BlockSpec((tm, D_MODEL), row), mod.spec(0, tm), mod.spec(1, tm),
                  pl.BlockSpec((1, D_MODEL), full),
                  pl.BlockSpec(w_bf.shape, full),
                  pl.BlockSpec((1, LANES), full), pl.BlockSpec((1, LANES), full),
                  pl.BlockSpec((tm, LANES), tab), pl.BlockSpec((tm, LANES), tab)],
        out_specs=[pl.BlockSpec((tm, D_MODEL), row)] * 3,
        out_shape=[jax.ShapeDtypeStruct((t, D_MODEL), BF16),
                   jax.ShapeDtypeStruct((t, D_MODEL), F32),
                   jax.ShapeDtypeStruct((t, D_MODEL), F32)],
        compiler_params=_cparams(("parallel",)),
        name="qkv_proj",
    )(x, mod.arr, mod.arr, ng, w_bf, gq, gk, cos, sin)


def _cache_layout_kernel(k_ref, v_ref, k4_ref, v4_ref):
    heads = v4_ref.shape[1]
    for h in range(heads):
        kb = k_ref[:, h * LANES:(h + 1) * LANES]
        v4_ref[:, h, :] = v_ref[:, h * LANES:(h + 1) * LANES]
        for c in range(2):
            k4_ref[:, h, c, :] = kb[:, c * DH:(c + 1) * DH]


def _cache_layout(k, v, *, tm):
    t = k.shape[0]
    heads = D_MODEL // LANES
    row = lambda i: (i, 0)
    return pl.pallas_call(
        _cache_layout_kernel,
        grid=(t // tm,),
        in_specs=[pl.BlockSpec((tm, D_MODEL), row)] * 2,
        out_specs=[pl.BlockSpec((tm, heads, 2, DH), lambda i: (i, 0, 0, 0)),
                   pl.BlockSpec((tm, heads, LANES), lambda i: (i, 0, 0))],
        out_shape=[jax.ShapeDtypeStruct((t, heads, 2, DH), F32), jax.ShapeDtypeStruct((t, heads, LANES), F32)],
        compiler_params=_cparams(("parallel",)),
        name="cache_layout",
    )(k, v)


def _diff_lambda(lamv_ref, lam_init):
    lv = lamv_ref[...]
    e1 = jnp.exp(jnp.sum(lv[0:1] * lv[1:2], axis=-1, keepdims=True))
    e2 = jnp.exp(jnp.sum(lv[2:3] * lv[3:4], axis=-1, keepdims=True))
    return e1 - e2 + lam_init


def _diff_finish(o0, o1, lamv_ref, gsub_ref, lam_init):
    o = o0 - _diff_lambda(lamv_ref, lam_init) * o1
    ms = jnp.mean(o * o, axis=-1, keepdims=True)
    return ((o * lax.rsqrt(ms + EPS) * gsub_ref[...]) * (1.0 - lam_init)).astype(BF16)


def _component_queries(q):
    lane = lax.broadcasted_iota(jnp.int32, q.shape, 1)
    zero = jnp.zeros_like(q)
    return jnp.where(lane < DH, q, zero), jnp.where(lane < DH, zero, q)


def _flash_diff_kernel(qt_ref, kt_ref, q_ref, k_ref, v_ref, lamv_ref, gsub_ref, o_ref,
                       m_ref, l_ref, a_ref, *, lam_init):
    p = pl.program_id(2)
    qi = qt_ref[p]
    ki = kt_ref[p]
    tq = q_ref.shape[0]

    @pl.when(ki == 0)
    def _():
        m_ref[...] = jnp.full(m_ref.shape, NEG, F32)
        l_ref[...] = jnp.zeros(l_ref.shape, F32)
        a_ref[...] = jnp.zeros(a_ref.shape, F32)

    def update(masked):
        kb = k_ref[...].astype(BF16)
        vb = v_ref[...].astype(BF16)
        qq = jnp.concatenate(_component_queries(q_ref[...]), axis=0)
        for r in range(2 * tq // FLASH_ROWS):
            rows = slice(r * FLASH_ROWS, (r + 1) * FLASH_ROWS)
            q0 = (r * FLASH_ROWS) % tq
            ncol = min(q0 + FLASH_ROWS, tq) if masked else tq
            s = _dot_nt(qq[rows], kb[:ncol])
            if masked:
                row_c = ((lax.broadcasted_iota(jnp.int32, (FLASH_ROWS, ncol), 0) + q0) % tq) // CHUNK
                col_c = lax.broadcasted_iota(jnp.int32, (FLASH_ROWS, ncol), 1) // CHUNK
                s = jnp.where(col_c <= row_c, s, NEG)
            m_prev = m_ref[rows, :]
            m_new = jnp.maximum(m_prev, jnp.max(s, axis=-1, keepdims=True))
            alpha = jnp.exp2(m_prev - m_new)
            pm = jnp.exp2(s - jnp.tile(m_new, (1, ncol // LANES)))
            l_ref[rows, :] = alpha * l_ref[rows, :] + jnp.sum(pm, axis=-1, keepdims=True)
            a_ref[rows, :] = alpha * a_ref[rows, :] + _dot(pm.astype(BF16), vb[:ncol])
            m_ref[rows, :] = m_new

    @pl.when(ki < qi)
    def _():
        update(False)

    @pl.when(ki == qi)
    def _():
        update(True)
        o = a_ref[...] / l_ref[...]
        o_ref[...] = _diff_finish(o[:tq], o[tq:], lamv_ref, gsub_ref, lam_init)


def _flash_diff(q, k, v, lamv, gsub, *, batch, seq, lam_init):
    tq = min(TQ_FLASH, seq)
    nq = seq // tq
    heads = D_MODEL // LANES
    pairs = [(a, b) for a in range(nq) for b in range(a + 1)]
    qt = jnp.asarray([a for a, _ in pairs], jnp.int32)
    kt = jnp.asarray([b for _, b in pairs], jnp.int32)
    qmap = lambda b, h, p, qt, kt: (b * nq + qt[p], h)
    kmap = lambda b, h, p, qt, kt: (b * nq + kt[p], h)
    return pl.pallas_call(
        functools.partial(_flash_diff_kernel, lam_init=lam_init),
        grid_spec=pltpu.PrefetchScalarGridSpec(
            num_scalar_prefetch=2,
            grid=(batch, heads, len(pairs)),
            in_specs=[pl.BlockSpec((tq, LANES), qmap),
                      pl.BlockSpec((tq, LANES), kmap),
                      pl.BlockSpec((tq, LANES), kmap),
                      pl.BlockSpec(lamv.shape, lambda *_: (0, 0)),
                      pl.BlockSpec((1, LANES), lambda *_: (0, 0))],
            out_specs=pl.BlockSpec((tq, LANES), qmap),
            scratch_shapes=[pltpu.VMEM((2 * tq, LANES), F32)] * 3),
        out_shape=jax.ShapeDtypeStruct(q.shape, BF16),
        compiler_params=_cparams(("parallel", "parallel", "arbitrary")),
        name="flash_diff_attn",
    )(qt, kt, q, k, v, lamv, gsub)


def _joint_softmax_attend(qc, pieces):
    ss = []
    for kb, _, bias, visible in pieces:
        s = _dot_nt(qc, kb)
        if bias is not None:
            s = s + bias
        if visible is not None:
            s = jnp.where(visible, s, NEG)
        ss.append(s)
    m = functools.reduce(jnp.maximum, [jnp.max(s, axis=-1, keepdims=True) for s in ss])
    l = 0.0
    o = 0.0
    for s, (_, vb, _, _) in zip(ss, pieces):
        pm = jnp.exp(s - m)
        l = l + jnp.sum(pm, axis=-1, keepdims=True)
        o = o + _dot(pm.astype(BF16), vb)
    return o / l


def _diff_sample_kernel(q_ref, ck_ref, cv_ref, nk_ref, nv_ref, lamv_ref, gsub_ref, o_ref, *, lam_init):
    t_new = q_ref.shape[0]
    pieces = [(ck_ref[...].astype(BF16), cv_ref[...].astype(BF16), None, None),
              (nk_ref[...].astype(BF16), nv_ref[...].astype(BF16), None, None)]
    o = _joint_softmax_attend(jnp.concatenate(_component_queries(q_ref[...]), axis=0), pieces)
    o_ref[...] = _diff_finish(o[:t_new], o[t_new:], lamv_ref, gsub_ref, lam_init)


def _diff_sample(q, ck, cv, nk, nv, lamv, gsub, *, batch, t_new, past, lam_init):
    heads = D_MODEL // LANES
    bh = lambda b, h: (b, h)
    return pl.pallas_call(
        functools.partial(_diff_sample_kernel, lam_init=lam_init),
        grid=(batch, heads),
        in_specs=[pl.BlockSpec((t_new, LANES), bh),
                  pl.BlockSpec((past, LANES), bh), pl.BlockSpec((past, LANES), bh),
                  pl.BlockSpec((t_new, LANES), bh), pl.BlockSpec((t_new, LANES), bh),
                  pl.BlockSpec(lamv.shape, lambda b, h: (0, 0)),
                  pl.BlockSpec((1, LANES), lambda b, h: (0, 0))],
        out_specs=pl.BlockSpec((t_new, LANES), bh),
        out_shape=jax.ShapeDtypeStruct(q.shape, BF16),
        compiler_params=_cparams(("parallel", "parallel")),
        name="diff_attn_sample",
    )(q, ck, cv, nk, nv, lamv, gsub)


def _band_finish(o):
    tq = o.shape[0] // 2
    lane = lax.broadcasted_iota(jnp.int32, (tq, LANES), 1)
    return jnp.where(lane < DH, o[:tq], o[tq:]).astype(BF16)


def _band_prompt_kernel(q_ref, *refs):
    nkv = (len(refs) - 2) // 2
    nsub = nkv - 2
    k_refs, v_refs, bias_ref, o_ref = refs[:nkv], refs[nkv:2 * nkv], refs[2 * nkv], refs[2 * nkv + 1]
    first = pl.program_id(2) * nsub
    tq = q_ref.shape[0] // nsub
    row_c = (lax.broadcasted_iota(jnp.int32, (2 * tq, tq), 0) % tq) // CHUNK
    col_c = lax.broadcasted_iota(jnp.int32, (2 * tq, tq), 1) // CHUNK
    kvs = [(k_ref[...].astype(BF16), v_ref[...].astype(BF16)) for k_ref, v_ref in zip(k_refs, v_refs)]
    for sub in range(nsub):
        pieces = []
        for r in range(3):
            before_start = jnp.where(first + sub + r - 2 < 0, 4 * PREV_CHUNKS, 0)
            kc = col_c + (r - 2) * (tq // CHUNK) + before_start
            visible = (kc <= row_c) & (kc >= row_c - PREV_CHUNKS)
            pieces.append((*kvs[sub + r], bias_ref[:, r * tq:(r + 1) * tq], visible))
        rows = slice(sub * tq, (sub + 1) * tq)
        qq = jnp.concatenate(_component_queries(q_ref[rows, :]), axis=0)
        o_ref[rows, :] = _band_finish(_joint_softmax_attend(qq, pieces))


def _band_bias_tiles(table, nq, nk, c0):
    n = nq + nk - 1
    m = np.arange(n)
    offset = np.where(m < nk, -m, n - m)
    e = table[:, np.clip(c0 + offset, -REL_CLIP, REL_CLIP) + REL_CLIP]
    a = jnp.tile(e, (1, nq))[:, :nq * (n - 1)].reshape(-1, nq, n - 1)[:, :, :nk]
    return a.reshape(table.shape[0] // 2, 2 * nq, nk)


def _band_prompt(q, k, v, table, *, batch, seq):
    tq = TQ_BAND
    nsub = math.gcd(BAND_QTILES, seq // tq)
    assert BAND_PAST == 2 * tq and seq % (nsub * tq) == 0
    nq = seq // tq
    hp = D_MODEL // LANES
    bias = _band_bias_tiles(table, tq, 3 * tq, 2 * tq)
    qmap = lambda h, b, i: (b * (nq // nsub) + i, h)
    kmap = lambda r: (lambda h, b, i: (b * nq + jnp.maximum(nsub * i + r - 2, 0), h))
    kv_specs = [pl.BlockSpec((tq, LANES), kmap(r)) for r in range(nsub + 2)]
    return pl.pallas_call(
        _band_prompt_kernel,
        grid=(hp, batch, nq // nsub),
        in_specs=[pl.BlockSpec((nsub * tq, LANES), qmap)] + kv_specs * 2
                 + [pl.BlockSpec((None, 2 * tq, 3 * tq), lambda h, b, i: (h, 0, 0))],
        out_specs=pl.BlockSpec((nsub * tq, LANES), qmap),
        out_shape=jax.ShapeDtypeStruct(q.shape, BF16),
        compiler_params=_cparams(("parallel", "parallel", "parallel")),
        name="band_attn_prompt",
    )(q, *([k] * (nsub + 2)), *([v] * (nsub + 2)), bias)


def _band_sample_kernel(q_ref, ck_ref, cv_ref, nk_ref, nv_ref, bc_ref, bn_ref, o_ref):
    pieces = [(ck_ref[...].astype(BF16), cv_ref[...].astype(BF16), bc_ref[...], None),
              (nk_ref[...].astype(BF16), nv_ref[...].astype(BF16), bn_ref[...], None)]
    qq = jnp.concatenate(_component_queries(q_ref[...]), axis=0)
    o_ref[...] = _band_finish(_joint_softmax_attend(qq, pieces))


def _band_sample(q, ck, cv, nk, nv, table, *, batch, t_new, past):
    hp = D_MODEL // LANES
    bias = _band_bias_tiles(table, t_new, past + t_new, past)
    bias_c, bias_n = bias[..., :past], bias[..., past:]
    hb = lambda h, b: (b, h)
    return pl.pallas_call(
        _band_sample_kernel,
        grid=(hp, batch),
        in_specs=[pl.BlockSpec((t_new, LANES), hb),
                  pl.BlockSpec((past, LANES), hb), pl.BlockSpec((past, LANES), hb),
                  pl.BlockSpec((t_new, LANES), hb), pl.BlockSpec((t_new, LANES), hb),
                  pl.BlockSpec((None, 2 * t_new, past), lambda h, b: (h, 0, 0)),
                  pl.BlockSpec((None, 2 * t_new, t_new), lambda h, b: (h, 0, 0))],
        out_specs=pl.BlockSpec((t_new, LANES), hb),
        out_shape=jax.ShapeDtypeStruct(q.shape, BF16),
        compiler_params=_cparams(("parallel", "parallel")),
        name="band_attn_sample",
    )(q, ck, cv, nk, nv, bias_c, bias_n)


def _outproj_kernel(o_ref, w_ref, x_ref, gate_ref, y_ref):
    y_ref[...] = x_ref[...] + gate_ref[...] * _dot(o_ref[...], w_ref[...])


def _outproj(o, w_bf, x, mod, *, tm):
    t = x.shape[0]
    row = lambda i: (i, 0)
    return pl.pallas_call(
        _outproj_kernel,
        grid=(t // tm,),
        in_specs=[pl.BlockSpec((tm, D_MODEL), row), pl.BlockSpec(w_bf.shape, lambda i: (0, 0)),
                  pl.BlockSpec((tm, D_MODEL), row), mod.spec(2, tm)],
        out_specs=pl.BlockSpec((tm, D_MODEL), row),
        out_shape=jax.ShapeDtypeStruct(x.shape, F32),
        compiler_params=_cparams(("parallel",)),
        name="out_proj",
    )(o, w_bf, x, mod.arr)


def _top_rows(s, k):
    if s.shape[1] > LANES:
        cols = [_top_rows(s[:, c:c + LANES], k) for c in range(0, s.shape[1], LANES)]
        return tuple(jnp.concatenate(x, axis=1) for x in zip(*cols))
    rows = s.shape[0]
    rid = lax.broadcasted_iota(jnp.int32, s.shape, 0).astype(F32)
    vals, ids = [], []
    for r in range(k):
        cand = [(s[v:v + _SUB], rid[v:v + _SUB]) for v in range(0, rows, _SUB)]
        while len(cand) > 1:
            nxt = []
            for a in range(0, len(cand) - 1, 2):
                (va, ia), (vb, ib) = cand[a], cand[a + 1]
                keep = va >= vb
                nxt.append((jnp.where(keep, va, vb), jnp.where(keep, ia, ib)))
            cand = nxt + cand[len(cand) - len(cand) % 2:]
        v8, i8 = cand[0]
        m = jnp.max(v8, axis=0, keepdims=True)
        i = jnp.min(jnp.where(v8 == m, i8, float(rows)), axis=0, keepdims=True)
        vals.append(m)
        ids.append(i)
        if r + 1 < k:
            s = jnp.where(rid == i, -jnp.inf, s)
    return jnp.concatenate(vals, axis=0), jnp.concatenate(ids, axis=0)


_SUB = 8
_STAIR_PIECES = ([(0, 1, 0, _SUB), (0, 1, _SUB, _SUB), (1, 1, 0, _SUB)]
                 + [(a, 1, 0, TOPK // (a + 1)) for a in range(2, _SUB)] + [(_SUB, _SUB, 0, 1)])


def _stair_candidates(s1, s2):
    sub = lax.broadcasted_iota(jnp.int32, (_SUB, s1.shape[1]), 0)
    pieces = []
    for a0, na, b0, nb in _STAIR_PIECES:
        if na == 1:
            piece = s1[a0:a0 + 1, :] + s2[b0:b0 + _SUB, :]
            if nb < _SUB:
                piece = jnp.where(sub < nb, piece, -jnp.inf)
        else:
            piece = s1[a0:a0 + na, :] + s2[b0:b0 + 1, :]
        pieces.append(piece)
    return jnp.concatenate(pieces, axis=0)


def _stair_ranks(pos):
    a = jnp.zeros(pos.shape, F32)
    b = pos
    for p, (a0, na, b0, nb) in enumerate(_STAIR_PIECES):
        start = float(p * _SUB)
        inside = pos >= start
        if na == 1:
            a = jnp.where(inside, float(a0), a)
            b = jnp.where(inside, pos - start + float(b0), b)
        else:
            a = jnp.where(inside, pos - start + float(a0), a)
            b = jnp.where(inside, float(b0), b)
    return a, b


def _pick_rows(sel, table):
    out = jnp.zeros(sel.shape, F32)
    for a in range(table.shape[0]):
        out = out + jnp.where(sel == float(a), table[a:a + 1, :], 0.0)
    return out


def _select_unit(q_scr, keys_ref, sel_t_scr, h, part):
    tokens = pl.ds(pl.multiple_of(part * LANES, LANES), LANES)
    top = [_top_rows(_dot_nt(keys_ref[2 * h + c], q_scr[2 * h + c, tokens, :]), TOPK) for c in range(2)]
    (s1, i1), (s2, i2) = top
    top_s, pos = _top_rows(_stair_candidates(s1, s2), TOPK)
    a_sel, b_sel = _stair_ranks(pos)
    e = jnp.exp(top_s - top_s[0:1, :])
    rows = pl.ds(pl.multiple_of(h * TOPK, TOPK), TOPK)
    sel_t_scr[0, part, rows, :] = _pick_rows(a_sel, i1)
    sel_t_scr[1, part, rows, :] = _pick_rows(b_sel, i2)
    sel_t_scr[2, part, rows, :] = e / jnp.sum(e, axis=0, keepdims=True)


def _peer_kernel(xs_ref, sh_ref, sc_ref, ng_ref, wq_ref, keys_ref, u_ref, v_ref, xr_ref, gate_ref, y_ref,
                 hb_scr, q_scr, sel_t_scr, sel_scr, w3_ref, acc_ref):
    i = pl.program_id(0)
    j = pl.program_id(1)
    tm = xs_ref.shape[0]
    nparts = tm // LANES
    half = N_KEYS // 2
    pitch = w3_ref.shape[0] // half
    e_tile = u_ref.shape[0]
    slot_new, slot_dense = i % 2, (i + 1) % 2

    @pl.when((i == 0) & (j == 0))
    def _():
        w3_ref[...] = jnp.zeros(w3_ref.shape, w3_ref.dtype)
        hb_scr[1] = jnp.zeros(hb_scr.shape[1:], hb_scr.dtype)

    @pl.when((i > 0) & (j == 0))
    def _():
        for k in range(3):
            for part in range(nparts):
                sel_scr[k, part * LANES:(part + 1) * LANES, :] = sel_t_scr[k, part].T
        row = lax.broadcasted_iota(jnp.int32, (N_KEYS, N_KEYS), 0)
        i1_ids = jnp.where(row < half, 2 * row, 2 * (row - half) + 1).astype(F32)
        i2_ids = row.astype(F32)

        def token(t, carry):
            arow = sel_scr[0, pl.ds(t, 1), :]
            brow = sel_scr[1, pl.ds(t, 1), :]
            grow = sel_scr[2, pl.ds(t, 1), :]
            oa = jnp.where(arow == i1_ids, 1.0, 0.0).astype(BF16)
            ob = jnp.where(brow == i2_ids, 0.5 * grow, 0.0).astype(BF16)
            w = _dot_nt(oa, ob)
            w3_ref[pl.ds(t, half, stride=pitch), :] = pltpu.pack_elementwise([w[:half], w[half:]], packed_dtype=BF16)
            return carry

        lax.fori_loop(0, tm, token, 0, unroll=TOKEN_UNROLL)

    @pl.when(j == 0)
    def _():
        acc_ref[...] = jnp.zeros(acc_ref.shape, F32)
        hb = _modulated(xs_ref[...], ng_ref[...], sc_ref[...], sh_ref[...]).astype(BF16)
        hb_scr[slot_new] = hb
        q_all = _dot(hb, wq_ref[...])
        for hc in range(2 * PEER_HEADS):
            q_scr[hc] = q_all[:, hc * N_KEYS:(hc + 1) * N_KEYS].astype(BF16)

    _select_unit(q_scr, keys_ref, sel_t_scr, j // nparts, j % nparts)

    cols = []
    for q in range(e_tile // (2 * N_KEYS)):
        word = w3_ref[pl.ds(pl.multiple_of((e_tile // (2 * N_KEYS) * j + q) * pitch, 8), tm), :]
        cols += [pltpu.unpack_elementwise(word, index=k, packed_dtype=BF16, unpacked_dtype=F32) for k in range(2)]
    w = jnp.concatenate(cols, axis=1)
    hid = _dot_nt(hb_scr[slot_dense], u_ref[...])
    act = hid * (1.0 + lax.erf(hid * math.sqrt(0.5)))
    acc_ref[...] += _dot((w * act).astype(BF16), v_ref[...])

    @pl.when((i > 0) & (j == pl.num_programs(1) - 1))
    def _():
        y_ref[...] = xr_ref[...] + gate_ref[...] * acc_ref[...]


def _peer(x, mod, ng, wq_bf, keys_bf, u_bf, v_bf, *, tm):
    t = x.shape[0]
    nt = t // tm
    units = PEER_HEADS * (tm // LANES)
    e_tile = u_bf.shape[0] // units
    assert e_tile % (2 * N_KEYS) == 0 and tm % TOKEN_UNROLL == 0
    new_tile = lambda i: jnp.minimum(i, nt - 1)
    dense_tile = lambda i: jnp.maximum(i - 1, 0)
    full = lambda i, j: (0, 0)
    return pl.pallas_call(
        _peer_kernel,
        grid=(nt + 1, units),
        in_specs=[pl.BlockSpec((tm, D_MODEL), lambda i, j: (new_tile(i), 0)),
                  mod.spec(3, tm, new_tile), mod.spec(4, tm, new_tile),
                  pl.BlockSpec((1, D_MODEL), full),
                  pl.BlockSpec(wq_bf.shape, full),
                  pl.BlockSpec(keys_bf.shape, lambda i, j: (0, 0, 0)),
                  pl.BlockSpec((e_tile, D_MODEL), lambda i, j: (j, 0)),
                  pl.BlockSpec((e_tile, D_MODEL), lambda i, j: (j, 0)),
                  pl.BlockSpec((tm, D_MODEL), lambda i, j: (dense_tile(i), 0)),
                  mod.spec(5, tm, dense_tile)],
        out_specs=pl.BlockSpec((tm, D_MODEL), lambda i, j: (dense_tile(i), 0)),
        out_shape=jax.ShapeDtypeStruct(x.shape, F32),
        scratch_shapes=[pltpu.VMEM((2, tm, D_MODEL), BF16),
                        pltpu.VMEM((2 * PEER_HEADS, tm, N_KEYS), BF16),
                        pltpu.VMEM((3, tm // LANES, PEER_HEADS * TOPK, LANES), F32),
                        pltpu.VMEM((3, tm, PEER_HEADS * TOPK), F32),
                        pltpu.VMEM((N_KEYS // 2 * (tm + W3_PAD), N_KEYS), jnp.int32),
                        pltpu.VMEM((tm, D_MODEL), F32)],
        compiler_params=_cparams(("arbitrary", "arbitrary")),
        name="peer",
    )(x, mod.arr, mod.arr, ng, wq_bf, keys_bf, u_bf, v_bf, x, mod.arr)


def _rope_tables(pos):
    half = DH // 2
    inv = ROPE_THETA ** (-jnp.arange(half, dtype=F32) / half)
    ang = pos.astype(F32)[:, None] * inv[None, :]
    cos, sin = jnp.cos(ang), jnp.sin(ang)
    return jnp.tile(cos, (1, 4)), jnp.concatenate([-sin, sin, -sin, sin], axis=1)


def _pair_tile(g):
    return jnp.tile(g.reshape(1, -1), (1, LANES // g.shape[-1]))


def kernel(x_prompt, x_sample, c_prompt, c_sample, cache_a_k, cache_a_v, cache_b_k, cache_b_v, ada_w, ada_b, norm_g, a_w_in, a_g_q, a_g_k, a_lq1, a_lk1, a_lq2, a_lk2, a_g_sub, a_w_out, b_w_in, b_g_q, b_g_k, b_rel_bias, b_w_out, peer_w_q, peer_sub_keys, peer_u, peer_v):
    batch, seq, _ = x_prompt.shape
    dbatch, t_new, _ = x_sample.shape
    past_a = cache_a_k.shape[2]
    past_b = cache_b_k.shape[2]
    depth = ada_w.shape[0]
    tp, ts = batch * seq, dbatch * t_new
    tm_s = min(TM_PROJ, ts)
    tmd_s = min(TM_DENSE // 2, ts)

    xp = x_prompt.reshape(tp, D_MODEL)
    xs = x_sample.reshape(ts, D_MODEL)
    c_all = jnp.concatenate([c_prompt, c_sample], axis=0)

    cos_p, sin_p = _rope_tables(jnp.arange(seq))
    cos_s, sin_s = _rope_tables(past_a + jnp.arange(t_new))
    cos_s, sin_s = jnp.tile(cos_s, (tm_s // t_new, 1)), jnp.tile(sin_s, (tm_s // t_new, 1))

    outs = {n: [] for n in ("akp", "avp", "aks", "avs", "bkp", "bvp", "bks", "bvs")}
    for i in range(depth):
        j = i // 2
        mod = _ada_mod(c_all, ada_w[i], ada_b[i])
        mod_p = _Mod(mod[:batch].reshape(batch * 6, 1, D_MODEL), False, lambda tm: seq // tm)
        mod_s = _Mod(jnp.repeat(mod[batch:].reshape(dbatch, 6, D_MODEL).transpose(1, 0, 2), t_new, axis=1),
                     True, None)
        ng0, ng1 = norm_g[i, 0].reshape(1, -1), norm_g[i, 1].reshape(1, -1)

        if i % 2 == 0:
            lam_init = 0.8 - 0.6 * math.exp(-0.3 * i)
            w_in = a_w_in[j].astype(BF16)
            gq, gk = _pair_tile(a_g_q[j]), _pair_tile(a_g_k[j])
            lamv = jnp.stack([a_lq1[j], a_lk1[j], a_lq2[j], a_lk2[j]])
            gsub = a_g_sub[j].reshape(1, -1)
            qp, kp, vp = _inproj(xp, mod_p, ng0, w_in, gq, gk, cos_p, sin_p, rope=True, tm=TM_PROJ,
                                 qscale=DH ** -0.5 * _LOG2E)
            qs, ks, vs = _inproj(xs, mod_s, ng0, w_in, gq, gk, cos_s, sin_s, rope=True, tm=tm_s)
            op = _flash_diff(qp, kp, vp, lamv, gsub, batch=batch, seq=seq, lam_init=lam_init)
            os_ = _diff_sample(qs, cache_a_k[j].reshape(dbatch * past_a, D_MODEL),
                               cache_a_v[j].reshape(dbatch * past_a, D_MODEL), ks, vs, lamv, gsub,
                               batch=dbatch, t_new=t_new, past=past_a, lam_init=lam_init)
            w_out = a_w_out[j].astype(BF16)
            heads = D_MODEL // LANES
            k4p, v4p = _cache_layout(kp, vp, tm=TM_PROJ)
            k4s, v4s = _cache_layout(ks, vs, tm=tm_s)
            outs["akp"].append(k4p.reshape(batch, seq, heads, 2, DH))
            outs["avp"].append(v4p.reshape(batch, seq, heads, 2 * DH))
            outs["aks"].append(k4s.reshape(dbatch, t_new, heads, 2, DH))
            outs["avs"].append(v4s.reshape(dbatch, t_new, heads, 2 * DH))
        else:
            w_in = b_w_in[j].astype(BF16)
            gq, gk = _pair_tile(b_g_q[j]), _pair_tile(b_g_k[j])
            qp, kp, vp = _inproj(xp, mod_p, ng0, w_in, gq, gk, cos_p, sin_p, rope=False, tm=TM_PROJ)
            qs, ks, vs = _inproj(xs, mod_s, ng0, w_in, gq, gk, cos_s, sin_s, rope=False, tm=tm_s)
            op = _band_prompt(qp, kp, vp, b_rel_bias[j], batch=batch, seq=seq)
            os_ = _band_sample(qs, cache_b_k[j].reshape(dbatch * past_b, D_MODEL),
                               cache_b_v[j].reshape(dbatch * past_b, D_MODEL), ks, vs, b_rel_bias[j],
                               batch=dbatch, t_new=t_new, past=past_b)
            w_out = b_w_out[j].astype(BF16)
            heads = D_MODEL // DH
            keep = min(BAND_PAST, seq)
            k4 = kp.reshape(batch, seq, heads, DH)
            v4 = vp.reshape(batch, seq, heads, DH)
            outs["bkp"].append(k4[:, seq - keep:])
            outs["bvp"].append(v4[:, seq - keep:])
            outs["bks"].append(jnp.concatenate([cache_b_k[j], ks.reshape(dbatch, t_new, heads, DH)], axis=1)[:, t_new:])
            outs["bvs"].append(jnp.concatenate([cache_b_v[j], vs.reshape(dbatch, t_new, heads, DH)], axis=1)[:, t_new:])

        xp = _outproj(op, w_out, xp, mod_p, tm=TM_PROJ)
        xs = _outproj(os_, w_out, xs, mod_s, tm=tm_s)

        wq = peer_w_q[i].astype(BF16)
        keys = peer_sub_keys[i].astype(BF16).reshape(PEER_HEADS * 2, N_KEYS, -1)
        u_bf, v_bf = peer_u[i].astype(BF16), peer_v[i].astype(BF16)
        xp = _peer(xp, mod_p, ng1, wq, keys, u_bf, v_bf, tm=TM_DENSE)
        xs = _peer(xs, mod_s, ng1, wq, keys, u_bf, v_bf, tm=tmd_s)

    st = lambda n: jnp.stack(outs[n], 0)
    return (xp.reshape(x_prompt.shape), xs.reshape(x_sample.shape),
            st("akp"), st("avp"), st("aks"), st("avs"), st("bkp"), st("bvp"), st("bks"), st("bvs"))
```
